```python
import math
import jax, jax.numpy as jnp
from jax import lax
import numpy as np

D_MODEL = 1024
BATCH = 8
SEQ = 2048
DEPTH = 4
DEC_BATCH = 128
DEC_SEQ = 4
PAST_LEN = 16384
PAGE_SIZE = 128

N_MIXERS = 4
RW_HD = 64
RW_HEADS = D_MODEL // RW_HD
RW_DECAY_LORA = 64
RW_AAA_LORA = 64
RW_GATE_LORA = 128
RW_GN_EPS = 64e-5
GD_DK = 128
GD_DV = 128
GD_HEADS = D_MODEL // GD_DK
GD_CONV = 4
GD_CHUNK = 64
RT_DK = 256
RT_HEADS = D_MODEL // RT_DK
RT_DV = 2 * RT_DK
RT_CHUNK = 64
HG_EXPAND = 128
HG_HEADS = D_MODEL // HG_EXPAND
HG_DV = D_MODEL // HG_HEADS
HG_CHUNK = 32
N_EXPERTS = 32
TOP_K = 4
D_FF = D_MODEL
SWIGLU_LIMIT = 7.0
SWIGLU_ALPHA = 1.702
NORM_EPS = 1e-6
L_RWKV = len(range(0, DEPTH, N_MIXERS))
L_GDN = len(range(1, DEPTH, N_MIXERS))
L_RET = len(range(2, DEPTH, N_MIXERS))
L_HGRN = len(range(3, DEPTH, N_MIXERS))

kernel_name = 'hybrid_rwkv7_gdn_retnet_hgrn2_moe_adaln_step'


def _rms(x, eps=NORM_EPS):
    xf = x.astype(jnp.float32)
    return xf * lax.rsqrt(jnp.mean(xf * xf, axis=-1, keepdims=True) + eps)


def rms_norm(x, g):
    return (_rms(x) * g.astype(jnp.float32)).astype(x.dtype)


def l2norm(x, eps=1e-6):
    xf = x.astype(jnp.float32)
    return xf * lax.rsqrt(jnp.sum(xf * xf, axis=-1, keepdims=True) + eps)


def split_heads(x, h):
    return x.reshape(x.shape[:-1] + (h, x.shape[-1] // h))


def to_chunks(x, c):
    B, T = x.shape[:2]
    x = x.reshape((B, T // c, c) + x.shape[2:])
    return jnp.moveaxis(x, (1, 3), (0, 2))


def from_chunks(x):
    n, B, H, c = x.shape[:4]
    x = jnp.moveaxis(x, (0, 2), (1, 3))
    return x.reshape((B, n * c, H) + x.shape[4:])


def adaln(c, w, b):
    m = (jax.nn.silu(c) @ w + b).reshape(c.shape[0], 6, D_MODEL)
    return tuple(m[:, n, None, :] for n in range(6))


def modulate(x, g, shift, scale):
    return rms_norm(x, g) * (1.0 + scale) + shift


def rotary_every_two(x, pos):
    half = x.shape[-1] // 2
    inv = 1.0 / (10000.0 ** jnp.linspace(0.0, 1.0, half, dtype=jnp.float32))
    ang = pos[:, None] * inv[None, :]
    cos = jnp.cos(ang)[None, :, None, :]
    sin = jnp.sin(ang)[None, :, None, :]
    x1, x2 = x[..., 0::2], x[..., 1::2]
    return jnp.stack([x1 * cos - x2 * sin, x1 * sin + x2 * cos], axis=-1).reshape(x.shape)


def rwkv7_mix(h, shift0, wkv0, mu, w_rkv, w0, w1, w2, a0, a1, a2, g1, g2, k_k, k_a, r_k, ln_w, ln_b, w_o):
    B, T, D = h.shape
    H, N = RW_HEADS, RW_HD
    prev = jnp.concatenate([shift0[:, None, :].astype(h.dtype), h[:, :-1]], axis=1)
    dx = prev - h
    xr, xw, xk, xv, xa, xg = [h + dx * mu[n] for n in range(6)]
    r = xr @ w_rkv[0]
    k = xk @ w_rkv[1]
    v = xv @ w_rkv[2]
    w_log = -jax.nn.softplus(-(w0 + jnp.tanh(xw @ w1) @ w2)) - 0.5
    decay = jnp.exp(-jnp.exp(w_log.astype(jnp.float32)))
    a = jax.nn.sigmoid(a0 + (xa @ a1) @ a2)
    g = jax.nn.sigmoid(xg @ g1) @ g2
    kk = l2norm(split_heads(k * k_k, H))
    k = k * (1.0 + (a - 1.0) * k_a)
    r_, k_, v_, a_, w_ = [split_heads(z, H).astype(jnp.float32) for z in (r, k, v, a, decay)]

    def step(S, inp):
        r_t, w_t, k_t, v_t, kk_t, a_t = inp
        sa = jnp.einsum('bhvk,bhk->bhv', S, kk_t)
        S = (S * w_t[:, :, None, :] - sa[..., None] * (kk_t * a_t)[:, :, None, :]
             + v_t[..., None] * k_t[:, :, None, :])
        return S, jnp.einsum('bhvk,bhk->bhv', S, r_t)

    xs = tuple(jnp.moveaxis(z, 1, 0) for z in (r_, w_, k_, v_, kk, a_))
    S, y = lax.scan(step, wkv0.astype(jnp.float32), xs)
    y = jnp.moveaxis(y, 0, 1)
    mean = jnp.mean(y, axis=-1, keepdims=True)
    var = jnp.mean(jnp.square(y - mean), axis=-1, keepdims=True)
    y = ((y - mean) * lax.rsqrt(var + RW_GN_EPS)).reshape(B, T, D) * ln_w + ln_b
    bonus = jnp.sum(r_ * k_ * r_k, axis=-1, keepdims=True) * v_
    y = (y + bonus.reshape(B, T, D)) * g
    return y.astype(h.dtype) @ w_o, h[:, -1], S.astype(wkv0.dtype)


def chunk_gated_delta(q, k, v, beta, g, S0):
    T, DV = q.shape[1], v.shape[-1]
    c = math.gcd(T, GD_CHUNK)
    idx = jnp.arange(c)
    incl = idx[:, None] >= idx[None, :]
    strict = idx[:, None] > idx[None, :]

    def body(S, inp):
        q_, k_, v_, b_, g_ = inp
        cum = jnp.cumsum(g_, axis=-1)
        dec = jnp.exp(jnp.where(incl, cum[..., :, None] - cum[..., None, :], -jnp.inf))
        kb = k_ * b_[..., None]
        A = jnp.where(strict, jnp.einsum('bhtd,bhjd->bhtj', kb, k_) * dec, 0.0)
        rhs = jnp.concatenate([v_ * b_[..., None], kb * jnp.exp(cum)[..., None]], axis=-1)
        sol = lax.linalg.triangular_solve(A, rhs, left_side=True, lower=True, unit_diagonal=True)
        u = sol[..., :DV] - jnp.einsum('bhtk,bhkv->bhtv', sol[..., DV:], S)
        attn = jnp.einsum('bhtd,bhjd->bhtj', q_, k_) * dec
        o = (jnp.einsum('bhtd,bhdv->bhtv', q_ * jnp.exp(cum)[..., None], S)
             + jnp.einsum('bhtj,bhjv->bhtv', attn, u))
        last = cum[..., -1:]
        S = (S * jnp.exp(last)[..., None]
             + jnp.einsum('bhjd,bhjv->bhdv', k_ * jnp.exp(last - cum)[..., None], u))
        return S, o

    S, o = lax.scan(body, S0, (to_chunks(q, c), to_chunks(k, c), to_chunks(v, c),
                              to_chunks(beta, c), to_chunks(g, c)))
    return from_chunks(o), S


def gdn_mix(h, conv0, S0, w_in, conv_w, a_log, dt_bias, norm_w, w_o):
    B, T, D = h.shape
    H, DK, DV = GD_HEADS, GD_DK, GD_DV
    KD, VD = H * DK, H * DV
    proj = h @ w_in
    qkv, z, b_logit, a_logit = jnp.split(proj, [2 * KD + VD, 2 * KD + 2 * VD, 2 * KD + 2 * VD + H], axis=-1)
    xp = jnp.concatenate([conv0.astype(qkv.dtype), qkv], axis=1)
    conv = sum(xp[:, j:j + T] * conv_w[j] for j in range(GD_CONV))
    q, k, v = jnp.split(jax.nn.silu(conv), [KD, 2 * KD], axis=-1)
    q = l2norm(split_heads(q, H)) * (DK ** -0.5)
    k = l2norm(split_heads(k, H))
    v = split_heads(v, H).astype(jnp.float32)
    beta = jax.nn.sigmoid(b_logit.astype(jnp.float32))
    g = -jnp.exp(a_log) * jax.nn.softplus(a_logit.astype(jnp.float32) + dt_bias)
    o, S = chunk_gated_delta(q, k, v, beta, g, S0.astype(jnp.float32))
    o = _rms(o) * norm_w * jax.nn.silu(split_heads(z, H).astype(jnp.float32))
    out = o.reshape(B, T, VD).astype(h.dtype) @ w_o
    return out, xp[:, -(GD_CONV - 1):].astype(conv0.dtype), S.astype(S0.dtype)


def chunk_retention(q, k, v, log_gamma, S0):
    T = q.shape[1]
    c = math.gcd(T, RT_CHUNK)
    idx = jnp.arange(c, dtype=jnp.float32)
    diff = idx[:, None] - idx[None, :]
    dmask = jnp.where(diff >= 0, jnp.exp(log_gamma[:, None, None] * jnp.maximum(diff, 0.0)), 0.0)
    q_dec = jnp.exp(log_gamma[:, None] * (idx + 1.0))
    k_dec = jnp.exp(log_gamma[:, None] * (c - 1.0 - idx))
    chunk_dec = jnp.exp(log_gamma * c)

    def body(S, inp):
        q_, k_, v_ = inp
        inner = jnp.einsum('bhtd,bhjd->bhtj', q_, k_) * dmask
        o = (jnp.einsum('bhtj,bhjv->bhtv', inner, v_)
             + jnp.einsum('bhtd,bhdv->bhtv', q_, S) * q_dec[..., None])
        S = S * chunk_dec[:, None, None] + jnp.einsum('bhjd,bhjv->bhdv', k_ * k_dec[..., None], v_)
        return S, o

    S, o = lax.scan(body, S0, (to_chunks(q, c), to_chunks(k, c), to_chunks(v, c)))
    return from_chunks(o), S


def retention_mix(h, S0, pos0, w_in, norm_w, w_o):
    B, T, D = h.shape
    H, DK, DV = RT_HEADS, RT_DK, RT_DV
    q, k, v, gate = jnp.split(h @ w_in, [H * DK, 2 * H * DK, 2 * H * DK + H * DV], axis=-1)
    pos = jnp.arange(T, dtype=jnp.float32) + float(pos0)
    q = rotary_every_two(split_heads(q, H).astype(jnp.float32), pos)
    k = rotary_every_two(split_heads(k, H).astype(jnp.float32), pos) * (DK ** -0.5)
    v = split_heads(v, H).astype(jnp.float32)
    log_gamma = jnp.log1p(-jnp.exp2(-5.0 - jnp.arange(H, dtype=jnp.float32)))
    o, S = chunk_retention(q, k, v, log_gamma, S0.astype(jnp.float32))
    o = (_rms(o) * norm_w.reshape(H, DV)).reshape(B, T, H * DV) * jax.nn.silu(gate.astype(jnp.float32))
    return o.astype(h.dtype) @ w_o, S.astype(S0.dtype)


def chunk_gla(q, k, v, log_f, S0):
    T = q.shape[1]
    c = math.gcd(T, HG_CHUNK)
    idx = jnp.arange(c)
    incl = (idx[:, None] >= idx[None, :])[..., None]

    def body(S, inp):
        q_, k_, v_, lf = inp
        cum = jnp.cumsum(lf, axis=-2)
        dec = jnp.exp(jnp.where(incl, cum[..., :, None, :] - cum[..., None, :, :], -jnp.inf))
        attn = jnp.einsum('bhte,bhje,bhtje->bhtj', q_, k_, dec)
        o = (jnp.einsum('bhte,bhev->bhtv', q_ * jnp.exp(cum), S)
             + jnp.einsum('bhtj,bhjv->bhtv', attn, v_))
        last = cum[..., -1:, :]
        S = (S * jnp.exp(last[..., 0, :])[..., None]
             + jnp.einsum('bhje,bhjv->bhev', k_ * jnp.exp(last - cum), v_))
        return S, o

    S, o = lax.scan(body, S0, (to_chunks(q, c), to_chunks(k, c), to_chunks(v, c), to_chunks(log_f, c)))
    return from_chunks(o), S


def hgrn2_mix(h, S0, lb, w_in, norm_w, w_o):
    B, T, D = h.shape
    H, E, DV = HG_HEADS, HG_EXPAND, HG_DV
    q, f, i, gate = jnp.split(h @ w_in, [H * E, 2 * H * E, 2 * H * E + H * DV], axis=-1)
    lb = lb.reshape(H, E)
    f = lb + (1.0 - lb) * jax.nn.sigmoid(split_heads(f, H).astype(jnp.float32))
    o, S = chunk_gla(split_heads(q, H).astype(jnp.float32), 1.0 - f,
                     split_heads(i, H).astype(jnp.float32), jnp.log(f), S0.astype(jnp.float32))
    o = (_rms(o) * norm_w.reshape(H, DV)).reshape(B, T, H * DV) * jax.nn.silu(gate.astype(jnp.float32))
    return o.astype(h.dtype) @ w_o, S.astype(S0.dtype)


def moe(x, w_router, b_router, w_gu, b_gu, w_down, b_down):
    logits = (x @ w_router).astype(jnp.float32) + b_router
    top_v, top_i = lax.top_k(logits, TOP_K)
    gates = jax.nn.softmax(top_v, axis=-1)
    combine = jnp.sum(jax.nn.one_hot(top_i, N_EXPERTS, dtype=jnp.float32) * gates[..., None], axis=1)
    out = jnp.zeros(x.shape, jnp.float32)
    for e in range(N_EXPERTS):
        gu = x @ w_gu[e] + b_gu[e]
        gl = jnp.minimum(gu[:, :D_FF], SWIGLU_LIMIT)
        up = jnp.clip(gu[:, D_FF:], -SWIGLU_LIMIT, SWIGLU_LIMIT)
        act = (up + 1.0) * gl * jax.nn.sigmoid(SWIGLU_ALPHA * gl)
        out = out + combine[:, e:e + 1] * (act @ w_down[e] + b_down[e]).astype(jnp.float32)
    return out.astype(x.dtype)


def setup_inputs(seed: int = 0) -> dict:
    key = jax.random.key(seed)
    ks = iter(jax.random.split(key, 64))

    def nrm(shape, scale):
        return jax.random.normal(next(ks), shape, jnp.float32) * scale

    def unif(shape, lo, hi):
        return jax.random.uniform(next(ks), shape, jnp.float32, lo, hi)

    D = D_MODEL
    GKD, GVD = GD_HEADS * GD_DK, GD_HEADS * GD_DV
    gd_dt = jnp.exp(unif((L_GDN, GD_HEADS), math.log(1e-3), math.log(1e-1)))
    return {
        'x_prompt': nrm((BATCH, SEQ, D), 1.0),
        'x_sample': nrm((DEC_BATCH, DEC_SEQ, D), 1.0),
        'c_prompt': nrm((BATCH, D), 1.0),
        'c_sample': nrm((DEC_BATCH, D), 1.0),
        'state_rwkv_wkv': nrm((L_RWKV, DEC_BATCH, RW_HEADS, RW_HD, RW_HD), 1.0),
        'state_rwkv_shift': nrm((L_RWKV, DEC_BATCH, D), 1.0),
        'state_gdn_ssm': nrm((L_GDN, DEC_BATCH, GD_HEADS, GD_DK, GD_DV), 0.5),
        'state_gdn_conv': nrm((L_GDN, DEC_BATCH, GD_CONV - 1, 2 * GKD + GVD), 1.0),
        'state_ret': nrm((L_RET, DEC_BATCH, RT_HEADS, RT_DK, RT_DV), 1.0),
        'state_hgrn': nrm((L_HGRN, DEC_BATCH, HG_HEADS, HG_EXPAND, HG_DV), 1.0),
        'ada_w': nrm((DEPTH, D, 6 * D), 0.5 * D ** -0.5),
        'ada_b': nrm((DEPTH, 6 * D), 0.02),
        'norm_mix': 1.0 + nrm((DEPTH, D), 0.02),
        'norm_ffn': 1.0 + nrm((DEPTH, D), 0.02),
        'norm_final': 1.0 + nrm((D,), 0.02),
        'rwkv_mu': unif((L_RWKV, 6, D), 0.0, 1.0),
        'rwkv_w_rkv': nrm((L_RWKV, 3, D, D), D ** -0.5),
        'rwkv_w0': unif((L_RWKV, D), -4.0, 0.0),
        'rwkv_w1': nrm((L_RWKV, D, RW_DECAY_LORA), D ** -0.5),
        'rwkv_w2': nrm((L_RWKV, RW_DECAY_LORA, D), 0.1),
        'rwkv_a0': nrm((L_RWKV, D), 0.1),
        'rwkv_a1': nrm((L_RWKV, D, RW_AAA_LORA), D ** -0.5),
        'rwkv_a2': nrm((L_RWKV, RW_AAA_LORA, D), 0.1),
        'rwkv_g1': nrm((L_RWKV, D, RW_GATE_LORA), D ** -0.5),
        'rwkv_g2': nrm((L_RWKV, RW_GATE_LORA, D), RW_GATE_LORA ** -0.5),
        'rwkv_k_k': 0.85 + nrm((L_RWKV, D), 0.05),
        'rwkv_k_a': 1.0 + nrm((L_RWKV, D), 0.05),
        'rwkv_r_k': nrm((L_RWKV, RW_HEADS, RW_HD), 0.1),
        'rwkv_ln_w': 1.0 + nrm((L_RWKV, D), 0.02),
        'rwkv_ln_b': nrm((L_RWKV, D), 0.02),
        'rwkv_w_o': nrm((L_RWKV, D, D), D ** -0.5),
        'gdn_w_in': nrm((L_GDN, D, 2 * GKD + 2 * GVD + 2 * GD_HEADS), D ** -0.5),
        'gdn_conv_w': nrm((L_GDN, GD_CONV, 2 * GKD + GVD), 0.5),
        'gdn_a_log': jnp.log(unif((L_GDN, GD_HEADS), 1.0, 16.0)),
        'gdn_dt_bias': gd_dt + jnp.log(-jnp.expm1(-gd_dt)),
        'gdn_norm_w': 1.0 + nrm((L_GDN, GD_DV), 0.02),
        'gdn_w_o': nrm((L_GDN, GVD, D), GVD ** -0.5),
        'ret_w_in': nrm((L_RET, D, 2 * RT_HEADS * RT_DK + 2 * RT_HEADS * RT_DV), D ** -0.5),
        'ret_norm_w': 1.0 + nrm((L_RET, RT_HEADS * RT_DV), 0.02),
        'ret_w_o': nrm((L_RET, RT_HEADS * RT_DV, D), (RT_HEADS * RT_DV) ** -0.5),
        'hgrn_w_in': nrm((L_HGRN, D, 2 * HG_HEADS * HG_EXPAND + 2 * HG_HEADS * HG_DV), D ** -0.5),
        'hgrn_lb_logits': nrm((DEPTH, HG_HEADS * HG_EXPAND), 0.1),
        'hgrn_norm_w': 1.0 + nrm((L_HGRN, HG_HEADS * HG_DV), 0.02),
        'hgrn_w_o': nrm((L_HGRN, HG_HEADS * HG_DV, D), (HG_HEADS * HG_DV) ** -0.5),
        'moe_w_router': nrm((DEPTH, D, N_EXPERTS), D ** -0.5),
        'moe_b_router': nrm((DEPTH, N_EXPERTS), 0.01),
        'moe_w_gu': nrm((DEPTH, N_EXPERTS, D, 2 * D_FF), D ** -0.5),
        'moe_b_gu': nrm((DEPTH, N_EXPERTS, 2 * D_FF), 0.01),
        'moe_w_down': nrm((DEPTH, N_EXPERTS, D_FF, D), D_FF ** -0.5),
        'moe_b_down': nrm((DEPTH, N_EXPERTS, D), 0.01),
    }


def reference(x_prompt, x_sample, c_prompt, c_sample,
              state_rwkv_wkv, state_rwkv_shift, state_gdn_ssm, state_gdn_conv, state_ret, state_hgrn,
              ada_w, ada_b, norm_mix, norm_ffn, norm_final,
              rwkv_mu, rwkv_w_rkv, rwkv_w0, rwkv_w1, rwkv_w2, rwkv_a0, rwkv_a1, rwkv_a2, rwkv_g1, rwkv_g2,
              rwkv_k_k, rwkv_k_a, rwkv_r_k, rwkv_ln_w, rwkv_ln_b, rwkv_w_o,
              gdn_w_in, gdn_conv_w, gdn_a_log, gdn_dt_bias, gdn_norm_w, gdn_w_o,
              ret_w_in, ret_norm_w, ret_w_o,
              hgrn_w_in, hgrn_lb_logits, hgrn_norm_w, hgrn_w_o,
              moe_w_router, moe_b_router, moe_w_gu, moe_b_gu, moe_w_down, moe_b_down):
    dt = x_prompt.dtype
    Bp, Tp, _ = x_prompt.shape
    Bs, Ts, _ = x_sample.shape
    p_lb = jax.nn.softmax(hgrn_lb_logits.astype(jnp.float32), axis=0)
    lb_all = jnp.cumsum(p_lb, axis=0) - p_lb[0:1]

    rw_wkv_p, rw_sh_p, gd_ssm_p, gd_conv_p, rt_p, hg_p = [], [], [], [], [], []
    rw_wkv_s, rw_sh_s, gd_ssm_s, gd_conv_s, rt_s, hg_s = [], [], [], [], [], []
    xp, xs = x_prompt, x_sample
    for i in range(DEPTH):
        kind, j = i % N_MIXERS, i // N_MIXERS
        shp, scp, gtp, shp2, scp2, gtp2 = adaln(c_prompt, ada_w[i], ada_b[i])
        shs, scs, gts, shs2, scs2, gts2 = adaln(c_sample, ada_w[i], ada_b[i])
        hp = modulate(xp, norm_mix[i], shp, scp)
        hs = modulate(xs, norm_mix[i], shs, scs)
        if kind == 0:
            prm = (rwkv_mu[j], rwkv_w_rkv[j], rwkv_w0[j], rwkv_w1[j], rwkv_w2[j], rwkv_a0[j], rwkv_a1[j],
                   rwkv_a2[j], rwkv_g1[j], rwkv_g2[j], rwkv_k_k[j], rwkv_k_a[j], rwkv_r_k[j],
                   rwkv_ln_w[j], rwkv_ln_b[j], rwkv_w_o[j])
            o_p, sh_p, wkv_p = rwkv7_mix(hp, jnp.zeros((Bp, D_MODEL), dt),
                                         jnp.zeros((Bp, RW_HEADS, RW_HD, RW_HD), dt), *prm)
            o_s, sh_s, wkv_s = rwkv7_mix(hs, state_rwkv_shift[j], state_rwkv_wkv[j], *prm)
            rw_wkv_p.append(wkv_p); rw_sh_p.append(sh_p); rw_wkv_s.append(wkv_s); rw_sh_s.append(sh_s)
        elif kind == 1:
            prm = (gdn_w_in[j], gdn_conv_w[j], gdn_a_log[j], gdn_dt_bias[j], gdn_norm_w[j], gdn_w_o[j])
            o_p, cv_p, ss_p = gdn_mix(hp, jnp.zeros((Bp, GD_CONV - 1, state_gdn_conv.shape[-1]), dt),
                                      jnp.zeros((Bp, GD_HEADS, GD_DK, GD_DV), dt), *prm)
            o_s, cv_s, ss_s = gdn_mix(hs, state_gdn_conv[j], state_gdn_ssm[j], *prm)
            gd_ssm_p.append(ss_p); gd_conv_p.append(cv_p); gd_ssm_s.append(ss_s); gd_conv_s.append(cv_s)
        elif kind == 2:
            prm = (ret_w_in[j], ret_norm_w[j], ret_w_o[j])
            o_p, r_p = retention_mix(hp, jnp.zeros((Bp, RT_HEADS, RT_DK, RT_DV), dt), 0, *prm)
            o_s, r_s = retention_mix(hs, state_ret[j], PAST_LEN, *prm)
            rt_p.append(r_p); rt_s.append(r_s)
        else:
            prm = (lb_all[i], hgrn_w_in[j], hgrn_norm_w[j], hgrn_w_o[j])
            o_p, h_p = hgrn2_mix(hp, jnp.zeros((Bp, HG_HEADS, HG_EXPAND, HG_DV), dt), *prm)
            o_s, h_s = hgrn2_mix(hs, state_hgrn[j], *prm)
            hg_p.append(h_p); hg_s.append(h_s)
        xp = xp + gtp * o_p
        xs = xs + gts * o_s
        hp2 = modulate(xp, norm_ffn[i], shp2, scp2).reshape(Bp * Tp, D_MODEL)
        hs2 = modulate(xs, norm_ffn[i], shs2, scs2).reshape(Bs * Ts, D_MODEL)
        f = moe(jnp.concatenate([hp2, hs2], axis=0), moe_w_router[i], moe_b_router[i],
                moe_w_gu[i], moe_b_gu[i], moe_w_down[i], moe_b_down[i])
        xp = xp + gtp2 * f[:Bp * Tp].reshape(Bp, Tp, D_MODEL)
        xs = xs + gts2 * f[Bp * Tp:].reshape(Bs, Ts, D_MODEL)

    y_prompt = rms_norm(xp, norm_final)
    y_sample = rms_norm(xs, norm_final)
    new_rwkv_wkv_p = jnp.stack(rw_wkv_p)
    new_rwkv_shift_p = jnp.stack(rw_sh_p)
    new_gdn_ssm_p = jnp.stack(gd_ssm_p)
    new_gdn_conv_p = jnp.stack(gd_conv_p)
    new_ret_p = jnp.stack(rt_p)
    new_hgrn_p = jnp.stack(hg_p)
    new_rwkv_wkv_s = jnp.stack(rw_wkv_s)
    new_rwkv_shift_s = jnp.stack(rw_sh_s)
    new_gdn_ssm_s = jnp.stack(gd_ssm_s)
    new_gdn_conv_s = jnp.stack(gd_conv_s)
    new_ret_s = jnp.stack(rt_s)
    new_hgrn_s = jnp.stack(hg_s)
    return (y_prompt, y_sample,
            new_rwkv_wkv_p, new_rwkv_shift_p, new_gdn_ssm_p, new_gdn_conv_p, new_ret_p, new_hgrn_p,
            new_rwkv_wkv_s, new_rwkv_shift_s, new_gdn_ssm_s, new_gdn_conv_s, new_ret_s, new_hgrn_s)
```

```python
import functools
import math

import jax
import jax.numpy as jnp
from jax import lax
from jax.experimental import pallas as pl
from jax.experimental.pallas import tpu as pltpu

F32 = jnp.float32
BF16 = jnp.bfloat16
I32 = jnp.int32
HIGHEST = lax.Precision.HIGHEST

D = 1024
DEPTH = 4
NORM_EPS = 1e-6
RW_H, RW_N = 16, 64
RW_GN_EPS = 64e-5
GD_H, GD_DK, GD_DV, GD_CONV = 8, 128, 128, 4
GD_C = 3 * GD_H * GD_DK
RT_H, RT_DK, RT_DV = 4, 256, 512
HG_H, HG_E, HG_DV = 8, 128, 128
N_EXPERTS, TOP_K, D_FF = 32, 4, 1024
SWIGLU_LIMIT, SWIGLU_ALPHA = 7.0, 1.702

LANES = 128
EXPERT_TILE = 512
TOKEN_BLOCK = 256
VMEM_LIMIT = 56 * 1024 * 1024


def _cparams(sem, vmem=VMEM_LIMIT):
    return pltpu.CompilerParams(dimension_semantics=sem, vmem_limit_bytes=vmem)


def _sigmoid(x):
    return 1.0 / (1.0 + jnp.exp(-x))


def _silu(x):
    return x * _sigmoid(x)


def _softplus(x):
    return jnp.maximum(x, 0.0) + jnp.log(1.0 + jnp.exp(-jnp.abs(x)))


def _modulate(x, g, shift, scale):
    ms = jnp.mean(x * x, axis=-1, keepdims=True)
    return (x * lax.rsqrt(ms + NORM_EPS) * g) * (1.0 + scale) + shift


def _bdot(a, b):
    return jnp.dot(a.astype(BF16), b.astype(BF16), preferred_element_type=F32)


def _bdot_nt(a, b):
    return lax.dot_general(a.astype(BF16), b.astype(BF16), (((1,), (1,)), ((), ())),
                           preferred_element_type=F32)


def _bdot_tn(a, b):
    return lax.dot_general(a.astype(BF16), b.astype(BF16), (((0,), (0,)), ((), ())),
                           preferred_element_type=F32)


def _fdot(a, b):
    return jnp.dot(a, b, precision=HIGHEST, preferred_element_type=F32)


class _Group:
    def __init__(self, b, t, tm):
        self.b, self.t, self.n = b, t, b * t
        self.tm = min(tm, self.n)
        assert self.n % self.tm == 0
        assert (self.t % self.tm == 0) or (self.tm % self.t == 0)
        self.per_batch = self.t % self.tm == 0
        self.tiles = self.n // self.tm

    def rows(self, width):
        return pl.BlockSpec((self.tm, width), lambda i: (i, 0))

    def rowmod(self, arr):
        w = arr.shape[-1]
        if self.per_batch:
            k = self.t // self.tm
            return arr.reshape(self.b, 1, w), pl.BlockSpec((1, 1, w), lambda i: (i // k, 0, 0))
        rep = jnp.repeat(arr, self.t, axis=0).reshape(self.tiles, self.tm, w)
        return rep, pl.BlockSpec((1, self.tm, w), lambda i: (i, 0, 0))

    def rowseq(self, arr):
        w = arr.shape[-1]
        if self.per_batch:
            assert arr.shape[1] == 1
            k = self.t // self.tm
            return arr, pl.BlockSpec((1, 1, w), lambda i: (i // k, 0, 0))
        return arr.reshape(self.tiles, self.tm, w), pl.BlockSpec((1, self.tm, w), lambda i: (i, 0, 0))

    def postab(self, tab):
        w = tab.shape[-1]
        if self.per_batch:
            k = self.t // self.tm
            return tab.reshape(k, self.tm, w), pl.BlockSpec((1, self.tm, w), lambda i: (i % k, 0, 0))
        rep = jnp.tile(tab, (self.tm // self.t, 1)).reshape(1, self.tm, w)
        return rep, pl.BlockSpec((1, self.tm, w), lambda i: (0, 0, 0))

    def prev8(self, width):
        k = self.tm // 8
        return pl.BlockSpec((8, width), lambda i: (jnp.maximum(i * k - 1, 0), 0))


def _full(shape):
    nd = len(shape)
    return pl.BlockSpec(shape, lambda *a: (0,) * nd)


def _tpos(tm, t):
    row = pl.program_id(0) * tm + lax.broadcasted_iota(I32, (tm, 1), 0)
    return row % t


def _shift_rows(cur, prev8, d):
    rolled = pltpu.roll(cur, d, 0)
    head = jnp.where(lax.broadcasted_iota(I32, (8, 1), 0) < d, pltpu.roll(prev8, d, 0), rolled[0:8])
    if cur.shape[0] == 8:
        return head
    return jnp.concatenate([head, rolled[8:]], axis=0)


def _ada_kernel(c_ref, w_ref, b_ref, o_ref):
    o_ref[0] = _bdot(_silu(c_ref[...]), w_ref[0]) + b_ref[0]


def _ada_call(c_all, ada_w, ada_b):
    nb = c_all.shape[0]
    tn = 1536
    return pl.pallas_call(
        _ada_kernel,
        out_shape=jax.ShapeDtypeStruct((DEPTH, nb, 6 * D), F32),
        grid=(DEPTH, 6 * D // tn),
        in_specs=[pl.BlockSpec((nb, D), lambda l, j: (0, 0)),
                  pl.BlockSpec((1, D, tn), lambda l, j: (l, 0, j)),
                  pl.BlockSpec((1, 1, tn), lambda l, j: (l, 0, j))],
        out_specs=pl.BlockSpec((1, nb, tn), lambda l, j: (l, 0, j)),
        compiler_params=_cparams(("arbitrary", "arbitrary")),
        name="adaln",
    )(c_all, ada_w, ada_b.reshape(DEPTH, 1, 6 * D))


def _modrows_kernel(x_ref, g_ref, sh_ref, sc_ref, *rest):
    h = _modulate(x_ref[...], g_ref[...], sh_ref[...], sc_ref[...])
    if len(rest) == 2:
        w_ref, o_ref = rest
        o_ref[...] = _bdot(h, w_ref[...])
    else:
        rest[0][...] = h


def _modrows_call(x, g, shift, scale, w=None):
    n = x.shape[0]
    args = [x, g.reshape(1, D), shift, scale]
    specs = [_full((n, D)), _full((1, D)), _full((n, D)), _full((n, D))]
    width = D
    if w is not None:
        args.append(w)
        specs.append(_full(w.shape))
        width = w.shape[1]
    return pl.pallas_call(
        _modrows_kernel,
        out_shape=jax.ShapeDtypeStruct((n, width), F32),
        grid=(1,),
        in_specs=specs,
        out_specs=_full((n, width)),
        compiler_params=_cparams(("arbitrary",)),
        name="modrows",
    )(*args)


def _outproj_kernel(has_mul, x_ref, y_ref, *rest):
    if has_mul:
        m_ref, w_ref, gt_ref, o_ref = rest
        y = y_ref[...].astype(F32) * m_ref[...].astype(F32)
    else:
        w_ref, gt_ref, o_ref = rest
        y = y_ref[...]
    o_ref[...] = x_ref[...] + gt_ref[0] * _bdot(y, w_ref[...])


def _outproj_call(grp, x, y, w_o, gate, mul=None):
    dy = y.shape[1]
    gt, gt_spec = grp.rowmod(gate)
    args = [x, y]
    specs = [grp.rows(D), grp.rows(dy)]
    if mul is not None:
        args.append(mul)
        specs.append(grp.rows(dy))
    args += [w_o, gt]
    specs += [_full(w_o.shape), gt_spec]
    return pl.pallas_call(
        functools.partial(_outproj_kernel, mul is not None),
        out_shape=jax.ShapeDtypeStruct((grp.n, D), F32),
        grid=(grp.tiles,),
        in_specs=specs,
        out_specs=grp.rows(D),
        compiler_params=_cparams(("arbitrary",)),
        name="outproj",
    )(*args)


def _final_kernel(x_ref, g_ref, o_ref):
    x = x_ref[...]
    ms = jnp.mean(x * x, axis=-1, keepdims=True)
    o_ref[...] = x * lax.rsqrt(ms + NORM_EPS) * g_ref[...]


def _final_call(grp, x, g):
    return pl.pallas_call(
        _final_kernel,
        out_shape=jax.ShapeDtypeStruct((grp.n, D), F32),
        grid=(grp.tiles,),
        in_specs=[grp.rows(D), _full((1, D))],
        out_specs=grp.rows(D),
        compiler_params=_cparams(("arbitrary",)),
        name="final_norm",
    )(x, g.reshape(1, D))


def _router_kernel(cin_ref, x_ref, g_ref, sh_ref, sc_ref, wr_ref, br_ref,
                   h_ref, idx_ref, gate_ref, rank_ref, cnt_ref, carry):
    i = pl.program_id(0)

    @pl.when(i == 0)
    def _():
        carry[...] = cin_ref[...]

    tm = x_ref.shape[0]
    h = _modulate(x_ref[...], g_ref[...], sh_ref[0], sc_ref[0])
    h_ref[...] = h
    logits = _fdot(h, wr_ref[...]) + br_ref[...]
    lane = lax.broadcasted_iota(I32, logits.shape, 1)
    work = logits
    sel = jnp.zeros(logits.shape, jnp.bool_)
    picks, vals = [], []
    for _ in range(TOP_K):
        m = jnp.max(work, axis=-1, keepdims=True)
        idx = jnp.min(jnp.where(work == m, lane, N_EXPERTS), axis=-1, keepdims=True)
        pick = lane == idx
        picks.append((idx, pick))
        vals.append(m)
        sel = jnp.logical_or(sel, pick)
        work = jnp.where(pick, -jnp.inf, work)
    es = [jnp.exp(v - vals[0]) for v in vals]
    denom = es[0] + es[1] + es[2] + es[3]
    self_f = sel.astype(F32)
    tri = (lax.broadcasted_iota(I32, (tm, tm), 0) > lax.broadcasted_iota(I32, (tm, tm), 1))
    local = jnp.dot(tri.astype(BF16), self_f.astype(BF16), preferred_element_type=F32)
    rank = local + carry[...]
    carry[...] = carry[...] + jnp.sum(self_f, axis=0, keepdims=True)
    cnt_ref[...] = carry[...]
    lane_o = lax.broadcasted_iota(I32, (tm, LANES), 1)
    idx_o = jnp.zeros((tm, LANES), I32)
    gate_o = jnp.zeros((tm, LANES), F32)
    rank_o = jnp.zeros((tm, LANES), I32)
    for k in range(TOP_K):
        idx, pick = picks[k]
        rk = jnp.sum(jnp.where(pick, rank, 0.0), axis=-1, keepdims=True)
        idx_o = jnp.where(lane_o == k, idx, idx_o)
        gate_o = jnp.where(lane_o == k, es[k] / denom, gate_o)
        rank_o = jnp.where(lane_o == k, rk.astype(I32), rank_o)
    idx_ref[...] = idx_o
    gate_ref[...] = gate_o
    rank_ref[...] = rank_o


def _router_call(grp, counts_in, x, g, shift, scale, w_router, b_router):
    sh, sh_spec = grp.rowmod(shift)
    sc, sc_spec = grp.rowmod(scale)
    pad = pl.BlockSpec((grp.tm, LANES), lambda i: (i, 0))
    return pl.pallas_call(
        _router_kernel,
        out_shape=(jax.ShapeDtypeStruct((grp.n, D), F32),
                   jax.ShapeDtypeStruct((grp.n, LANES), I32),
                   jax.ShapeDtypeStruct((grp.n, LANES), F32),
                   jax.ShapeDtypeStruct((grp.n, LANES), I32),
                   jax.ShapeDtypeStruct((1, N_EXPERTS), F32)),
        grid=(grp.tiles,),
        in_specs=[_full((1, N_EXPERTS)), grp.rows(D), _full((1, D)), sh_spec, sc_spec,
                  _full((D, N_EXPERTS)), _full((1, N_EXPERTS))],
        out_specs=(grp.rows(D), pad, pad, pad, _full((1, N_EXPERTS))),
        scratch_shapes=[pltpu.VMEM((1, N_EXPERTS), F32)],
        compiler_params=_cparams(("arbitrary",)),
        name="moe_router",
    )(counts_in, x, g.reshape(1, D), sh, sc, w_router, b_router.reshape(1, N_EXPERTS))


def _dispatch_kernel(cnt_ref, off_ref, nv_ref, pos_ref, h_ref, z_ref, zt_ref, xs_ref, sem, sem_t):
    i = pl.program_id(0)
    nrow = TOKEN_BLOCK * TOP_K

    def copy(src, p):
        return pltpu.make_async_copy(src, xs_ref.at[pl.ds(p, 1)], sem)

    def issue(r, c):
        tok = i * TOKEN_BLOCK + r // TOP_K
        copy(h_ref.at[pl.ds(tok, 1)], pos_ref[r]).start()
        return c

    lax.fori_loop(0, nrow, issue, 0)

    def drain(r, c):
        copy(h_ref.at[pl.ds(0, 1)], 0).wait()
        return c

    lax.fori_loop(0, nrow, drain, 0)

    @pl.when(i == pl.num_programs(0) - 1)
    def _():
        def per_expert(e, c):
            n = cnt_ref[e]
            npad = ((n + EXPERT_TILE - 1) // EXPERT_TILE) * EXPERT_TILE
            base = off_ref[e]

            def fill(p, c2):
                copy(z_ref, base + p).start()
                return c2

            def fill_wait(p, c2):
                copy(z_ref, 0).wait()
                return c2

            lax.fori_loop(n, npad, fill, 0)
            lax.fori_loop(n, npad, fill_wait, 0)
            return c

        lax.fori_loop(0, N_EXPERTS, per_expert, 0)

        def tail(j, c):
            cp = pltpu.make_async_copy(zt_ref, xs_ref.at[pl.ds(j * EXPERT_TILE, EXPERT_TILE)], sem_t)
            cp.start()
            cp.wait()
            return c

        lax.fori_loop(nv_ref[0], xs_ref.shape[0] // EXPERT_TILE, tail, 0)


def _dispatch_call(counts, offsets, n_valid, pos_flat, h, n_rows):
    n = h.shape[0]
    zrow = jnp.zeros((1, D), F32)
    ztile = jnp.zeros((EXPERT_TILE, D), F32)
    return pl.pallas_call(
        _dispatch_kernel,
        out_shape=jax.ShapeDtypeStruct((n_rows, D), F32),
        grid_spec=pltpu.PrefetchScalarGridSpec(
            num_scalar_prefetch=3,
            grid=(n // TOKEN_BLOCK,),
            in_specs=[pl.BlockSpec((TOKEN_BLOCK * TOP_K,), lambda i, c, o, v: (i,), memory_space=pltpu.SMEM),
                      pl.BlockSpec(memory_space=pl.ANY),
                      pl.BlockSpec(memory_space=pl.ANY),
                      pl.BlockSpec(memory_space=pl.ANY)],
            out_specs=pl.BlockSpec(memory_space=pl.ANY),
            scratch_shapes=[pltpu.SemaphoreType.DMA(()), pltpu.SemaphoreType.DMA(())]),
        compiler_params=_cparams(("arbitrary",)),
        name="moe_dispatch",
    )(counts, offsets, n_valid, pos_flat, h, zrow, ztile)


def _expert_kernel(te_ref, nv_ref, x_ref, wgu_ref, bgu_ref, wd_ref, bd_ref, o_ref, wgu_s, wd_s):
    j = pl.program_id(0)
    fresh = jnp.logical_or(j == 0, te_ref[j] != te_ref[jnp.maximum(j - 1, 0)])

    @pl.when(jnp.logical_and(j < nv_ref[0], fresh))
    def _():
        wgu_s[...] = wgu_ref[0].astype(BF16)
        wd_s[...] = wd_ref[0].astype(BF16)

    @pl.when(j < nv_ref[0])
    def _():
        x = x_ref[...].astype(BF16)
        gu = jnp.dot(x, wgu_s[...], preferred_element_type=F32) + bgu_ref[0]
        gl = jnp.minimum(gu[:, :D_FF], SWIGLU_LIMIT)
        up = jnp.clip(gu[:, D_FF:], -SWIGLU_LIMIT, SWIGLU_LIMIT)
        act = (up + 1.0) * gl * _sigmoid(SWIGLU_ALPHA * gl)
        o_ref[...] = jnp.dot(act.astype(BF16), wd_s[...], preferred_element_type=F32) + bd_ref[0]

    @pl.when(j >= nv_ref[0])
    def _():
        o_ref[...] = jnp.zeros(o_ref.shape, F32)


def _expert_call(tile_expert, n_valid, xs, w_gu, b_gu, w_down, b_down):
    n_rows = xs.shape[0]
    g = n_rows // EXPERT_TILE
    return pl.pallas_call(
        _expert_kernel,
        out_shape=jax.ShapeDtypeStruct((n_rows, D), F32),
        grid_spec=pltpu.PrefetchScalarGridSpec(
            num_scalar_prefetch=2,
            grid=(g,),
            in_specs=[pl.BlockSpec((EXPERT_TILE, D), lambda j, te, nv: (jnp.minimum(j, nv[0] - 1), 0)),
                      pl.BlockSpec((1, D, 2 * D_FF), lambda j, te, nv: (te[j], 0, 0)),
                      pl.BlockSpec((1, 1, 2 * D_FF), lambda j, te, nv: (te[j], 0, 0)),
                      pl.BlockSpec((1, D_FF, D), lambda j, te, nv: (te[j], 0, 0)),
                      pl.BlockSpec((1, 1, D), lambda j, te, nv: (te[j], 0, 0))],
            out_specs=pl.BlockSpec((EXPERT_TILE, D), lambda j, te, nv: (j, 0)),
            scratch_shapes=[pltpu.VMEM((D, 2 * D_FF), BF16), pltpu.VMEM((D_FF, D), BF16)]),
        compiler_params=_cparams(("arbitrary",)),
        name="moe_experts",
    )(tile_expert, n_valid, xs, w_gu, b_gu.reshape(N_EXPERTS, 1, 2 * D_FF), w_down,
      b_down.reshape(N_EXPERTS, 1, D))


def _combine_kernel(pos_ref, x_ref, gate_ref, gt_ref, ys_ref, o_ref, buf, sem):
    nrow = TOKEN_BLOCK * TOP_K

    def copy(p, k, t):
        return pltpu.make_async_copy(ys_ref.at[pl.ds(p, 1)], buf.at[k, pl.ds(t, 1)], sem)

    def issue(t, c):
        for k in range(TOP_K):
            copy(pos_ref[t * TOP_K + k], k, t).start()
        return c

    lax.fori_loop(0, TOKEN_BLOCK, issue, 0)

    def drain(r, c):
        copy(0, 0, 0).wait()
        return c

    lax.fori_loop(0, nrow, drain, 0)
    gate = gate_ref[...]
    f = gate[:, 0:1] * buf[0]
    for k in range(1, TOP_K):
        f = f + gate[:, k:k + 1] * buf[k]
    o_ref[...] = x_ref[...] + gt_ref[0] * f


def _combine_call(grp, pos_flat, x, gate_pad, gate2, ys):
    gt, gt_spec = grp.rowmod(gate2)
    assert grp.tm == TOKEN_BLOCK
    return pl.pallas_call(
        _combine_kernel,
        out_shape=jax.ShapeDtypeStruct((grp.n, D), F32),
        grid=(grp.tiles,),
        in_specs=[pl.BlockSpec((TOKEN_BLOCK * TOP_K,), lambda i: (i,), memory_space=pltpu.SMEM),
                  grp.rows(D), grp.rows(LANES), gt_spec, pl.BlockSpec(memory_space=pl.ANY)],
        out_specs=grp.rows(D),
        scratch_shapes=[pltpu.VMEM((TOP_K, TOKEN_BLOCK, D), F32), pltpu.SemaphoreType.DMA(())],
        compiler_params=_cparams(("arbitrary",)),
        name="moe_combine",
    )(pos_flat, x, gate_pad, gt, ys)


def _moe(groups, xs_in, norm_g, mods, w_router, b_router, w_gu, b_gu, w_down, b_down):
    counts = jnp.zeros((1, N_EXPERTS), F32)
    hs, idxs, gates, ranks = [], [], [], []
    for grp, x, (sh2, sc2, _) in zip(groups, xs_in, mods):
        h, idx, gate, rank, counts = _router_call(grp, counts, x, norm_g, sh2, sc2, w_router, b_router)
        hs.append(h)
        idxs.append(idx[:, :TOP_K])
        gates.append(gate)
        ranks.append(rank[:, :TOP_K])
    h_all = jnp.concatenate(hs, axis=0)
    idx_all = jnp.concatenate(idxs, axis=0)
    rank_all = jnp.concatenate(ranks, axis=0)
    n = h_all.shape[0]
    cnt = counts[0].astype(I32)
    padded = ((cnt + EXPERT_TILE - 1) // EXPERT_TILE) * EXPERT_TILE
    ends = jnp.cumsum(padded)
    offsets = ends - padded
    n_tiles = (n * TOP_K + N_EXPERTS * (EXPERT_TILE - 1)) // EXPERT_TILE
    n_rows = n_tiles * EXPERT_TILE
    pos = (jnp.take(offsets, idx_all) + rank_all).astype(I32)
    pos_flat = pos.reshape(n * TOP_K)
    n_valid = (ends[-1] // EXPERT_TILE).astype(I32)
    tile_start = jnp.arange(n_tiles, dtype=I32) * EXPERT_TILE
    tile_start = jnp.minimum(tile_start, ends[-1] - EXPERT_TILE)
    tile_expert = jnp.searchsorted(ends, tile_start, side="right").astype(I32)
    n_valid = n_valid.reshape(1)
    xs = _dispatch_call(cnt, offsets.astype(I32), n_valid, pos_flat, h_all, n_rows)
    ys = _expert_call(tile_expert, n_valid, xs, w_gu, b_gu, w_down, b_down)
    outs = []
    start = 0
    for grp, x, gate, (_, _, gt2) in zip(groups, xs_in, gates, mods):
        cgrp = _Group(grp.b, grp.t, TOKEN_BLOCK)
        p = lax.dynamic_slice_in_dim(pos_flat, start * TOP_K, grp.n * TOP_K)
        outs.append(_combine_call(cgrp, p, x, gate, gt2, ys))
        start += grp.n
    return outs


def _rwkv_proj_kernel(t_len, x_ref, xp_ref, g_ref, sh_ref, sc_ref, s0_ref, mu_ref, wrkv_ref, w0_ref,
                      w1_ref, w2_ref, a0_ref, a1_ref, a2_ref, g1_ref, g2_ref,
                      r_ref, w_ref, k_ref, v_ref, a_ref, gg_ref):
    tm = x_ref.shape[0]
    g, sh, sc = g_ref[...], sh_ref[0], sc_ref[0]
    h = _modulate(x_ref[...], g, sh, sc)
    hp = _modulate(xp_ref[...], g, sh[0:8] if sh.shape[0] > 1 else sh, sc[0:8] if sc.shape[0] > 1 else sc)
    prev = jnp.where(_tpos(tm, t_len) == 0, s0_ref[0], _shift_rows(h, hp, 1))
    dx = prev - h
    mu = mu_ref[...]
    xr, xw, xk, xv, xa, xg = [h + dx * mu[n:n + 1] for n in range(6)]
    r_ref[...] = _bdot(xr, wrkv_ref[0])
    k_ref[...] = _bdot(xk, wrkv_ref[1])
    v_ref[...] = _bdot(xv, wrkv_ref[2])
    w_log = -_softplus(-(w0_ref[...] + _bdot(jnp.tanh(_bdot(xw, w1_ref[...])), w2_ref[...]))) - 0.5
    w_ref[...] = jnp.exp(-jnp.exp(w_log))
    a_ref[...] = _sigmoid(a0_ref[...] + _bdot(_bdot(xa, a1_ref[...]), a2_ref[...]))
    gg_ref[...] = _bdot(_sigmoid(_bdot(xg, g1_ref[...])), g2_ref[...]).astype(BF16)


def _rwkv_core_kernel(r_ref, w_ref, k_ref, v_ref, a_ref, kk_p, ka_p, rk_p, lnw_p, lnb_p, s0_ref,
                      y_ref, st_ref, state, kk_s, b_s, km_s):
    j = pl.program_id(1)
    tc = r_ref.shape[0]
    n = RW_N

    @pl.when(j == 0)
    def _():
        state[...] = s0_ref[...]

    def step(t, c):
        kt, at, vt, rt = k_ref[t], a_ref[t], v_ref[t], r_ref[t]
        kk = kt * kk_p[...]
        kk = kk * lax.rsqrt(jnp.sum(kk * kk, axis=0, keepdims=True) + 1e-6)
        km = kt * (1.0 + (at - 1.0) * ka_p[...])
        kk_s[...] = kk
        b_s[...] = kk * at
        km_s[...] = km
        sa = jnp.zeros((n, LANES), F32)
        for kx in range(n):
            sa = sa + state[kx] * kk_s[pl.ds(kx, 1), :]
        y = jnp.zeros((n, LANES), F32)
        for kx in range(n):
            s_new = (state[kx] * w_ref[t, pl.ds(kx, 1), :] - sa * b_s[pl.ds(kx, 1), :]
                     + vt * km_s[pl.ds(kx, 1), :])
            state[kx] = s_new
            y = y + s_new * r_ref[t, pl.ds(kx, 1), :]
        mean = jnp.mean(y, axis=0, keepdims=True)
        yc = y - mean
        var = jnp.mean(yc * yc, axis=0, keepdims=True)
        bonus = jnp.sum(rt * km * rk_p[...], axis=0, keepdims=True) * vt
        y_ref[t] = yc * lax.rsqrt(var + RW_GN_EPS) * lnw_p[...] + lnb_p[...] + bonus
        return c

    lax.fori_loop(0, tc, step, 0)

    @pl.when(j == pl.num_programs(1) - 1)
    def _():
        st_ref[...] = state[...]


def _rwkv_layer(grp, x, norm_g, mods, shift0, wkv0, mu, w_rkv, w0, w1, w2, a0, a1, a2, g1, g2,
                k_k, k_a, r_k, ln_w, ln_b, w_o):
    b, t = grp.b, grp.t
    shift, scale, gate = mods
    sh, sh_spec = grp.rowmod(shift)
    sc, sc_spec = grp.rowmod(scale)
    if shift0 is None:
        s0 = jnp.zeros((b, 1, D), F32)
    else:
        s0 = jnp.concatenate([shift0[:, None, :], jnp.zeros((b, t - 1, D), F32)], axis=1)
    s0, s0_spec = grp.rowseq(s0)
    bf = lambda z: z.astype(BF16)
    row = lambda z: z.reshape(1, -1)
    weights = [mu, bf(w_rkv), row(w0), bf(w1), bf(w2), row(a0), bf(a1), bf(a2), bf(g1), bf(g2)]
    outs = pl.pallas_call(
        functools.partial(_rwkv_proj_kernel, t),
        out_shape=tuple(jax.ShapeDtypeStruct((grp.n, D), F32) for _ in range(5))
        + (jax.ShapeDtypeStruct((grp.n, D), BF16),),
        grid=(grp.tiles,),
        in_specs=[grp.rows(D), grp.prev8(D), _full((1, D)), sh_spec, sc_spec, s0_spec]
        + [_full(z.shape) for z in weights],
        out_specs=tuple(grp.rows(D) for _ in range(6)),
        compiler_params=_cparams(("arbitrary",)),
        name="rwkv_proj",
    )(x, x, row(norm_g), sh, sc, s0, *weights)
    r, w, k, v, a, gg = outs
    bh = b * RW_H

    def to_core(z):
        return z.reshape(b, t, RW_H, RW_N).transpose(1, 3, 0, 2).reshape(t, RW_N, bh)

    def ptile(p):
        return jnp.tile(p.reshape(RW_H, RW_N).T, (1, b))

    if wkv0 is None:
        st0 = jnp.zeros((RW_N, RW_N, bh), F32)
    else:
        st0 = wkv0.transpose(3, 2, 0, 1).reshape(RW_N, RW_N, bh)
    tc = min(t, 16)
    seq = pl.BlockSpec((tc, RW_N, LANES), lambda q, j: (j, 0, q))
    par = pl.BlockSpec((RW_N, LANES), lambda q, j: (0, q))
    stt = pl.BlockSpec((RW_N, RW_N, LANES), lambda q, j: (0, 0, q))
    y, st = pl.pallas_call(
        _rwkv_core_kernel,
        out_shape=(jax.ShapeDtypeStruct((t, RW_N, bh), F32),
                   jax.ShapeDtypeStruct((RW_N, RW_N, bh), F32)),
        grid=(bh // LANES, t // tc),
        in_specs=[seq] * 5 + [par] * 5 + [stt],
        out_specs=(seq, stt),
        scratch_shapes=[pltpu.VMEM((RW_N, RW_N, LANES), F32)] + [pltpu.VMEM((RW_N, LANES), F32)] * 3,
        compiler_params=_cparams(("arbitrary", "arbitrary")),
        name="rwkv_core",
    )(to_core(r), to_core(w), to_core(k), to_core(v), to_core(a),
      ptile(k_k), ptile(k_a), ptile(r_k.reshape(-1)), ptile(ln_w), ptile(ln_b), st0)
    y_rows = y.reshape(t, RW_N, b, RW_H).transpose(2, 0, 3, 1).reshape(grp.n, D)
    x_new = _outproj_call(grp, x, y_rows, bf(w_o), gate, mul=gg)
    new_wkv = st.reshape(RW_N, RW_N, b, RW_H).transpose(2, 3, 1, 0)
    x_last = x.reshape(b, t, D)[:, -1]
    new_shift = _modrows_call(x_last, norm_g, shift, scale)
    return x_new, new_shift, new_wkv


def _pad_time(z, b, t, tp):
    if tp == t:
        return z
    w = z.shape[-1]
    return jnp.pad(z.reshape(b, t, w), ((0, 0), (0, tp - t), (0, 0))).reshape(b * tp, w)


def _unpad_time(z, b, t, tp):
    if tp == t:
        return z
    w = z.shape[-1]
    return z.reshape(b, tp, w)[:, :t].reshape(b * t, w)


def _gdn_proj_kernel(t_len, x_ref, xp_ref, g_ref, sh_ref, sc_ref, c1_ref, c2_ref, c3_ref, wqkv_ref,
                     wz_ref, wb_ref, wa_ref, cw_ref, alog_ref, dtb_ref,
                     qkv_ref, z_ref, beta_ref, gdec_ref):
    tm = x_ref.shape[0]
    g, sh, sc = g_ref[...], sh_ref[0], sc_ref[0]
    h = _modulate(x_ref[...], g, sh, sc)
    hp = _modulate(xp_ref[...], g, sh[0:8] if sh.shape[0] > 1 else sh, sc[0:8] if sc.shape[0] > 1 else sc)
    hb = h.astype(BF16)
    pre = jnp.dot(hb, wqkv_ref[...], preferred_element_type=F32)
    pre8 = _bdot(hp, wqkv_ref[...])
    tpos = _tpos(tm, t_len)
    cw = cw_ref[...]
    conv = pre * cw[3:4]
    for d, cref in ((1, c1_ref), (2, c2_ref), (3, c3_ref)):
        past = jnp.where(tpos >= d, _shift_rows(pre, pre8, d), cref[0])
        conv = conv + past * cw[3 - d:4 - d]
    act = _silu(conv)
    nh = GD_H
    for hh in range(2 * nh):
        sl = slice(hh * GD_DK, (hh + 1) * GD_DK)
        seg = act[:, sl]
        seg = seg * lax.rsqrt(jnp.sum(seg * seg, axis=-1, keepdims=True) + 1e-6)
        if hh < nh:
            seg = seg * (GD_DK ** -0.5)
        qkv_ref[:, sl] = seg.astype(BF16)
    qkv_ref[:, 2 * nh * GD_DK:] = act[:, 2 * nh * GD_DK:].astype(BF16)
    z_ref[...] = jnp.dot(hb, wz_ref[...], preferred_element_type=F32).astype(BF16)
    beta_ref[...] = _sigmoid(jnp.dot(hb, wb_ref[...], preferred_element_type=F32))
    a_logit = jnp.dot(hb, wa_ref[...], preferred_element_type=F32)
    gdec_ref[...] = -jnp.exp(alog_ref[...]) * _softplus(a_logit + dtb_ref[...])


def _unit_lower_inverse(a, eye, masks):
    blk8, offs = masks
    a8 = jnp.where(blk8, a, 0.0)
    x = eye - a8
    y = _bdot(a8, a8)
    x = x + _bdot(x, y)
    y = _bdot(y, y)
    x = x + _bdot(x, y)
    for off in offs:
        x = x - _bdot(x, _bdot(jnp.where(off, a, 0.0), x))
    return x


def _inverse_masks(c):
    ri = lax.broadcasted_iota(I32, (c, c), 0)
    ci = lax.broadcasted_iota(I32, (c, c), 1)
    sr = lambda z, s: lax.shift_right_logical(z, jnp.full(z.shape, s, I32))
    blk8 = sr(ri, 3) == sr(ci, 3)
    offs = []
    m, lg = 8, 3
    while m < c:
        same = sr(ri, lg + 1) == sr(ci, lg + 1)
        lower = jnp.logical_and((sr(ri, lg) & 1) == 1, (sr(ci, lg) & 1) == 0)
        offs.append(jnp.logical_and(same, lower))
        m, lg = m * 2, lg + 1
    return ri, ci, (blk8, offs)


def _gdn_core_kernel(q_ref, k_ref, v_ref, z_ref, beta_ref, g_ref, s0_ref, nw_ref, y_ref, st_ref, state):
    cidx = pl.program_id(1)
    c = q_ref.shape[0]

    @pl.when(cidx == 0)
    def _():
        state[...] = s0_ref[0]

    ri, ci, masks = _inverse_masks(c)
    incl = ri >= ci
    strict = ri > ci
    eye = (ri == ci).astype(F32)
    cum = _fdot(incl.astype(F32), g_ref[...])
    beta = beta_ref[...]
    lane = lax.broadcasted_iota(I32, (c, LANES), 1)
    nw = nw_ref[...]
    for h in range(GD_H):
        sl = slice(h * GD_DK, (h + 1) * GD_DK)
        qh, kh = q_ref[:, sl], k_ref[:, sl]
        kf = kh.astype(F32)
        vh = v_ref[:, sl].astype(F32)
        cum_c = cum[:, h:h + 1]
        cum_r = lax.dot_general((lane == h).astype(F32), cum, (((1,), (1,)), ((), ())),
                                precision=HIGHEST, preferred_element_type=F32)
        dec = jnp.where(incl, jnp.exp(jnp.where(incl, cum_c - cum_r, 0.0)), 0.0)
        bcol = beta[:, h:h + 1]
        kb = kf * bcol
        a = jnp.where(strict, _bdot_nt(kb, kh) * dec, 0.0)
        x = _unit_lower_inverse(a, eye, masks)
        ecum = jnp.exp(cum_c)
        sol = _bdot(x, jnp.concatenate([vh * bcol, kb * ecum], axis=1))
        s = state[h]
        u = sol[:, :GD_DV] - _bdot(sol[:, GD_DV:], s)
        attn = _bdot_nt(qh, kh) * dec
        o = _bdot(qh.astype(F32) * ecum, s) + _bdot(attn, u)
        last = cum[c - 1:c, h:h + 1]
        state[h] = s * jnp.exp(last) + _bdot_tn(kf * jnp.exp(last - cum_c), u)
        on = o * lax.rsqrt(jnp.mean(o * o, axis=-1, keepdims=True) + NORM_EPS) * nw
        y_ref[:, sl] = (on * _silu(z_ref[:, sl].astype(F32))).astype(BF16)

    @pl.when(cidx == pl.num_programs(1) - 1)
    def _():
        st_ref[0] = state[...]


def _gdn_layer(grp, x, norm_g, mods, conv0, ssm0, w_in, conv_w, a_log, dt_bias, norm_w, w_o, chunk):
    b, t = grp.b, grp.t
    shift, scale, gate = mods
    sh, sh_spec = grp.rowmod(shift)
    sc, sc_spec = grp.rowmod(scale)
    kd = GD_H * GD_DK
    cstates, cspecs = [], []
    for d in (1, 2, 3):
        if conv0 is None:
            cs = jnp.zeros((b, 1, GD_C), F32)
        else:
            cs = jnp.concatenate([conv0[:, 3 - d:, :], jnp.zeros((b, t - d, GD_C), F32)], axis=1)
        cs, spec = grp.rowseq(cs)
        cstates.append(cs)
        cspecs.append(spec)
    bf = lambda z: z.astype(BF16)
    pad128 = lambda z: jnp.pad(z, ((0, 0), (0, LANES - z.shape[1])))
    w_qkv = bf(w_in[:, :GD_C])
    w_z = bf(w_in[:, GD_C:GD_C + kd])
    w_b = bf(pad128(w_in[:, GD_C + kd:GD_C + kd + GD_H]))
    w_a = bf(pad128(w_in[:, GD_C + kd + GD_H:]))
    weights = [w_qkv, w_z, w_b, w_a, conv_w, pad128(a_log.reshape(1, GD_H)), pad128(dt_bias.reshape(1, GD_H))]
    qkv, z, beta, gdec = pl.pallas_call(
        functools.partial(_gdn_proj_kernel, t),
        out_shape=(jax.ShapeDtypeStruct((grp.n, GD_C), BF16), jax.ShapeDtypeStruct((grp.n, kd), BF16),
                   jax.ShapeDtypeStruct((grp.n, LANES), F32), jax.ShapeDtypeStruct((grp.n, LANES), F32)),
        grid=(grp.tiles,),
        in_specs=[grp.rows(D), grp.prev8(D), _full((1, D)), sh_spec, sc_spec] + cspecs
        + [_full(z_.shape) for z_ in weights],
        out_specs=(grp.rows(GD_C), grp.rows(kd), grp.rows(LANES), grp.rows(LANES)),
        compiler_params=_cparams(("arbitrary",)),
        name="gdn_proj",
    )(x, x, norm_g.reshape(1, D), sh, sc, *cstates, *weights)
    tp = ((t + chunk - 1) // chunk) * chunk
    nc = tp // chunk
    qkv_p, z_p = _pad_time(qkv, b, t, tp), _pad_time(z, b, t, tp)
    beta_p, g_p = _pad_time(beta, b, t, tp), _pad_time(gdec, b, t, tp)
    if ssm0 is None:
        ssm0 = jnp.zeros((b, GD_H, GD_DK, GD_DV), F32)
    col = lambda j: pl.BlockSpec((chunk, kd), lambda bi, c: (bi * nc + c, j))
    lan = pl.BlockSpec((chunk, LANES), lambda bi, c: (bi * nc + c, 0))
    stt = pl.BlockSpec((1, GD_H, GD_DK, GD_DV), lambda bi, c: (bi, 0, 0, 0))
    y, st = pl.pallas_call(
        _gdn_core_kernel,
        out_shape=(jax.ShapeDtypeStruct((b * tp, kd), BF16),
                   jax.ShapeDtypeStruct((b, GD_H, GD_DK, GD_DV), F32)),
        grid=(b, nc),
        in_specs=[col(0), col(1), col(2), col(0), lan, lan, stt, _full((1, GD_DV))],
        out_specs=(col(0), stt),
        scratch_shapes=[pltpu.VMEM((GD_H, GD_DK, GD_DV), F32)],
        compiler_params=_cparams(("arbitrary", "arbitrary")),
        name="gdn_core",
    )(qkv_p, qkv_p, qkv_p, z_p, beta_p, g_p, ssm0, norm_w.reshape(1, GD_DV))
    x_new = _outproj_call(grp, x, _unpad_time(y, b, t, tp), bf(w_o), gate)
    nl = min(t, GD_CONV - 1)
    x_last = x.reshape(b, t, D)[:, t - nl:].reshape(b * nl, D)
    rep = lambda m: jnp.repeat(m, nl, axis=0)
    pre_last = _modrows_call(x_last, norm_g, rep(shift), rep(scale), w_qkv).reshape(b, nl, GD_C)
    if nl < GD_CONV - 1:
        pre_last = jnp.concatenate([conv0[:, nl:], pre_last], axis=1)
    return x_new, pre_last, st


def _ret_proj_kernel(x_ref, g_ref, sh_ref, sc_ref, cos_ref, sin_ref, w_ref, q_ref, k_ref, v_ref, gate_ref):
    h = _modulate(x_ref[...], g_ref[...], sh_ref[0], sc_ref[0]).astype(BF16)
    kd = RT_H * RT_DK
    vd = RT_H * RT_DV
    cos, sin = cos_ref[0], sin_ref[0]
    even = (lax.broadcasted_iota(I32, (1, kd), 1) & 1) == 0

    def rotary(z):
        swapped = jnp.where(even, pltpu.roll(z, kd - 1, 1), pltpu.roll(z, 1, 1))
        return z * cos + swapped * sin

    q_ref[...] = rotary(jnp.dot(h, w_ref[:, 0:kd], preferred_element_type=F32)).astype(BF16)
    k = rotary(jnp.dot(h, w_ref[:, kd:2 * kd], preferred_element_type=F32))
    k_ref[...] = (k * (RT_DK ** -0.5)).astype(BF16)
    v_ref[...] = jnp.dot(h, w_ref[:, 2 * kd:2 * kd + vd], preferred_element_type=F32).astype(BF16)
    gate_ref[...] = jnp.dot(h, w_ref[:, 2 * kd + vd:], preferred_element_type=F32).astype(BF16)


def _ret_core_kernel(q_ref, k_ref, v_ref, gate_ref, dm_ref, qd_ref, kd_ref, cd_ref, s0_ref, nw_ref,
                     y_ref, st_ref, state):
    cidx = pl.program_id(1)

    @pl.when(cidx == 0)
    def _():
        state[...] = s0_ref[0]

    for h in range(RT_H):
        ks = slice(h * RT_DK, (h + 1) * RT_DK)
        vs = slice(h * RT_DV, (h + 1) * RT_DV)
        qh, kh, vh = q_ref[:, ks], k_ref[:, ks], v_ref[:, vs]
        s = state[h]
        inner = _bdot_nt(qh, kh) * dm_ref[h]
        o = _bdot(inner, vh) + _bdot(qh, s) * qd_ref[h]
        state[h] = s * cd_ref[h] + _bdot_tn(kh.astype(F32) * kd_ref[h], vh)
        on = o * lax.rsqrt(jnp.mean(o * o, axis=-1, keepdims=True) + NORM_EPS) * nw_ref[:, vs]
        y_ref[:, vs] = (on * _silu(gate_ref[:, vs].astype(F32))).astype(BF16)

    @pl.when(cidx == pl.num_programs(1) - 1)
    def _():
        st_ref[0] = state[...]


def _ret_layer(grp, x, norm_g, mods, s0, pos0, w_in, norm_w, w_o, chunk):
    b, t = grp.b, grp.t
    shift, scale, gate = mods
    sh, sh_spec = grp.rowmod(shift)
    sc, sc_spec = grp.rowmod(scale)
    kd, vd = RT_H * RT_DK, RT_H * RT_DV
    half = RT_DK // 2
    inv = 1.0 / (10000.0 ** jnp.linspace(0.0, 1.0, half, dtype=F32))
    pos = jnp.arange(t, dtype=F32) + float(pos0)
    ang = pos[:, None] * inv[None, :]
    cos = jnp.repeat(jnp.cos(ang), 2, axis=1)
    sin = jnp.stack([-jnp.sin(ang), jnp.sin(ang)], axis=-1).reshape(t, RT_DK)
    cos4, cos_spec = grp.postab(jnp.tile(cos, (1, RT_H)))
    sin4, sin_spec = grp.postab(jnp.tile(sin, (1, RT_H)))
    wb = w_in.astype(BF16)
    q, k, v, gt = pl.pallas_call(
        _ret_proj_kernel,
        out_shape=(jax.ShapeDtypeStruct((grp.n, kd), BF16), jax.ShapeDtypeStruct((grp.n, kd), BF16),
                   jax.ShapeDtypeStruct((grp.n, vd), BF16), jax.ShapeDtypeStruct((grp.n, vd), BF16)),
        grid=(grp.tiles,),
        in_specs=[grp.rows(D), _full((1, D)), sh_spec, sc_spec, cos_spec, sin_spec, _full(wb.shape)],
        out_specs=(grp.rows(kd), grp.rows(kd), grp.rows(vd), grp.rows(vd)),
        compiler_params=_cparams(("arbitrary",)),
        name="ret_proj",
    )(x, norm_g.reshape(1, D), sh, sc, cos4, sin4, wb)
    tp = ((t + chunk - 1) // chunk) * chunk
    nc = tp // chunk
    nv = min(t, chunk)
    assert tp == t or nc == 1
    log_gamma = jnp.log1p(-jnp.exp2(-5.0 - jnp.arange(RT_H, dtype=F32)))
    idx = jnp.arange(chunk, dtype=F32)
    diff = idx[:, None] - idx[None, :]
    dmask = jnp.where(diff >= 0, jnp.exp(log_gamma[:, None, None] * jnp.maximum(diff, 0.0)), 0.0)
    q_dec = jnp.exp(log_gamma[:, None] * (idx + 1.0))[:, :, None]
    k_dec = jnp.exp(log_gamma[:, None] * jnp.maximum(nv - 1.0 - idx, 0.0))[:, :, None]
    c_dec = jnp.exp(log_gamma * nv)[:, None, None]
    if s0 is None:
        s0 = jnp.zeros((b, RT_H, RT_DK, RT_DV), F32)
    rowk = pl.BlockSpec((chunk, kd), lambda bi, c: (bi * nc + c, 0))
    rowv = pl.BlockSpec((chunk, vd), lambda bi, c: (bi * nc + c, 0))
    stt = pl.BlockSpec((1, RT_H, RT_DK, RT_DV), lambda bi, c: (bi, 0, 0, 0))
    y, st = pl.pallas_call(
        _ret_core_kernel,
        out_shape=(jax.ShapeDtypeStruct((b * tp, vd), BF16),
                   jax.ShapeDtypeStruct((b, RT_H, RT_DK, RT_DV), F32)),
        grid=(b, nc),
        in_specs=[rowk, rowk, rowv, rowv, _full(dmask.shape), _full(q_dec.shape), _full(k_dec.shape),
                  _full(c_dec.shape), stt, _full((1, vd))],
        out_specs=(rowv, stt),
        scratch_shapes=[pltpu.VMEM((RT_H, RT_DK, RT_DV), F32)],
        compiler_params=_cparams(("arbitrary", "arbitrary")),
        name="ret_core",
    )(_pad_time(q, b, t, tp), _pad_time(k, b, t, tp), _pad_time(v, b, t, tp), _pad_time(gt, b, t, tp),
      dmask, q_dec, k_dec, c_dec, s0, norm_w.reshape(1, vd))
    x_new = _outproj_call(grp, x, _unpad_time(y, b, t, tp), w_o.astype(BF16), gate)
    return x_new, st


def _hgrn_proj_kernel(layer, x_ref, g_ref, sh_ref, sc_ref, lbl_ref, w_ref,
                      q_ref, k_ref, lf_ref, v_ref, gate_ref):
    h = _modulate(x_ref[...], g_ref[...], sh_ref[0], sc_ref[0]).astype(BF16)
    ed = HG_H * HG_E
    logits = lbl_ref[...]
    e = jnp.exp(logits - jnp.max(logits, axis=0, keepdims=True))
    lrow = lax.broadcasted_iota(I32, logits.shape, 0)
    part = jnp.where(jnp.logical_and(lrow >= 1, lrow <= layer), e, 0.0)
    lb = jnp.sum(part, axis=0, keepdims=True) / jnp.sum(e, axis=0, keepdims=True)
    q_ref[...] = jnp.dot(h, w_ref[:, 0:ed], preferred_element_type=F32)
    f = lb + (1.0 - lb) * _sigmoid(jnp.dot(h, w_ref[:, ed:2 * ed], preferred_element_type=F32))
    k_ref[...] = 1.0 - f
    lf_ref[...] = jnp.log(f)
    v_ref[...] = jnp.dot(h, w_ref[:, 2 * ed:3 * ed], preferred_element_type=F32).astype(BF16)
    gate_ref[...] = jnp.dot(h, w_ref[:, 3 * ed:], preferred_element_type=F32).astype(BF16)


def _hgrn_core_kernel(q_ref, k_ref, lf_ref, v_ref, gate_ref, s0_ref, nw_ref, y_ref, st_ref, state):
    cidx = pl.program_id(1)
    c = q_ref.shape[0]

    @pl.when(cidx == 0)
    def _():
        for h in range(HG_H):
            state[h] = s0_ref[0, h].T

    ri = lax.broadcasted_iota(I32, (c, c), 0)
    ci = lax.broadcasted_iota(I32, (c, c), 1)
    ltri = (ri >= ci).astype(F32)
    rowi = lax.broadcasted_iota(I32, (c, 1), 0)
    for h in range(HG_H):
        sl = slice(h * HG_E, (h + 1) * HG_E)
        qh, kh = q_ref[:, sl], k_ref[:, sl]
        vh = v_ref[:, sl].astype(F32)
        cum = _fdot(ltri, lf_ref[:, sl])
        st = state[h]
        o = _bdot_nt(qh * jnp.exp(cum), st)
        for j in range(c):
            cj = cum[j:j + 1, :]
            causal = rowi >= j
            dec = jnp.exp(jnp.where(causal, cum - cj, 0.0))
            col = jnp.sum(qh * kh[j:j + 1, :] * dec, axis=-1, keepdims=True)
            col = jnp.where(causal, col, 0.0)
            o = o + col * vh[j:j + 1, :]
        last = cum[c - 1:c, :]
        state[h] = st * jnp.exp(last) + _bdot_tn(vh, kh * jnp.exp(last - cum))
        on = o * lax.rsqrt(jnp.mean(o * o, axis=-1, keepdims=True) + NORM_EPS) * nw_ref[:, sl]
        y_ref[:, sl] = (on * _silu(gate_ref[:, sl].astype(F32))).astype(BF16)

    @pl.when(cidx == pl.num_programs(1) - 1)
    def _():
        for h in range(HG_H):
            st_ref[0, h] = state[h].T


def _hgrn_layer(grp, x, norm_g, mods, s0, layer, lb_logits, w_in, norm_w, w_o, chunk):
    b, t = grp.b, grp.t
    shift, scale, gate = mods
    sh, sh_spec = grp.rowmod(shift)
    sc, sc_spec = grp.rowmod(scale)
    ed, vd = HG_H * HG_E, HG_H * HG_DV
    wb = w_in.astype(BF16)
    q, k, lf, v, gt = pl.pallas_call(
        functools.partial(_hgrn_proj_kernel, layer),
        out_shape=(jax.ShapeDtypeStruct((grp.n, ed), F32), jax.ShapeDtypeStruct((grp.n, ed), F32),
                   jax.ShapeDtypeStruct((grp.n, ed), F32), jax.ShapeDtypeStruct((grp.n, vd), BF16),
                   jax.ShapeDtypeStruct((grp.n, vd), BF16)),
        grid=(grp.tiles,),
        in_specs=[grp.rows(D), _full((1, D)), sh_spec, sc_spec, _full(lb_logits.shape), _full(wb.shape)],
        out_specs=(grp.rows(ed), grp.rows(ed), grp.rows(ed), grp.rows(vd), grp.rows(vd)),
        compiler_params=_cparams(("arbitrary",)),
        name="hgrn_proj",
    )(x, norm_g.reshape(1, D), sh, sc, lb_logits, wb)
    tp = ((t + chunk - 1) // chunk) * chunk
    nc = tp // chunk
    if s0 is None:
        s0 = jnp.zeros((b, HG_H, HG_E, HG_DV), F32)
    row = pl.BlockSpec((chunk, ed), lambda bi, c: (bi * nc + c, 0))
    stt = pl.BlockSpec((1, HG_H, HG_E, HG_DV), lambda bi, c: (bi, 0, 0, 0))
    y, st = pl.pallas_call(
        _hgrn_core_kernel,
        out_shape=(jax.ShapeDtypeStruct((b * tp, vd), BF16),
                   jax.ShapeDtypeStruct((b, HG_H, HG_E, HG_DV), F32)),
        grid=(b, nc),
        in_specs=[row, row, row, row, row, stt, _full((1, vd))],
        out_specs=(row, stt),
        scratch_shapes=[pltpu.VMEM((HG_H, HG_DV, HG_E), F32)],
        compiler_params=_cparams(("arbitrary", "arbitrary")),
        name="hgrn_core",
    )(_pad_time(q, b, t, tp), _pad_time(k, b, t, tp), _pad_time(lf, b, t, tp), _pad_time(v, b, t, tp),
      _pad_time(gt, b, t, tp), s0, norm_w.reshape(1, vd))
    x_new = _outproj_call(grp, x, _unpad_time(y, b, t, tp), w_o.astype(BF16), gate)
    return x_new, st


ROW_TILE = 256
MOE_TILE = 512
GDN_CHUNK, RET_CHUNK, HGRN_CHUNK = 64, 128, 16
SAMPLE_CHUNK = 16
PAST_LEN = 16384


def kernel(x_prompt, x_sample, c_prompt, c_sample, state_rwkv_wkv, state_rwkv_shift, state_gdn_ssm, state_gdn_conv, state_ret, state_hgrn, ada_w, ada_b, norm_mix, norm_ffn, norm_final, rwkv_mu, rwkv_w_rkv, rwkv_w0, rwkv_w1, rwkv_w2, rwkv_a0, rwkv_a1, rwkv_a2, rwkv_g1, rwkv_g2, rwkv_k_k, rwkv_k_a, rwkv_r_k, rwkv_ln_w, rwkv_ln_b, rwkv_w_o, gdn_w_in, gdn_conv_w, gdn_a_log, gdn_dt_bias, gdn_norm_w, gdn_w_o, ret_w_in, ret_norm_w, ret_w_o, hgrn_w_in, hgrn_lb_logits, hgrn_norm_w, hgrn_w_o, moe_w_router, moe_b_router, moe_w_gu, moe_b_gu, moe_w_down, moe_b_down):
    bp, tp, _ = x_prompt.shape
    bs, ts, _ = x_sample.shape
    gp, gs = _Group(bp, tp, ROW_TILE), _Group(bs, ts, ROW_TILE)
    mp, msg = _Group(bp, tp, MOE_TILE), _Group(bs, ts, MOE_TILE)
    ada = _ada_call(jnp.concatenate([c_prompt, c_sample], axis=0), ada_w, ada_b)
    xp = x_prompt.reshape(bp * tp, D)
    xs = x_sample.reshape(bs * ts, D)
    outs_p = {k: [] for k in ("wkv", "shift", "ssm", "conv", "ret", "hgrn")}
    outs_s = {k: [] for k in ("wkv", "shift", "ssm", "conv", "ret", "hgrn")}
    for i in range(DEPTH):
        kind, j = i % 4, i // 4
        m = ada[i].reshape(bp + bs, 6, D)
        mod_p = [m[:bp, n] for n in range(6)]
        mod_s = [m[bp:, n] for n in range(6)]
        g = norm_mix[i]
        if kind == 0:
            prm = (rwkv_mu[j], rwkv_w_rkv[j], rwkv_w0[j], rwkv_w1[j], rwkv_w2[j], rwkv_a0[j], rwkv_a1[j],
                   rwkv_a2[j], rwkv_g1[j], rwkv_g2[j], rwkv_k_k[j], rwkv_k_a[j], rwkv_r_k[j],
                   rwkv_ln_w[j], rwkv_ln_b[j], rwkv_w_o[j])
            xp, sh_p, wkv_p = _rwkv_layer(gp, xp, g, mod_p[:3], None, None, *prm)
            xs, sh_s, wkv_s = _rwkv_layer(gs, xs, g, mod_s[:3], state_rwkv_shift[j], state_rwkv_wkv[j], *prm)
            outs_p["wkv"].append(wkv_p); outs_p["shift"].append(sh_p)
            outs_s["wkv"].append(wkv_s); outs_s["shift"].append(sh_s)
        elif kind == 1:
            prm = (gdn_w_in[j], gdn_conv_w[j], gdn_a_log[j], gdn_dt_bias[j], gdn_norm_w[j], gdn_w_o[j])
            xp, cv_p, ss_p = _gdn_layer(gp, xp, g, mod_p[:3], None, None, *prm, GDN_CHUNK)
            xs, cv_s, ss_s = _gdn_layer(gs, xs, g, mod_s[:3], state_gdn_conv[j], state_gdn_ssm[j], *prm,
                                        SAMPLE_CHUNK)
            outs_p["ssm"].append(ss_p); outs_p["conv"].append(cv_p)
            outs_s["ssm"].append(ss_s); outs_s["conv"].append(cv_s)
        elif kind == 2:
            prm = (ret_w_in[j], ret_norm_w[j], ret_w_o[j])
            xp, r_p = _ret_layer(gp, xp, g, mod_p[:3], None, 0, *prm, RET_CHUNK)
            xs, r_s = _ret_layer(gs, xs, g, mod_s[:3], state_ret[j], PAST_LEN, *prm, SAMPLE_CHUNK)
            outs_p["ret"].append(r_p); outs_s["ret"].append(r_s)
        else:
            prm = (i, hgrn_lb_logits, hgrn_w_in[j], hgrn_norm_w[j], hgrn_w_o[j])
            xp, h_p = _hgrn_layer(gp, xp, g, mod_p[:3], None, *prm, HGRN_CHUNK)
            xs, h_s = _hgrn_layer(gs, xs, g, mod_s[:3], state_hgrn[j], *prm, SAMPLE_CHUNK)
            outs_p["hgrn"].append(h_p); outs_s["hgrn"].append(h_s)
        xp, xs = _moe([mp, msg], [xp, xs], norm_ffn[i], [mod_p[3:], mod_s[3:]], moe_w_router[i],
                      moe_b_router[i], moe_w_gu[i], moe_b_gu[i], moe_w_down[i], moe_b_down[i])
    y_prompt = _final_call(gp, xp, norm_final).reshape(bp, tp, D)
    y_sample = _final_call(gs, xs, norm_final).reshape(bs, ts, D)
    order = ("wkv", "shift", "ssm", "conv", "ret", "hgrn")
    return ((y_prompt, y_sample) + tuple(jnp.stack(outs_p[k]) for k in order)
            + tuple(jnp.stack(outs_s[k]) for k in order))
```

```python
import functools
import math

import jax
import jax.numpy as jnp
from jax import lax
from jax.experimental import pallas as pl
from jax.experimental.pallas import tpu as pltpu

F32 = jnp.float32
BF16 = jnp.bfloat16
I32 = jnp.int32
HIGHEST = lax.Precision.HIGHEST

D = 1024
DEPTH = 4
NORM_EPS = 1e-6
RW_H, RW_N = 16, 64
RW_GN_EPS = 64e-5
GD_H, GD_DK, GD_DV, GD_CONV = 8, 128, 128, 4
GD_C = 3 * GD_H * GD_DK
RT_H, RT_DK, RT_DV = 4, 256, 512
HG_H, HG_E, HG_DV = 8, 128, 128
N_EXPERTS, TOP_K, D_FF = 32, 4, 1024
SWIGLU_LIMIT, SWIGLU_ALPHA = 7.0, 1.702

LANES = 128
EXPERT_TILE = 512
TOKEN_BLOCK = 256
VMEM_LIMIT = 56 * 1024 * 1024


def _cparams(sem, vmem=VMEM_LIMIT):
    return pltpu.CompilerParams(dimension_semantics=sem, vmem_limit_bytes=vmem)


def _sigmoid(x):
    return 1.0 / (1.0 + jnp.exp(-x))


def _silu(x):
    return x * _sigmoid(x)


def _softplus(x):
    return jnp.maximum(x, 0.0) + jnp.log(1.0 + jnp.exp(-jnp.abs(x)))


def _modulate(x, g, shift, scale):
    ms = jnp.mean(x * x, axis=-1, keepdims=True)
    return (x * lax.rsqrt(ms + NORM_EPS) * g) * (1.0 + scale) + shift


def _bdot(a, b):
    return jnp.dot(a.astype(BF16), b.astype(BF16), preferred_element_type=F32)


def _bdot_nt(a, b):
    return lax.dot_general(a.astype(BF16), b.astype(BF16), (((1,), (1,)), ((), ())),
                           preferred_element_type=F32)


def _bdot_tn(a, b):
    return lax.dot_general(a.astype(BF16), b.astype(BF16), (((0,), (0,)), ((), ())),
                           preferred_element_type=F32)


def _fdot(a, b):
    return jnp.dot(a, b, precision=HIGHEST, preferred_element_type=F32)


class _Group:
    def __init__(self, b, t, tm):
        self.b, self.t, self.n = b, t, b * t
        self.tm = min(tm, self.n)
        assert self.n % self.tm == 0
        assert (self.t % self.tm == 0) or (self.tm % self.t == 0)
        self.per_batch = self.t % self.tm == 0
        self.tiles = self.n // self.tm

    def rows(self, width):
        return pl.BlockSpec((self.tm, width), lambda i: (i, 0))

    def rowmod(self, arr):
        w = arr.shape[-1]
        if self.per_batch:
            k = self.t // self.tm
            return arr.reshape(self.b, 1, w), pl.BlockSpec((1, 1, w), lambda i: (i // k, 0, 0))
        rep = jnp.repeat(arr, self.t, axis=0).reshape(self.tiles, self.tm, w)
        return rep, pl.BlockSpec((1, self.tm, w), lambda i: (i, 0, 0))

    def rowseq(self, arr):
        w = arr.shape[-1]
        if self.per_batch:
            assert arr.shape[1] == 1
            k = self.t // self.tm
            return arr, pl.BlockSpec((1, 1, w), lambda i: (i // k, 0, 0))
        return arr.reshape(self.tiles, self.tm, w), pl.BlockSpec((1, self.tm, w), lambda i: (i, 0, 0))

    def postab(self, tab):
        w = tab.shape[-1]
        if self.per_batch:
            k = self.t // self.tm
            return tab.reshape(k, self.tm, w), pl.BlockSpec((1, self.tm, w), lambda i: (i % k, 0, 0))
        rep = jnp.tile(tab, (self.tm // self.t, 1)).reshape(1, self.tm, w)
        return rep, pl.BlockSpec((1, self.tm, w), lambda i: (0, 0, 0))

    def prev8(self, width):
        k = self.tm // 8
        return pl.BlockSpec((8, width), lambda i: (jnp.maximum(i * k - 1, 0), 0))


def _full(shape):
    nd = len(shape)
    return pl.BlockSpec(shape, lambda *a: (0,) * nd)


def _tpos(tm, t):
    row = pl.program_id(0) * tm + lax.broadcasted_iota(I32, (tm, 1), 0)
    return row % t


def _shift_rows(cur, prev8, d):
    rolled = pltpu.roll(cur, d, 0)
    head = jnp.where(lax.broadcasted_iota(I32, (8, 1), 0) < d, pltpu.roll(prev8, d, 0), rolled[0:8])
    if cur.shape[0] == 8:
        return head
    return jnp.concatenate([head, rolled[8:]], axis=0)


def _ada_kernel(c_ref, w_ref, b_ref, o_ref):
    o_ref[0] = _bdot(_silu(c_ref[...]), w_ref[0]) + b_ref[0]


def _ada_call(c_all, ada_w, ada_b):
    nb = c_all.shape[0]
    tn = 1536
    return pl.pallas_call(
        _ada_kernel,
        out_shape=jax.ShapeDtypeStruct((DEPTH, nb, 6 * D), F32),
        grid=(DEPTH, 6 * D // tn),
        in_specs=[pl.BlockSpec((nb, D), lambda l, j: (0, 0)),
                  pl.BlockSpec((1, D, tn), lambda l, j: (l, 0, j)),
                  pl.BlockSpec((1, 1, tn), lambda l, j: (l, 0, j))],
        out_specs=pl.BlockSpec((1, nb, tn), lambda l, j: (l, 0, j)),
        compiler_params=_cparams(("arbitrary", "arbitrary")),
        name="adaln",
    )(c_all, ada_w, ada_b.reshape(DEPTH, 1, 6 * D))


def _modrows_kernel(x_ref, g_ref, sh_ref, sc_ref, *rest):
    h = _modulate(x_ref[...], g_ref[...], sh_ref[...], sc_ref[...])
    if len(rest) == 2:
        w_ref, o_ref = rest
        o_ref[...] = _bdot(h, w_ref[...])
    else:
        rest[0][...] = h


def _modrows_call(x, g, shift, scale, w=None):
    n = x.shape[0]
    args = [x, g.reshape(1, D), shift, scale]
    specs = [_full((n, D)), _full((1, D)), _full((n, D)), _full((n, D))]
    width = D
    if w is not None:
        args.append(w)
        specs.append(_full(w.shape))
        width = w.shape[1]
    return pl.pallas_call(
        _modrows_kernel,
        out_shape=jax.ShapeDtypeStruct((n, width), F32),
        grid=(1,),
        in_specs=specs,
        out_specs=_full((n, width)),
        compiler_params=_cparams(("arbitrary",)),
        name="modrows",
    )(*args)


def _outproj_kernel(has_mul, x_ref, y_ref, *rest):
    if has_mul:
        m_ref, w_ref, gt_ref, o_ref = rest
        y = y_ref[...].astype(F32) * m_ref[...].astype(F32)
    else:
        w_ref, gt_ref, o_ref = rest
        y = y_ref[...]
    o_ref[...] = x_ref[...] + gt_ref[0] * _bdot(y, w_ref[...])


def _outproj_call(grp, x, y, w_o, gate, mul=None):
    dy = y.shape[1]
    gt, gt_spec = grp.rowmod(gate)
    args = [x, y]
    specs = [grp.rows(D), grp.rows(dy)]
    if mul is not None:
        args.append(mul)
        specs.append(grp.rows(dy))
    args += [w_o, gt]
    specs += [_full(w_o.shape), gt_spec]
    return pl.pallas_call(
        functools.partial(_outproj_kernel, mul is not None),
        out_shape=jax.ShapeDtypeStruct((grp.n, D), F32),
        grid=(grp.tiles,),
        in_specs=specs,
        out_specs=grp.rows(D),
        compiler_params=_cparams(("arbitrary",)),
        name="outproj",
    )(*args)


def _final_kernel(x_ref, g_ref, o_ref):
    x = x_ref[...]
    ms = jnp.mean(x * x, axis=-1, keepdims=True)
    o_ref[...] = x * lax.rsqrt(ms + NORM_EPS) * g_ref[...]


def _final_call(grp, x, g):
    return pl.pallas_call(
        _final_kernel,
        out_shape=jax.ShapeDtypeStruct((grp.n, D), F32),
        grid=(grp.tiles,),
        in_specs=[grp.rows(D), _full((1, D))],
        out_specs=grp.rows(D),
        compiler_params=_cparams(("arbitrary",)),
        name="final_norm",
    )(x, g.reshape(1, D))


def _router_kernel(cin_ref, x_ref, g_ref, sh_ref, sc_ref, wr_ref, br_ref,
                   h_ref, idx_ref, gate_ref, rank_ref, cnt_ref, carry):
    i = pl.program_id(0)

    @pl.when(i == 0)
    def _():
        carry[...] = cin_ref[...]

    tm = x_ref.shape[0]
    h = _modulate(x_ref[...], g_ref[...], sh_ref[0], sc_ref[0])
    h_ref[...] = h
    logits = _fdot(h, wr_ref[...]) + br_ref[...]
    lane = lax.broadcasted_iota(I32, logits.shape, 1)
    work = logits
    sel = jnp.zeros(logits.shape, jnp.bool_)
    picks, vals = [], []
    for _ in range(TOP_K):
        m = jnp.max(work, axis=-1, keepdims=True)
        idx = jnp.min(jnp.where(work == m, lane, N_EXPERTS), axis=-1, keepdims=True)
        pick = lane == idx
        picks.append((idx, pick))
        vals.append(m)
        sel = jnp.logical_or(sel, pick)
        work = jnp.where(pick, -jnp.inf, work)
    es = [jnp.exp(v - vals[0]) for v in vals]
    denom = es[0] + es[1] + es[2] + es[3]
    self_f = sel.astype(F32)
    tri = (lax.broadcasted_iota(I32, (tm, tm), 0) > lax.broadcasted_iota(I32, (tm, tm), 1))
    local = jnp.dot(tri.astype(BF16), self_f.astype(BF16), preferred_element_type=F32)
    rank = local + carry[...]
    carry[...] = carry[...] + jnp.sum(self_f, axis=0, keepdims=True)
    cnt_ref[...] = carry[...]
    lane_o = lax.broadcasted_iota(I32, (tm, LANES), 1)
    idx_o = jnp.zeros((tm, LANES), I32)
    gate_o = jnp.zeros((tm, LANES), F32)
    rank_o = jnp.zeros((tm, LANES), I32)
    for k in range(TOP_K):
        idx, pick = picks[k]
        rk = jnp.sum(jnp.where(pick, rank, 0.0), axis=-1, keepdims=True)
        idx_o = jnp.where(lane_o == k, idx, idx_o)
        gate_o = jnp.where(lane_o == k, es[k] / denom, gate_o)
        rank_o = jnp.where(lane_o == k, rk.astype(I32), rank_o)
    idx_ref[...] = idx_o
    gate_ref[...] = gate_o
    rank_ref[...] = rank_o


def _router_call(grp, counts_in, x, g, shift, scale, w_router, b_router):
    sh, sh_spec = grp.rowmod(shift)
    sc, sc_spec = grp.rowmod(scale)
    pad = pl.BlockSpec((grp.tm, LANES), lambda i: (i, 0))
    return pl.pallas_call(
        _router_kernel,
        out_shape=(jax.ShapeDtypeStruct((grp.n, D), F32),
                   jax.ShapeDtypeStruct((grp.n, LANES), I32),
                   jax.ShapeDtypeStruct((grp.n, LANES), F32),
                   jax.ShapeDtypeStruct((grp.n, LANES), I32),
                   jax.ShapeDtypeStruct((1, N_EXPERTS), F32)),
        grid=(grp.tiles,),
        in_specs=[_full((1, N_EXPERTS)), grp.rows(D), _full((1, D)), sh_spec, sc_spec,
                  _full((D, N_EXPERTS)), _full((1, N_EXPERTS))],
        out_specs=(grp.rows(D), pad, pad, pad, _full((1, N_EXPERTS))),
        scratch_shapes=[pltpu.VMEM((1, N_EXPERTS), F32)],
        compiler_params=_cparams(("arbitrary",)),
        name="moe_router",
    )(counts_in, x, g.reshape(1, D), sh, sc, w_router, b_router.reshape(1, N_EXPERTS))


def _dispatch_kernel(cnt_ref, off_ref, nv_ref, pos_ref, h_ref, xs_ref, zbuf, sem, sem_z):
    i = pl.program_id(0)

    def issue(t, c):
        for k in range(TOP_K):
            pltpu.make_async_copy(h_ref.at[pl.ds(t, 1)], xs_ref.at[pl.ds(pos_ref[t * TOP_K + k], 1)],
                                  sem).start()
        return c

    lax.fori_loop(0, TOKEN_BLOCK, issue, 0, unroll=8)
    for k in range(TOP_K):
        pltpu.make_async_copy(h_ref, xs_ref.at[pl.ds(0, TOKEN_BLOCK)], sem).wait()

    @pl.when(i == pl.num_programs(0) - 1)
    def _():
        zbuf[...] = jnp.zeros(zbuf.shape, F32)
        bits = [1 << s for s in range(EXPERT_TILE.bit_length() - 2, 2, -1)]

        def pad_copies(e, wait):
            n = cnt_ref[e]
            start = off_ref[e] + n
            end = off_ref[e] + ((n + EXPERT_TILE - 1) // EXPERT_TILE) * EXPERT_TILE
            head = (-start) & 7

            def one(r, c):
                cp = pltpu.make_async_copy(zbuf.at[pl.ds(0, 1)], xs_ref.at[pl.ds(start + r, 1)], sem_z)
                if wait:
                    cp.wait()
                else:
                    cp.start()
                return c

            lax.fori_loop(0, head, one, 0)
            start8 = start + head
            rem = end - start8
            for bit in bits:
                @pl.when((rem & bit) != 0)
                def _():
                    s = pl.multiple_of(start8 + (rem & ~(2 * bit - 1)), 8)
                    cp = pltpu.make_async_copy(zbuf.at[pl.ds(0, bit)], xs_ref.at[pl.ds(s, bit)], sem_z)
                    if wait:
                        cp.wait()
                    else:
                        cp.start()

        def tail_copy(j):
            return pltpu.make_async_copy(zbuf, xs_ref.at[pl.ds(j * EXPERT_TILE, EXPERT_TILE)], sem_z)

        n_tiles = xs_ref.shape[0] // EXPERT_TILE
        for wait in (False, True):
            def per_expert(e, c):
                pad_copies(e, wait)
                return c

            def per_tail(j, c):
                if wait:
                    tail_copy(j).wait()
                else:
                    tail_copy(j).start()
                return c

            lax.fori_loop(0, N_EXPERTS, per_expert, 0)
            lax.fori_loop(nv_ref[0], n_tiles, per_tail, 0)


def _dispatch_call(counts, offsets, n_valid, pos_flat, h, n_rows):
    n = h.shape[0]
    return pl.pallas_call(
        _dispatch_kernel,
        out_shape=jax.ShapeDtypeStruct((n_rows, D), F32),
        grid_spec=pltpu.PrefetchScalarGridSpec(
            num_scalar_prefetch=3,
            grid=(n // TOKEN_BLOCK,),
            in_specs=[pl.BlockSpec((TOKEN_BLOCK * TOP_K,), lambda i, c, o, v: (i,), memory_space=pltpu.SMEM),
                      pl.BlockSpec((TOKEN_BLOCK, D), lambda i, c, o, v: (i, 0))],
            out_specs=pl.BlockSpec(memory_space=pl.ANY),
            scratch_shapes=[pltpu.VMEM((EXPERT_TILE, D), F32), pltpu.SemaphoreType.DMA(()),
                            pltpu.SemaphoreType.DMA(())]),
        compiler_params=_cparams(("arbitrary",)),
        name="moe_dispatch",
    )(counts, offsets, n_valid, pos_flat, h)


def _expert_kernel(te_ref, nv_ref, x_ref, wgu_ref, bgu_ref, wd_ref, bd_ref, o_ref, wgu_s, wd_s):
    j = pl.program_id(0)
    fresh = jnp.logical_or(j == 0, te_ref[j] != te_ref[jnp.maximum(j - 1, 0)])

    @pl.when(jnp.logical_and(j < nv_ref[0], fresh))
    def _():
        wgu_s[...] = wgu_ref[0, 0].astype(BF16)
        wd_s[...] = wd_ref[0, 0].astype(BF16)

    @pl.when(j < nv_ref[0])
    def _():
        x = x_ref[...].astype(BF16)
        gu = jnp.dot(x, wgu_s[...], preferred_element_type=F32) + bgu_ref[0, 0]
        gl = jnp.minimum(gu[:, :D_FF], SWIGLU_LIMIT)
        up = jnp.clip(gu[:, D_FF:], -SWIGLU_LIMIT, SWIGLU_LIMIT)
        act = (up + 1.0) * gl * _sigmoid(SWIGLU_ALPHA * gl)
        o_ref[...] = jnp.dot(act.astype(BF16), wd_s[...], preferred_element_type=F32) + bd_ref[0, 0]

    @pl.when(j >= nv_ref[0])
    def _():
        o_ref[...] = jnp.zeros(o_ref.shape, F32)


def _expert_call(layer, tile_expert, n_valid, xs, w_gu, b_gu, w_down, b_down):
    n_rows = xs.shape[0]
    g = n_rows // EXPERT_TILE
    return pl.pallas_call(
        _expert_kernel,
        out_shape=jax.ShapeDtypeStruct((n_rows, D), F32),
        grid_spec=pltpu.PrefetchScalarGridSpec(
            num_scalar_prefetch=2,
            grid=(g,),
            in_specs=[pl.BlockSpec((EXPERT_TILE, D), lambda j, te, nv: (jnp.minimum(j, nv[0] - 1), 0)),
                      pl.BlockSpec((1, 1, D, 2 * D_FF), lambda j, te, nv: (layer, te[j], 0, 0)),
                      pl.BlockSpec((1, 1, 1, 2 * D_FF), lambda j, te, nv: (layer, te[j], 0, 0)),
                      pl.BlockSpec((1, 1, D_FF, D), lambda j, te, nv: (layer, te[j], 0, 0)),
                      pl.BlockSpec((1, 1, 1, D), lambda j, te, nv: (layer, te[j], 0, 0))],
            out_specs=pl.BlockSpec((EXPERT_TILE, D), lambda j, te, nv: (j, 0)),
            scratch_shapes=[pltpu.VMEM((D, 2 * D_FF), BF16), pltpu.VMEM((D_FF, D), BF16)]),
        compiler_params=_cparams(("arbitrary",)),
        name="moe_experts",
    )(tile_expert, n_valid, xs, w_gu, b_gu.reshape(-1, N_EXPERTS, 1, 2 * D_FF), w_down,
      b_down.reshape(-1, N_EXPERTS, 1, D))


def _combine_kernel(pos_ref, x_ref, gate_ref, gt_ref, ys_ref, o_ref, buf, sem):
    def issue(t, c):
        for k in range(TOP_K):
            pltpu.make_async_copy(ys_ref.at[pl.ds(pos_ref[t * TOP_K + k], 1)], buf.at[k, pl.ds(t, 1)],
                                  sem).start()
        return c

    lax.fori_loop(0, TOKEN_BLOCK, issue, 0, unroll=8)
    for k in range(TOP_K):
        pltpu.make_async_copy(ys_ref.at[pl.ds(0, TOKEN_BLOCK)], buf.at[k], sem).wait()
    gate = gate_ref[...]
    f = gate[:, 0:1] * buf[0]
    for k in range(1, TOP_K):
        f = f + gate[:, k:k + 1] * buf[k]
    o_ref[...] = x_ref[...] + gt_ref[0] * f


def _combine_call(grp, pos_flat, x, gate_pad, gate2, ys):
    gt, gt_spec = grp.rowmod(gate2)
    assert grp.tm == TOKEN_BLOCK
    return pl.pallas_call(
        _combine_kernel,
        out_shape=jax.ShapeDtypeStruct((grp.n, D), F32),
        grid=(grp.tiles,),
        in_specs=[pl.BlockSpec((TOKEN_BLOCK * TOP_K,), lambda i: (i,), memory_space=pltpu.SMEM),
                  grp.rows(D), grp.rows(LANES), gt_spec, pl.BlockSpec(memory_space=pl.ANY)],
        out_specs=grp.rows(D),
        scratch_shapes=[pltpu.VMEM((TOP_K, TOKEN_BLOCK, D), F32), pltpu.SemaphoreType.DMA(())],
        compiler_params=_cparams(("arbitrary",)),
        name="moe_combine",
    )(pos_flat, x, gate_pad, gt, ys)


def _moe(layer, groups, xs_in, norm_g, mods, w_router, b_router, w_gu, b_gu, w_down, b_down):
    counts = jnp.zeros((1, N_EXPERTS), F32)
    hs, idxs, gates, ranks = [], [], [], []
    for grp, x, (sh2, sc2, _) in zip(groups, xs_in, mods):
        h, idx, gate, rank, counts = _router_call(grp, counts, x, norm_g, sh2, sc2, w_router, b_router)
        hs.append(h)
        idxs.append(idx[:, :TOP_K])
        gates.append(gate)
        ranks.append(rank[:, :TOP_K])
    h_all = jnp.concatenate(hs, axis=0)
    idx_all = jnp.concatenate(idxs, axis=0)
    rank_all = jnp.concatenate(ranks, axis=0)
    n = h_all.shape[0]
    cnt = counts[0].astype(I32)
    padded = ((cnt + EXPERT_TILE - 1) // EXPERT_TILE) * EXPERT_TILE
    ends = jnp.cumsum(padded)
    offsets = ends - padded
    n_tiles = (n * TOP_K + N_EXPERTS * (EXPERT_TILE - 1)) // EXPERT_TILE
    n_rows = n_tiles * EXPERT_TILE
    pos = (jnp.take(offsets, idx_all) + rank_all).astype(I32)
    pos_flat = pos.reshape(n * TOP_K)
    n_valid = (ends[-1] // EXPERT_TILE).astype(I32)
    tile_start = jnp.arange(n_tiles, dtype=I32) * EXPERT_TILE
    tile_start = jnp.minimum(tile_start, ends[-1] - EXPERT_TILE)
    tile_expert = jnp.sum(tile_start[:, None] >= ends[None, :], axis=1).astype(I32)
    n_valid = n_valid.reshape(1)
    xs = _dispatch_call(cnt, offsets.astype(I32), n_valid, pos_flat, h_all, n_rows)
    ys = _expert_call(layer, tile_expert, n_valid, xs, w_gu, b_gu, w_down, b_down)
    outs = []
    start = 0
    for grp, x, gate, (_, _, gt2) in zip(groups, xs_in, gates, mods):
        cgrp = _Group(grp.b, grp.t, TOKEN_BLOCK)
        p = lax.dynamic_slice_in_dim(pos_flat, start * TOP_K, grp.n * TOP_K)
        outs.append(_combine_call(cgrp, p, x, gate, gt2, ys))
        start += grp.n
    return outs


def _rwkv_proj_kernel(t_len, x_ref, xp_ref, g_ref, sh_ref, sc_ref, s0_ref, mu_ref, wrkv_ref, w0_ref,
                      w1_ref, w2_ref, a0_ref, a1_ref, a2_ref, g1_ref, g2_ref,
                      r_ref, w_ref, k_ref, v_ref, a_ref, gg_ref):
    tm = x_ref.shape[0]
    g, sh, sc = g_ref[...], sh_ref[0], sc_ref[0]
    h = _modulate(x_ref[...], g, sh, sc)
    hp = _modulate(xp_ref[...], g, sh[0:8] if sh.shape[0] > 1 else sh, sc[0:8] if sc.shape[0] > 1 else sc)
    prev = jnp.where(_tpos(tm, t_len) == 0, s0_ref[0], _shift_rows(h, hp, 1))
    dx = prev - h
    mu = mu_ref[...]
    xr, xw, xk, xv, xa, xg = [h + dx * mu[n:n + 1] for n in range(6)]
    r_ref[...] = _bdot(xr, wrkv_ref[0])
    k_ref[...] = _bdot(xk, wrkv_ref[1])
    v_ref[...] = _bdot(xv, wrkv_ref[2])
    w_log = -_softplus(-(w0_ref[...] + _bdot(jnp.tanh(_bdot(xw, w1_ref[...])), w2_ref[...]))) - 0.5
    w_ref[...] = jnp.exp(-jnp.exp(w_log))
    a_ref[...] = _sigmoid(a0_ref[...] + _bdot(_bdot(xa, a1_ref[...]), a2_ref[...]))
    gg_ref[...] = _bdot(_sigmoid(_bdot(xg, g1_ref[...])), g2_ref[...]).astype(BF16)


def _rwkv_core_kernel(r_ref, w_ref, k_ref, v_ref, a_ref, kk_p, ka_p, rk_p, lnw_p, lnb_p, s0_ref,
                      y_ref, st_ref, state, kk_s, b_s, km_s):
    j = pl.program_id(1)
    tc = r_ref.shape[0]
    n = RW_N

    @pl.when(j == 0)
    def _():
        state[...] = s0_ref[...]

    def step(t, c):
        kt, at, vt, rt = k_ref[t], a_ref[t], v_ref[t], r_ref[t]
        kk = kt * kk_p[...]
        kk = kk * lax.rsqrt(jnp.sum(kk * kk, axis=0, keepdims=True) + 1e-6)
        km = kt * (1.0 + (at - 1.0) * ka_p[...])
        kk_s[...] = kk
        b_s[...] = kk * at
        km_s[...] = km
        sa = jnp.zeros((n, LANES), F32)
        for kx in range(n):
            sa = sa + state[kx] * kk_s[pl.ds(kx, 1), :]
        y = jnp.zeros((n, LANES), F32)
        for kx in range(n):
            s_new = (state[kx] * w_ref[t, pl.ds(kx, 1), :] - sa * b_s[pl.ds(kx, 1), :]
                     + vt * km_s[pl.ds(kx, 1), :])
            state[kx] = s_new
            y = y + s_new * r_ref[t, pl.ds(kx, 1), :]
        mean = jnp.mean(y, axis=0, keepdims=True)
        yc = y - mean
        var = jnp.mean(yc * yc, axis=0, keepdims=True)
        bonus = jnp.sum(rt * km * rk_p[...], axis=0, keepdims=True) * vt
        y_ref[t] = yc * lax.rsqrt(var + RW_GN_EPS) * lnw_p[...] + lnb_p[...] + bonus
        return c

    lax.fori_loop(0, tc, step, 0)

    @pl.when(j == pl.num_programs(1) - 1)
    def _():
        st_ref[...] = state[...]


def _rwkv_layer(grp, x, norm_g, mods, shift0, wkv0, mu, w_rkv, w0, w1, w2, a0, a1, a2, g1, g2,
                k_k, k_a, r_k, ln_w, ln_b, w_o):
    b, t = grp.b, grp.t
    shift, scale, gate = mods
    sh, sh_spec = grp.rowmod(shift)
    sc, sc_spec = grp.rowmod(scale)
    if shift0 is None:
        s0 = jnp.zeros((b, 1, D), F32)
    else:
        s0 = jnp.concatenate([shift0[:, None, :], jnp.zeros((b, t - 1, D), F32)], axis=1)
    s0, s0_spec = grp.rowseq(s0)
    bf = lambda z: z.astype(BF16)
    row = lambda z: z.reshape(1, -1)
    weights = [mu, bf(w_rkv), row(w0), bf(w1), bf(w2), row(a0), bf(a1), bf(a2), bf(g1), bf(g2)]
    outs = pl.pallas_call(
        functools.partial(_rwkv_proj_kernel, t),
        out_shape=tuple(jax.ShapeDtypeStruct((grp.n, D), F32) for _ in range(5))
        + (jax.ShapeDtypeStruct((grp.n, D), BF16),),
        grid=(grp.tiles,),
        in_specs=[grp.rows(D), grp.prev8(D), _full((1, D)), sh_spec, sc_spec, s0_spec]
        + [_full(z.shape) for z in weights],
        out_specs=tuple(grp.rows(D) for _ in range(6)),
        compiler_params=_cparams(("arbitrary",)),
        name="rwkv_proj",
    )(x, x, row(norm_g), sh, sc, s0, *weights)
    r, w, k, v, a, gg = outs
    bh = b * RW_H

    def to_core(z):
        return z.reshape(b, t, RW_H, RW_N).transpose(1, 3, 0, 2).reshape(t, RW_N, bh)

    def ptile(p):
        return jnp.tile(p.reshape(RW_H, RW_N).T, (1, b))

    if wkv0 is None:
        st0 = jnp.zeros((RW_N, RW_N, bh), F32)
    else:
        st0 = wkv0.transpose(3, 2, 0, 1).reshape(RW_N, RW_N, bh)
    tc = min(t, 16)
    seq = pl.BlockSpec((tc, RW_N, LANES), lambda q, j: (j, 0, q))
    par = pl.BlockSpec((RW_N, LANES), lambda q, j: (0, q))
    stt = pl.BlockSpec((RW_N, RW_N, LANES), lambda q, j: (0, 0, q))
    y, st = pl.pallas_call(
        _rwkv_core_kernel,
        out_shape=(jax.ShapeDtypeStruct((t, RW_N, bh), F32),
                   jax.ShapeDtypeStruct((RW_N, RW_N, bh), F32)),
        grid=(bh // LANES, t // tc),
        in_specs=[seq] * 5 + [par] * 5 + [stt],
        out_specs=(seq, stt),
        scratch_shapes=[pltpu.VMEM((RW_N, RW_N, LANES), F32)] + [pltpu.VMEM((RW_N, LANES), F32)] * 3,
        compiler_params=_cparams(("arbitrary", "arbitrary")),
        name="rwkv_core",
    )(to_core(r), to_core(w), to_core(k), to_core(v), to_core(a),
      ptile(k_k), ptile(k_a), ptile(r_k.reshape(-1)), ptile(ln_w), ptile(ln_b), st0)
    y_rows = y.reshape(t, RW_N, b, RW_H).transpose(2, 0, 3, 1).reshape(grp.n, D)
    x_new = _outproj_call(grp, x, y_rows, bf(w_o), gate, mul=gg)
    new_wkv = st.reshape(RW_N, RW_N, b, RW_H).transpose(2, 3, 1, 0)
    x_last = x.reshape(b, t, D)[:, -1]
    new_shift = _modrows_call(x_last, norm_g, shift, scale)
    return x_new, new_shift, new_wkv


def _pad_time(z, b, t, tp):
    if tp == t:
        return z
    w = z.shape[-1]
    return jnp.pad(z.reshape(b, t, w), ((0, 0), (0, tp - t), (0, 0))).reshape(b * tp, w)


def _unpad_time(z, b, t, tp):
    if tp == t:
        return z
    w = z.shape[-1]
    return z.reshape(b, tp, w)[:, :t].reshape(b * t, w)


def _gdn_proj_kernel(t_len, x_ref, xp_ref, g_ref, sh_ref, sc_ref, c1_ref, c2_ref, c3_ref, wqkv_ref,
                     wz_ref, wb_ref, wa_ref, cw_ref, alog_ref, dtb_ref,
                     qkv_ref, z_ref, beta_ref, gdec_ref):
    tm = x_ref.shape[0]
    g, sh, sc = g_ref[...], sh_ref[0], sc_ref[0]
    h = _modulate(x_ref[...], g, sh, sc)
    hp = _modulate(xp_ref[...], g, sh[0:8] if sh.shape[0] > 1 else sh, sc[0:8] if sc.shape[0] > 1 else sc)
    hb = h.astype(BF16)
    pre = jnp.dot(hb, wqkv_ref[...], preferred_element_type=F32)
    pre8 = _bdot(hp, wqkv_ref[...])
    tpos = _tpos(tm, t_len)
    cw = cw_ref[...]
    conv = pre * cw[3:4]
    for d, cref in ((1, c1_ref), (2, c2_ref), (3, c3_ref)):
        past = jnp.where(tpos >= d, _shift_rows(pre, pre8, d), cref[0])
        conv = conv + past * cw[3 - d:4 - d]
    act = _silu(conv)
    nh = GD_H
    for hh in range(2 * nh):
        sl = slice(hh * GD_DK, (hh + 1) * GD_DK)
        seg = act[:, sl]
        seg = seg * lax.rsqrt(jnp.sum(seg * seg, axis=-1, keepdims=True) + 1e-6)
        if hh < nh:
            seg = seg * (GD_DK ** -0.5)
        qkv_ref[:, sl] = seg.astype(BF16)
    qkv_ref[:, 2 * nh * GD_DK:] = act[:, 2 * nh * GD_DK:].astype(BF16)
    z_ref[...] = jnp.dot(hb, wz_ref[...], preferred_element_type=F32).astype(BF16)
    beta_ref[...] = _sigmoid(jnp.dot(hb, wb_ref[...], preferred_element_type=F32))
    a_logit = jnp.dot(hb, wa_ref[...], preferred_element_type=F32)
    gdec_ref[...] = -jnp.exp(alog_ref[...]) * _softplus(a_logit + dtb_ref[...])


def _unit_lower_inverse(a, eye, masks):
    blk8, offs = masks
    a8 = jnp.where(blk8, a, 0.0)
    x = eye - a8
    y = _bdot(a8, a8)
    x = x + _bdot(x, y)
    y = _bdot(y, y)
    x = x + _bdot(x, y)
    for off in offs:
        x = x - _bdot(x, _bdot(jnp.where(off, a, 0.0), x))
    return x


def _inverse_masks(c):
    ri = lax.broadcasted_iota(I32, (c, c), 0)
    ci = lax.broadcasted_iota(I32, (c, c), 1)
    sr = lambda z, s: lax.shift_right_logical(z, jnp.full(z.shape, s, I32))
    blk8 = sr(ri, 3) == sr(ci, 3)
    offs = []
    m, lg = 8, 3
    while m < c:
        same = sr(ri, lg + 1) == sr(ci, lg + 1)
        lower = jnp.logical_and((sr(ri, lg) & 1) == 1, (sr(ci, lg) & 1) == 0)
        offs.append(jnp.logical_and(same, lower))
        m, lg = m * 2, lg + 1
    return ri, ci, (blk8, offs)


def _gdn_core_kernel(q_ref, k_ref, v_ref, z_ref, beta_ref, g_ref, s0_ref, nw_ref, y_ref, st_ref, state):
    cidx = pl.program_id(1)
    c = q_ref.shape[0]

    @pl.when(cidx == 0)
    def _():
        state[...] = s0_ref[0]

    ri, ci, masks = _inverse_masks(c)
    incl = ri >= ci
    strict = ri > ci
    eye = (ri == ci).astype(F32)
    cum = _fdot(incl.astype(F32), g_ref[...])
    beta = beta_ref[...]
    lane = lax.broadcasted_iota(I32, (c, LANES), 1)
    nw = nw_ref[...]
    for h in range(GD_H):
        sl = slice(h * GD_DK, (h + 1) * GD_DK)
        qh, kh = q_ref[:, sl], k_ref[:, sl]
        kf = kh.astype(F32)
        vh = v_ref[:, sl].astype(F32)
        cum_c = cum[:, h:h + 1]
        cum_r = lax.dot_general((lane == h).astype(F32), cum, (((1,), (1,)), ((), ())),
                                precision=HIGHEST, preferred_element_type=F32)
        dec = jnp.where(incl, jnp.exp(jnp.where(incl, cum_c - cum_r, 0.0)), 0.0)
        bcol = beta[:, h:h + 1]
        kb = kf * bcol
        a = jnp.where(strict, _bdot_nt(kb, kh) * dec, 0.0)
        x = _unit_lower_inverse(a, eye, masks)
        ecum = jnp.exp(cum_c)
        sol = _bdot(x, jnp.concatenate([vh * bcol, kb * ecum], axis=1))
        s = state[h]
        u = sol[:, :GD_DV] - _bdot(sol[:, GD_DV:], s)
        attn = _bdot_nt(qh, kh) * dec
        o = _bdot(qh.astype(F32) * ecum, s) + _bdot(attn, u)
        last = cum[c - 1:c, h:h + 1]
        state[h] = s * jnp.exp(last) + _bdot_tn(kf * jnp.exp(last - cum_c), u)
        on = o * lax.rsqrt(jnp.mean(o * o, axis=-1, keepdims=True) + NORM_EPS) * nw
        y_ref[:, sl] = (on * _silu(z_ref[:, sl].astype(F32))).astype(BF16)

    @pl.when(cidx == pl.num_programs(1) - 1)
    def _():
        st_ref[0] = state[...]


def _gdn_layer(grp, x, norm_g, mods, conv0, ssm0, w_in, conv_w, a_log, dt_bias, norm_w, w_o, chunk):
    b, t = grp.b, grp.t
    shift, scale, gate = mods
    sh, sh_spec = grp.rowmod(shift)
    sc, sc_spec = grp.rowmod(scale)
    kd = GD_H * GD_DK
    cstates, cspecs = [], []
    for d in (1, 2, 3):
        if conv0 is None:
            cs = jnp.zeros((b, 1, GD_C), F32)
        else:
            cs = jnp.concatenate([conv0[:, 3 - d:, :], jnp.zeros((b, t - d, GD_C), F32)], axis=1)
        cs, spec = grp.rowseq(cs)
        cstates.append(cs)
        cspecs.append(spec)
    bf = lambda z: z.astype(BF16)
    pad128 = lambda z: jnp.pad(z, ((0, 0), (0, LANES - z.shape[1])))
    w_qkv = bf(w_in[:, :GD_C])
    w_z = bf(w_in[:, GD_C:GD_C + kd])
    w_b = bf(pad128(w_in[:, GD_C + kd:GD_C + kd + GD_H]))
    w_a = bf(pad128(w_in[:, GD_C + kd + GD_H:]))
    weights = [w_qkv, w_z, w_b, w_a, conv_w, pad128(a_log.reshape(1, GD_H)), pad128(dt_bias.reshape(1, GD_H))]
    qkv, z, beta, gdec = pl.pallas_call(
        functools.partial(_gdn_proj_kernel, t),
        out_shape=(jax.ShapeDtypeStruct((grp.n, GD_C), BF16), jax.ShapeDtypeStruct((grp.n, kd), BF16),
                   jax.ShapeDtypeStruct((grp.n, LANES), F32), jax.ShapeDtypeStruct((grp.n, LANES), F32)),
        grid=(grp.tiles,),
        in_specs=[grp.rows(D), grp.prev8(D), _full((1, D)), sh_spec, sc_spec] + cspecs
        + [_full(z_.shape) for z_ in weights],
        out_specs=(grp.rows(GD_C), grp.rows(kd), grp.rows(LANES), grp.rows(LANES)),
        compiler_params=_cparams(("arbitrary",)),
        name="gdn_proj",
    )(x, x, norm_g.reshape(1, D), sh, sc, *cstates, *weights)
    tp = ((t + chunk - 1) // chunk) * chunk
    nc = tp // chunk
    qkv_p, z_p = _pad_time(qkv, b, t, tp), _pad_time(z, b, t, tp)
    beta_p, g_p = _pad_time(beta, b, t, tp), _pad_time(gdec, b, t, tp)
    if ssm0 is None:
        ssm0 = jnp.zeros((b, GD_H, GD_DK, GD_DV), F32)
    col = lambda j: pl.BlockSpec((chunk, kd), lambda bi, c: (bi * nc + c, j))
    lan = pl.BlockSpec((chunk, LANES), lambda bi, c: (bi * nc + c, 0))
    stt = pl.BlockSpec((1, GD_H, GD_DK, GD_DV), lambda bi, c: (bi, 0, 0, 0))
    y, st = pl.pallas_call(
        _gdn_core_kernel,
        out_shape=(jax.ShapeDtypeStruct((b * tp, kd), BF16),
                   jax.ShapeDtypeStruct((b, GD_H, GD_DK, GD_DV), F32)),
        grid=(b, nc),
        in_specs=[col(0), col(1), col(2), col(0), lan, lan, stt, _full((1, GD_DV))],
        out_specs=(col(0), stt),
        scratch_shapes=[pltpu.VMEM((GD_H, GD_DK, GD_DV), F32)],
        compiler_params=_cparams(("arbitrary", "arbitrary")),
        name="gdn_core",
    )(qkv_p, qkv_p, qkv_p, z_p, beta_p, g_p, ssm0, norm_w.reshape(1, GD_DV))
    x_new = _outproj_call(grp, x, _unpad_time(y, b, t, tp), bf(w_o), gate)
    nl = min(t, GD_CONV - 1)
    x_last = x.reshape(b, t, D)[:, t - nl:].reshape(b * nl, D)
    rep = lambda m: jnp.repeat(m, nl, axis=0)
    pre_last = _modrows_call(x_last, norm_g, rep(shift), rep(scale), w_qkv).reshape(b, nl, GD_C)
    if nl < GD_CONV - 1:
        pre_last = jnp.concatenate([conv0[:, nl:], pre_last], axis=1)
    return x_new, pre_last, st


def _ret_proj_kernel(x_ref, g_ref, sh_ref, sc_ref, cos_ref, sin_ref, w_ref, q_ref, k_ref, v_ref, gate_ref):
    h = _modulate(x_ref[...], g_ref[...], sh_ref[0], sc_ref[0]).astype(BF16)
    kd = RT_H * RT_DK
    vd = RT_H * RT_DV
    cos, sin = cos_ref[0], sin_ref[0]
    even = (lax.broadcasted_iota(I32, (1, kd), 1) & 1) == 0

    def rotary(z):
        swapped = jnp.where(even, pltpu.roll(z, kd - 1, 1), pltpu.roll(z, 1, 1))
        return z * cos + swapped * sin

    q_ref[...] = rotary(jnp.dot(h, w_ref[:, 0:kd], preferred_element_type=F32)).astype(BF16)
    k = rotary(jnp.dot(h, w_ref[:, kd:2 * kd], preferred_element_type=F32))
    k_ref[...] = (k * (RT_DK ** -0.5)).astype(BF16)
    v_ref[...] = jnp.dot(h, w_ref[:, 2 * kd:2 * kd + vd], preferred_element_type=F32).astype(BF16)
    gate_ref[...] = jnp.dot(h, w_ref[:, 2 * kd + vd:], preferred_element_type=F32).astype(BF16)


def _ret_core_kernel(q_ref, k_ref, v_ref, gate_ref, dm_ref, qd_ref, kd_ref, cd_ref, s0_ref, nw_ref,
                     y_ref, st_ref, state):
    cidx = pl.program_id(1)

    @pl.when(cidx == 0)
    def _():
        state[...] = s0_ref[0]

    for h in range(RT_H):
        ks = slice(h * RT_DK, (h + 1) * RT_DK)
        vs = slice(h * RT_DV, (h + 1) * RT_DV)
        qh, kh, vh = q_ref[:, ks], k_ref[:, ks], v_ref[:, vs]
        s = state[h]
        inner = _bdot_nt(qh, kh) * dm_ref[h]
        o = _bdot(inner, vh) + _bdot(qh, s) * qd_ref[h]
        state[h] = s * cd_ref[h] + _bdot_tn(kh.astype(F32) * kd_ref[h], vh)
        on = o * lax.rsqrt(jnp.mean(o * o, axis=-1, keepdims=True) + NORM_EPS) * nw_ref[:, vs]
        y_ref[:, vs] = (on * _silu(gate_ref[:, vs].astype(F32))).astype(BF16)

    @pl.when(cidx == pl.num_programs(1) - 1)
    def _():
        st_ref[0] = state[...]


def _ret_layer(grp, x, norm_g, mods, s0, pos0, w_in, norm_w, w_o, chunk):
    b, t = grp.b, grp.t
    shift, scale, gate = mods
    sh, sh_spec = grp.rowmod(shift)
    sc, sc_spec = grp.rowmod(scale)
    kd, vd = RT_H * RT_DK, RT_H * RT_DV
    half = RT_DK // 2
    inv = 1.0 / (10000.0 ** jnp.linspace(0.0, 1.0, half, dtype=F32))
    pos = jnp.arange(t, dtype=F32) + float(pos0)
    ang = pos[:, None] * inv[None, :]
    cos = jnp.repeat(jnp.cos(ang), 2, axis=1)
    sin = jnp.stack([-jnp.sin(ang), jnp.sin(ang)], axis=-1).reshape(t, RT_DK)
    cos4, cos_spec = grp.postab(jnp.tile(cos, (1, RT_H)))
    sin4, sin_spec = grp.postab(jnp.tile(sin, (1, RT_H)))
    wb = w_in.astype(BF16)
    q, k, v, gt = pl.pallas_call(
        _ret_proj_kernel,
        out_shape=(jax.ShapeDtypeStruct((grp.n, kd), BF16), jax.ShapeDtypeStruct((grp.n, kd), BF16),
                   jax.ShapeDtypeStruct((grp.n, vd), BF16), jax.ShapeDtypeStruct((grp.n, vd), BF16)),
        grid=(grp.tiles,),
        in_specs=[grp.rows(D), _full((1, D)), sh_spec, sc_spec, cos_spec, sin_spec, _full(wb.shape)],
        out_specs=(grp.rows(kd), grp.rows(kd), grp.rows(vd), grp.rows(vd)),
        compiler_params=_cparams(("arbitrary",)),
        name="ret_proj",
    )(x, norm_g.reshape(1, D), sh, sc, cos4, sin4, wb)
    tp = ((t + chunk - 1) // chunk) * chunk
    nc = tp // chunk
    nv = min(t, chunk)
    assert tp == t or nc == 1
    log_gamma = jnp.log1p(-jnp.exp2(-5.0 - jnp.arange(RT_H, dtype=F32)))
    idx = jnp.arange(chunk, dtype=F32)
    diff = idx[:, None] - idx[None, :]
    dmask = jnp.where(diff >= 0, jnp.exp(log_gamma[:, None, None] * jnp.maximum(diff, 0.0)), 0.0)
    q_dec = jnp.exp(log_gamma[:, None] * (idx + 1.0))[:, :, None]
    k_dec = jnp.exp(log_gamma[:, None] * jnp.maximum(nv - 1.0 - idx, 0.0))[:, :, None]
    c_dec = jnp.exp(log_gamma * nv)[:, None, None]
    if s0 is None:
        s0 = jnp.zeros((b, RT_H, RT_DK, RT_DV), F32)
    rowk = pl.BlockSpec((chunk, kd), lambda bi, c: (bi * nc + c, 0))
    rowv = pl.BlockSpec((chunk, vd), lambda bi, c: (bi * nc + c, 0))
    stt = pl.BlockSpec((1, RT_H, RT_DK, RT_DV), lambda bi, c: (bi, 0, 0, 0))
    y, st = pl.pallas_call(
        _ret_core_kernel,
        out_shape=(jax.ShapeDtypeStruct((b * tp, vd), BF16),
                   jax.ShapeDtypeStruct((b, RT_H, RT_DK, RT_DV), F32)),
        grid=(b, nc),
        in_specs=[rowk, rowk, rowv, rowv, _full(dmask.shape), _full(q_dec.shape), _full(k_dec.shape),
                  _full(c_dec.shape), stt, _full((1, vd))],
        out_specs=(rowv, stt),
        scratch_shapes=[pltpu.VMEM((RT_H, RT_DK, RT_DV), F32)],
        compiler_params=_cparams(("arbitrary", "arbitrary")),
        name="ret_core",
    )(_pad_time(q, b, t, tp), _pad_time(k, b, t, tp), _pad_time(v, b, t, tp), _pad_time(gt, b, t, tp),
      dmask, q_dec, k_dec, c_dec, s0, norm_w.reshape(1, vd))
    x_new = _outproj_call(grp, x, _unpad_time(y, b, t, tp), w_o.astype(BF16), gate)
    return x_new, st


def _hgrn_proj_kernel(layer, x_ref, g_ref, sh_ref, sc_ref, lbl_ref, w_ref,
                      q_ref, k_ref, lf_ref, v_ref, gate_ref):
    h = _modulate(x_ref[...], g_ref[...], sh_ref[0], sc_ref[0]).astype(BF16)
    ed = HG_H * HG_E
    logits = lbl_ref[...]
    e = jnp.exp(logits - jnp.max(logits, axis=0, keepdims=True))
    lrow = lax.broadcasted_iota(I32, logits.shape, 0)
    part = jnp.where(jnp.logical_and(lrow >= 1, lrow <= layer), e, 0.0)
    lb = jnp.sum(part, axis=0, keepdims=True) / jnp.sum(e, axis=0, keepdims=True)
    q_ref[...] = jnp.dot(h, w_ref[:, 0:ed], preferred_element_type=F32)
    f = lb + (1.0 - lb) * _sigmoid(jnp.dot(h, w_ref[:, ed:2 * ed], preferred_element_type=F32))
    k_ref[...] = 1.0 - f
    lf_ref[...] = jnp.log(f)
    v_ref[...] = jnp.dot(h, w_ref[:, 2 * ed:3 * ed], preferred_element_type=F32).astype(BF16)
    gate_ref[...] = jnp.dot(h, w_ref[:, 3 * ed:], preferred_element_type=F32).astype(BF16)


def _hgrn_core_kernel(q_ref, k_ref, lf_ref, v_ref, gate_ref, s0_ref, nw_ref, y_ref, st_ref, state):
    cidx = pl.program_id(1)
    c = q_ref.shape[0]

    @pl.when(cidx == 0)
    def _():
        state[...] = s0_ref[0]

    ri = lax.broadcasted_iota(I32, (c, c), 0)
    ci = lax.broadcasted_iota(I32, (c, c), 1)
    ltri = (ri >= ci).astype(F32)
    rowi = lax.broadcasted_iota(I32, (c, 1), 0)
    ones_cv = jnp.ones((c, HG_DV), F32)
    for h in range(HG_H):
        sl = slice(h * HG_E, (h + 1) * HG_E)
        qh, kh = q_ref[:, sl], k_ref[:, sl]
        vh = v_ref[:, sl].astype(F32)
        cum = _fdot(ltri, lf_ref[:, sl])
        st = state[h]
        o = _bdot(qh * jnp.exp(cum), st)
        for j in range(c):
            cj = cum[j:j + 1, :]
            causal = rowi >= j
            dec = jnp.exp(jnp.where(causal, cum - cj, 0.0))
            col = jnp.sum(qh * kh[j:j + 1, :] * dec, axis=-1, keepdims=True)
            col = jnp.where(causal, col, 0.0)
            o = o + col * vh[j:j + 1, :]
        last = cum[c - 1:c, :]
        last_rows = lax.dot_general(lf_ref[:, sl], ones_cv, (((0,), (0,)), ((), ())),
                                    precision=HIGHEST, preferred_element_type=F32)
        state[h] = st * jnp.exp(last_rows) + _bdot_tn(kh * jnp.exp(last - cum), vh)
        on = o * lax.rsqrt(jnp.mean(o * o, axis=-1, keepdims=True) + NORM_EPS) * nw_ref[:, sl]
        y_ref[:, sl] = (on * _silu(gate_ref[:, sl].astype(F32))).astype(BF16)

    @pl.when(cidx == pl.num_programs(1) - 1)
    def _():
        st_ref[0] = state[...]


def _hgrn_layer(grp, x, norm_g, mods, s0, layer, lb_logits, w_in, norm_w, w_o, chunk):
    b, t = grp.b, grp.t
    shift, scale, gate = mods
    sh, sh_spec = grp.rowmod(shift)
    sc, sc_spec = grp.rowmod(scale)
    ed, vd = HG_H * HG_E, HG_H * HG_DV
    wb = w_in.astype(BF16)
    q, k, lf, v, gt = pl.pallas_call(
        functools.partial(_hgrn_proj_kernel, layer),
        out_shape=(jax.ShapeDtypeStruct((grp.n, ed), F32), jax.ShapeDtypeStruct((grp.n, ed), F32),
                   jax.ShapeDtypeStruct((grp.n, ed), F32), jax.ShapeDtypeStruct((grp.n, vd), BF16),
                   jax.ShapeDtypeStruct((grp.n, vd), BF16)),
        grid=(grp.tiles,),
        in_specs=[grp.rows(D), _full((1, D)), sh_spec, sc_spec, _full(lb_logits.shape), _full(wb.shape)],
        out_specs=(grp.rows(ed), grp.rows(ed), grp.rows(ed), grp.rows(vd), grp.rows(vd)),
        compiler_params=_cparams(("arbitrary",)),
        name="hgrn_proj",
    )(x, norm_g.reshape(1, D), sh, sc, lb_logits, wb)
    tp = ((t + chunk - 1) // chunk) * chunk
    nc = tp // chunk
    if s0 is None:
        s0 = jnp.zeros((b, HG_H, HG_E, HG_DV), F32)
    row = pl.BlockSpec((chunk, ed), lambda bi, c: (bi * nc + c, 0))
    stt = pl.BlockSpec((1, HG_H, HG_E, HG_DV), lambda bi, c: (bi, 0, 0, 0))
    y, st = pl.pallas_call(
        _hgrn_core_kernel,
        out_shape=(jax.ShapeDtypeStruct((b * tp, vd), BF16),
                   jax.ShapeDtypeStruct((b, HG_H, HG_E, HG_DV), F32)),
        grid=(b, nc),
        in_specs=[row, row, row, row, row, stt, _full((1, vd))],
        out_specs=(row, stt),
        scratch_shapes=[pltpu.VMEM((HG_H, HG_E, HG_DV), F32)],
        compiler_params=_cparams(("arbitrary", "arbitrary")),
        name="hgrn_core",
    )(_pad_time(q, b, t, tp), _pad_time(k, b, t, tp), _pad_time(lf, b, t, tp), _pad_time(v, b, t, tp),
      _pad_time(gt, b, t, tp), s0, norm_w.reshape(1, vd))
    x_new = _outproj_call(grp, x, _unpad_time(y, b, t, tp), w_o.astype(BF16), gate)
    return x_new, st


ROW_TILE = 256
MOE_TILE = 512
GDN_CHUNK, RET_CHUNK, HGRN_CHUNK = 64, 128, 16
SAMPLE_CHUNK = 16
PAST_LEN = 16384


def kernel(x_prompt, x_sample, c_prompt, c_sample, state_rwkv_wkv, state_rwkv_shift, state_gdn_ssm, state_gdn_conv, state_ret, state_hgrn, ada_w, ada_b, norm_mix, norm_ffn, norm_final, rwkv_mu, rwkv_w_rkv, rwkv_w0, rwkv_w1, rwkv_w2, rwkv_a0, rwkv_a1, rwkv_a2, rwkv_g1, rwkv_g2, rwkv_k_k, rwkv_k_a, rwkv_r_k, rwkv_ln_w, rwkv_ln_b, rwkv_w_o, gdn_w_in, gdn_conv_w, gdn_a_log, gdn_dt_bias, gdn_norm_w, gdn_w_o, ret_w_in, ret_norm_w, ret_w_o, hgrn_w_in, hgrn_lb_logits, hgrn_norm_w, hgrn_w_o, moe_w_router, moe_b_router, moe_w_gu, moe_b_gu, moe_w_down, moe_b_down):
    bp, tp, _ = x_prompt.shape
    bs, ts, _ = x_sample.shape
    gp, gs = _Group(bp, tp, ROW_TILE), _Group(bs, ts, ROW_TILE)
    mp, msg = _Group(bp, tp, MOE_TILE), _Group(bs, ts, MOE_TILE)
    ada = _ada_call(jnp.concatenate([c_prompt, c_sample], axis=0), ada_w, ada_b)
    xp = x_prompt.reshape(bp * tp, D)
    xs = x_sample.reshape(bs * ts, D)
    outs_p = {k: [] for k in ("wkv", "shift", "ssm", "conv", "ret", "hgrn")}
    outs_s = {k: [] for k in ("wkv", "shift", "ssm", "conv", "ret", "hgrn")}
    for i in range(DEPTH):
        kind, j = i % 4, i // 4
        m = ada[i].reshape(bp + bs, 6, D)
        mod_p = [m[:bp, n] for n in range(6)]
        mod_s = [m[bp:, n] for n in range(6)]
        g = norm_mix[i]
        if kind == 0:
            prm = (rwkv_mu[j], rwkv_w_rkv[j], rwkv_w0[j], rwkv_w1[j], rwkv_w2[j], rwkv_a0[j], rwkv_a1[j],
                   rwkv_a2[j], rwkv_g1[j], rwkv_g2[j], rwkv_k_k[j], rwkv_k_a[j], rwkv_r_k[j],
                   rwkv_ln_w[j], rwkv_ln_b[j], rwkv_w_o[j])
            xp, sh_p, wkv_p = _rwkv_layer(gp, xp, g, mod_p[:3], None, None, *prm)
            xs, sh_s, wkv_s = _rwkv_layer(gs, xs, g, mod_s[:3], state_rwkv_shift[j], state_rwkv_wkv[j], *prm)
            outs_p["wkv"].append(wkv_p); outs_p["shift"].append(sh_p)
            outs_s["wkv"].append(wkv_s); outs_s["shift"].append(sh_s)
        elif kind == 1:
            prm = (gdn_w_in[j], gdn_conv_w[j], gdn_a_log[j], gdn_dt_bias[j], gdn_norm_w[j], gdn_w_o[j])
            xp, cv_p, ss_p = _gdn_layer(gp, xp, g, mod_p[:3], None, None, *prm, GDN_CHUNK)
            xs, cv_s, ss_s = _gdn_layer(gs, xs, g, mod_s[:3], state_gdn_conv[j], state_gdn_ssm[j], *prm,
                                        SAMPLE_CHUNK)
            outs_p["ssm"].append(ss_p); outs_p["conv"].append(cv_p)
            outs_s["ssm"].append(ss_s); outs_s["conv"].append(cv_s)
        elif kind == 2:
            prm = (ret_w_in[j], ret_norm_w[j], ret_w_o[j])
            xp, r_p = _ret_layer(gp, xp, g, mod_p[:3], None, 0, *prm, RET_CHUNK)
            xs, r_s = _ret_layer(gs, xs, g, mod_s[:3], state_ret[j], PAST_LEN, *prm, SAMPLE_CHUNK)
            outs_p["ret"].append(r_p); outs_s["ret"].append(r_s)
        else:
            prm = (i, hgrn_lb_logits, hgrn_w_in[j], hgrn_norm_w[j], hgrn_w_o[j])
            xp, h_p = _hgrn_layer(gp, xp, g, mod_p[:3], None, *prm, HGRN_CHUNK)
            xs, h_s = _hgrn_layer(gs, xs, g, mod_s[:3], state_hgrn[j], *prm, SAMPLE_CHUNK)
            outs_p["hgrn"].append(h_p); outs_s["hgrn"].append(h_s)
        xp, xs = _moe(i, [mp, msg], [xp, xs], norm_ffn[i], [mod_p[3:], mod_s[3:]], moe_w_router[i],
                      moe_b_router[i], moe_w_gu, moe_b_gu, moe_w_down, moe_b_down)
    y_prompt = _final_call(gp, xp, norm_final).reshape(bp, tp, D)
    y_sample = _final_call(gs, xs, norm_final).reshape(bs, ts, D)
    order = ("wkv", "shift", "ssm", "conv", "ret", "hgrn")
    return ((y_prompt, y_sample) + tuple(jnp.stack(outs_p[k]) for k in order)
            + tuple(jnp.stack(outs_s[k]) for k in order))
```

```python
import functools
import math

import jax
import jax.numpy as jnp
from jax import lax
from jax.experimental import pallas as pl
from jax.experimental.pallas import tpu as pltpu

F32 = jnp.float32
BF16 = jnp.bfloat16
I32 = jnp.int32
HIGHEST = lax.Precision.HIGHEST

D = 1024
DEPTH = 4
NORM_EPS = 1e-6
RW_H, RW_N = 16, 64
RW_GN_EPS = 64e-5
GD_H, GD_DK, GD_DV, GD_CONV = 8, 128, 128, 4
GD_C = 3 * GD_H * GD_DK
RT_H, RT_DK, RT_DV = 4, 256, 512
HG_H, HG_E, HG_DV = 8, 128, 128
N_EXPERTS, TOP_K, D_FF = 32, 4, 1024
SWIGLU_LIMIT, SWIGLU_ALPHA = 7.0, 1.702

LANES = 128
EXPERT_TILE = 512
TOKEN_BLOCK = 256
VMEM_LIMIT = 56 * 1024 * 1024


def _cparams(sem, vmem=VMEM_LIMIT):
    return pltpu.CompilerParams(dimension_semantics=sem, vmem_limit_bytes=vmem)


def _sigmoid(x):
    return 1.0 / (1.0 + jnp.exp(-x))


def _silu(x):
    return x * _sigmoid(x)


def _softplus(x):
    return jnp.maximum(x, 0.0) + jnp.log(1.0 + jnp.exp(-jnp.abs(x)))


def _modulate(x, g, shift, scale):
    ms = jnp.mean(x * x, axis=-1, keepdims=True)
    return (x * lax.rsqrt(ms + NORM_EPS) * g) * (1.0 + scale) + shift


def _bdot(a, b):
    return jnp.dot(a.astype(BF16), b.astype(BF16), preferred_element_type=F32)


def _bdot_nt(a, b):
    return lax.dot_general(a.astype(BF16), b.astype(BF16), (((1,), (1,)), ((), ())),
                           preferred_element_type=F32)


def _bdot_tn(a, b):
    return lax.dot_general(a.astype(BF16), b.astype(BF16), (((0,), (0,)), ((), ())),
                           preferred_element_type=F32)


def _fdot(a, b):
    return jnp.dot(a, b, precision=HIGHEST, preferred_element_type=F32)


class _Group:
    def __init__(self, b, t, tm):
        self.b, self.t, self.n = b, t, b * t
        self.tm = min(tm, self.n)
        assert self.n % self.tm == 0
        assert (self.t % self.tm == 0) or (self.tm % self.t == 0)
        self.per_batch = self.t % self.tm == 0
        self.tiles = self.n // self.tm

    def rows(self, width):
        return pl.BlockSpec((self.tm, width), lambda i: (i, 0))

    def rowmod(self, arr):
        w = arr.shape[-1]
        if self.per_batch:
            k = self.t // self.tm
            return arr.reshape(self.b, 1, w), pl.BlockSpec((1, 1, w), lambda i: (i // k, 0, 0))
        rep = jnp.repeat(arr, self.t, axis=0).reshape(self.tiles, self.tm, w)
        return rep, pl.BlockSpec((1, self.tm, w), lambda i: (i, 0, 0))

    def rowseq(self, arr):
        w = arr.shape[-1]
        if self.per_batch:
            assert arr.shape[1] == 1
            k = self.t // self.tm
            return arr, pl.BlockSpec((1, 1, w), lambda i: (i // k, 0, 0))
        return arr.reshape(self.tiles, self.tm, w), pl.BlockSpec((1, self.tm, w), lambda i: (i, 0, 0))

    def postab(self, tab):
        w = tab.shape[-1]
        if self.per_batch:
            k = self.t // self.tm
            return tab.reshape(k, self.tm, w), pl.BlockSpec((1, self.tm, w), lambda i: (i % k, 0, 0))
        rep = jnp.tile(tab, (self.tm // self.t, 1)).reshape(1, self.tm, w)
        return rep, pl.BlockSpec((1, self.tm, w), lambda i: (0, 0, 0))

    def prev8(self, width):
        k = self.tm // 8
        return pl.BlockSpec((8, width), lambda i: (jnp.maximum(i * k - 1, 0), 0))


def _full(shape):
    nd = len(shape)
    return pl.BlockSpec(shape, lambda *a: (0,) * nd)


def _tpos(tm, t):
    row = pl.program_id(0) * tm + lax.broadcasted_iota(I32, (tm, 1), 0)
    return row % t


def _shift_rows(cur, prev8, d):
    rolled = pltpu.roll(cur, d, 0)
    head = jnp.where(lax.broadcasted_iota(I32, (8, 1), 0) < d, pltpu.roll(prev8, d, 0), rolled[0:8])
    if cur.shape[0] == 8:
        return head
    return jnp.concatenate([head, rolled[8:]], axis=0)


def _ada_kernel(c_ref, w_ref, b_ref, o_ref):
    o_ref[0] = _bdot(_silu(c_ref[...]), w_ref[0]) + b_ref[0]


def _ada_call(c_all, ada_w, ada_b):
    nb = c_all.shape[0]
    tn = 1536
    return pl.pallas_call(
        _ada_kernel,
        out_shape=jax.ShapeDtypeStruct((DEPTH, nb, 6 * D), F32),
        grid=(DEPTH, 6 * D // tn),
        in_specs=[pl.BlockSpec((nb, D), lambda l, j: (0, 0)),
                  pl.BlockSpec((1, D, tn), lambda l, j: (l, 0, j)),
                  pl.BlockSpec((1, 1, tn), lambda l, j: (l, 0, j))],
        out_specs=pl.BlockSpec((1, nb, tn), lambda l, j: (l, 0, j)),
        compiler_params=_cparams(("arbitrary", "arbitrary")),
        name="adaln",
    )(c_all, ada_w, ada_b.reshape(DEPTH, 1, 6 * D))


def _modrows_kernel(x_ref, g_ref, sh_ref, sc_ref, *rest):
    h = _modulate(x_ref[...], g_ref[...], sh_ref[...], sc_ref[...])
    if len(rest) == 2:
        w_ref, o_ref = rest
        o_ref[...] = _bdot(h, w_ref[...])
    else:
        rest[0][...] = h


def _modrows_call(x, g, shift, scale, w=None):
    n = x.shape[0]
    args = [x, g.reshape(1, D), shift, scale]
    specs = [_full((n, D)), _full((1, D)), _full((n, D)), _full((n, D))]
    width = D
    if w is not None:
        args.append(w)
        specs.append(_full(w.shape))
        width = w.shape[1]
    return pl.pallas_call(
        _modrows_kernel,
        out_shape=jax.ShapeDtypeStruct((n, width), F32),
        grid=(1,),
        in_specs=specs,
        out_specs=_full((n, width)),
        compiler_params=_cparams(("arbitrary",)),
        name="modrows",
    )(*args)


def _outproj_kernel(has_mul, x_ref, y_ref, *rest):
    if has_mul:
        m_ref, w_ref, gt_ref, o_ref = rest
        y = y_ref[...].astype(F32) * m_ref[...].astype(F32)
    else:
        w_ref, gt_ref, o_ref = rest
        y = y_ref[...]
    o_ref[...] = x_ref[...] + gt_ref[0] * _bdot(y, w_ref[...])


def _outproj_call(grp, x, y, w_o, gate, mul=None):
    dy = y.shape[1]
    gt, gt_spec = grp.rowmod(gate)
    args = [x, y]
    specs = [grp.rows(D), grp.rows(dy)]
    if mul is not None:
        args.append(mul)
        specs.append(grp.rows(dy))
    args += [w_o, gt]
    specs += [_full(w_o.shape), gt_spec]
    return pl.pallas_call(
        functools.partial(_outproj_kernel, mul is not None),
        out_shape=jax.ShapeDtypeStruct((grp.n, D), F32),
        grid=(grp.tiles,),
        in_specs=specs,
        out_specs=grp.rows(D),
        compiler_params=_cparams(("arbitrary",)),
        name="outproj",
    )(*args)


def _final_kernel(x_ref, g_ref, o_ref):
    x = x_ref[...]
    ms = jnp.mean(x * x, axis=-1, keepdims=True)
    o_ref[...] = x * lax.rsqrt(ms + NORM_EPS) * g_ref[...]


def _final_call(grp, x, g):
    return pl.pallas_call(
        _final_kernel,
        out_shape=jax.ShapeDtypeStruct((grp.n, D), F32),
        grid=(grp.tiles,),
        in_specs=[grp.rows(D), _full((1, D))],
        out_specs=grp.rows(D),
        compiler_params=_cparams(("arbitrary",)),
        name="final_norm",
    )(x, g.reshape(1, D))


def _router_kernel(cin_ref, x_ref, g_ref, sh_ref, sc_ref, wr_ref, br_ref,
                   h_ref, idx_ref, gate_ref, rank_ref, cnt_ref, carry):
    i = pl.program_id(0)

    @pl.when(i == 0)
    def _():
        carry[...] = cin_ref[...]

    tm = x_ref.shape[0]
    h = _modulate(x_ref[...], g_ref[...], sh_ref[0], sc_ref[0])
    h_ref[...] = h
    logits = _fdot(h, wr_ref[...]) + br_ref[...]
    lane = lax.broadcasted_iota(I32, logits.shape, 1)
    work = logits
    sel = jnp.zeros(logits.shape, jnp.bool_)
    picks, vals = [], []
    for _ in range(TOP_K):
        m = jnp.max(work, axis=-1, keepdims=True)
        idx = jnp.min(jnp.where(work == m, lane, N_EXPERTS), axis=-1, keepdims=True)
        pick = lane == idx
        picks.append((idx, pick))
        vals.append(m)
        sel = jnp.logical_or(sel, pick)
        work = jnp.where(pick, -jnp.inf, work)
    es = [jnp.exp(v - vals[0]) for v in vals]
    denom = es[0] + es[1] + es[2] + es[3]
    self_f = sel.astype(F32)
    tri = (lax.broadcasted_iota(I32, (tm, tm), 0) > lax.broadcasted_iota(I32, (tm, tm), 1))
    local = jnp.dot(tri.astype(BF16), self_f.astype(BF16), preferred_element_type=F32)
    rank = local + carry[...]
    carry[...] = carry[...] + jnp.sum(self_f, axis=0, keepdims=True)
    cnt_ref[...] = carry[...]
    lane_o = lax.broadcasted_iota(I32, (tm, LANES), 1)
    idx_o = jnp.zeros((tm, LANES), I32)
    gate_o = jnp.zeros((tm, LANES), F32)
    rank_o = jnp.zeros((tm, LANES), I32)
    for k in range(TOP_K):
        idx, pick = picks[k]
        rk = jnp.sum(jnp.where(pick, rank, 0.0), axis=-1, keepdims=True)
        idx_o = jnp.where(lane_o == k, idx, idx_o)
        gate_o = jnp.where(lane_o == k, es[k] / denom, gate_o)
        rank_o = jnp.where(lane_o == k, rk.astype(I32), rank_o)
    idx_ref[...] = idx_o
    gate_ref[...] = gate_o
    rank_ref[...] = rank_o


def _router_call(grp, counts_in, x, g, shift, scale, w_router, b_router):
    sh, sh_spec = grp.rowmod(shift)
    sc, sc_spec = grp.rowmod(scale)
    pad = pl.BlockSpec((grp.tm, LANES), lambda i: (i, 0))
    return pl.pallas_call(
        _router_kernel,
        out_shape=(jax.ShapeDtypeStruct((grp.n, D), F32),
                   jax.ShapeDtypeStruct((grp.n, LANES), I32),
                   jax.ShapeDtypeStruct((grp.n, LANES), F32),
                   jax.ShapeDtypeStruct((grp.n, LANES), I32),
                   jax.ShapeDtypeStruct((1, N_EXPERTS), F32)),
        grid=(grp.tiles,),
        in_specs=[_full((1, N_EXPERTS)), grp.rows(D), _full((1, D)), sh_spec, sc_spec,
                  _full((D, N_EXPERTS)), _full((1, N_EXPERTS))],
        out_specs=(grp.rows(D), pad, pad, pad, _full((1, N_EXPERTS))),
        scratch_shapes=[pltpu.VMEM((1, N_EXPERTS), F32)],
        compiler_params=_cparams(("arbitrary",)),
        name="moe_router",
    )(counts_in, x, g.reshape(1, D), sh, sc, w_router, b_router.reshape(1, N_EXPERTS))


def _dispatch_kernel(cnt_ref, off_ref, nv_ref, pos_ref, h_ref, xs_ref, zbuf, sem, sem_z):
    i = pl.program_id(0)

    def issue(t, c):
        for k in range(TOP_K):
            pltpu.make_async_copy(h_ref.at[pl.ds(t, 1)], xs_ref.at[pl.ds(pos_ref[t * TOP_K + k], 1)],
                                  sem).start()
        return c

    lax.fori_loop(0, TOKEN_BLOCK, issue, 0, unroll=8)
    for k in range(TOP_K):
        pltpu.make_async_copy(h_ref, xs_ref.at[pl.ds(0, TOKEN_BLOCK)], sem).wait()

    @pl.when(i == pl.num_programs(0) - 1)
    def _():
        zbuf[...] = jnp.zeros(zbuf.shape, F32)
        bits = [1 << s for s in range(EXPERT_TILE.bit_length() - 2, 2, -1)]

        def pad_copies(e, wait):
            n = cnt_ref[e]
            start = off_ref[e] + n
            end = off_ref[e] + ((n + EXPERT_TILE - 1) // EXPERT_TILE) * EXPERT_TILE
            head = (-start) & 7

            def one(r, c):
                cp = pltpu.make_async_copy(zbuf.at[pl.ds(0, 1)], xs_ref.at[pl.ds(start + r, 1)], sem_z)
                if wait:
                    cp.wait()
                else:
                    cp.start()
                return c

            lax.fori_loop(0, head, one, 0)
            start8 = start + head
            rem = end - start8
            for bit in bits:
                @pl.when((rem & bit) != 0)
                def _():
                    s = pl.multiple_of(start8 + (rem & ~(2 * bit - 1)), 8)
                    cp = pltpu.make_async_copy(zbuf.at[pl.ds(0, bit)], xs_ref.at[pl.ds(s, bit)], sem_z)
                    if wait:
                        cp.wait()
                    else:
                        cp.start()

        def tail_copy(j):
            return pltpu.make_async_copy(zbuf, xs_ref.at[pl.ds(j * EXPERT_TILE, EXPERT_TILE)], sem_z)

        n_tiles = xs_ref.shape[0] // EXPERT_TILE
        for wait in (False, True):
            def per_expert(e, c):
                pad_copies(e, wait)
                return c

            def per_tail(j, c):
                if wait:
                    tail_copy(j).wait()
                else:
                    tail_copy(j).start()
                return c

            lax.fori_loop(0, N_EXPERTS, per_expert, 0)
            lax.fori_loop(nv_ref[0], n_tiles, per_tail, 0)


def _dispatch_call(counts, offsets, n_valid, pos_flat, h, n_rows):
    n = h.shape[0]
    return pl.pallas_call(
        _dispatch_kernel,
        out_shape=jax.ShapeDtypeStruct((n_rows, D), F32),
        grid_spec=pltpu.PrefetchScalarGridSpec(
            num_scalar_prefetch=3,
            grid=(n // TOKEN_BLOCK,),
            in_specs=[pl.BlockSpec((TOKEN_BLOCK * TOP_K,), lambda i, c, o, v: (i,), memory_space=pltpu.SMEM),
                      pl.BlockSpec((TOKEN_BLOCK, D), lambda i, c, o, v: (i, 0))],
            out_specs=pl.BlockSpec(memory_space=pl.ANY),
            scratch_shapes=[pltpu.VMEM((EXPERT_TILE, D), F32), pltpu.SemaphoreType.DMA(()),
                            pltpu.SemaphoreType.DMA(())]),
        compiler_params=_cparams(("arbitrary",)),
        name="moe_dispatch",
    )(counts, offsets, n_valid, pos_flat, h)


def _expert_kernel(te_ref, nv_ref, x_ref, wgu_ref, bgu_ref, wd_ref, bd_ref, o_ref, wgu_s, wd_s):
    j = pl.program_id(0)
    fresh = jnp.logical_or(j == 0, te_ref[j] != te_ref[jnp.maximum(j - 1, 0)])

    @pl.when(jnp.logical_and(j < nv_ref[0], fresh))
    def _():
        wgu_s[...] = wgu_ref[0, 0].astype(BF16)
        wd_s[...] = wd_ref[0, 0].astype(BF16)

    @pl.when(j < nv_ref[0])
    def _():
        x = x_ref[...].astype(BF16)
        gu = jnp.dot(x, wgu_s[...], preferred_element_type=F32) + bgu_ref[0, 0]
        gl = jnp.minimum(gu[:, :D_FF], SWIGLU_LIMIT)
        up = jnp.clip(gu[:, D_FF:], -SWIGLU_LIMIT, SWIGLU_LIMIT)
        act = (up + 1.0) * gl * _sigmoid(SWIGLU_ALPHA * gl)
        o_ref[...] = jnp.dot(act.astype(BF16), wd_s[...], preferred_element_type=F32) + bd_ref[0, 0]

    @pl.when(j >= nv_ref[0])
    def _():
        o_ref[...] = jnp.zeros(o_ref.shape, F32)


def _expert_call(layer, tile_expert, n_valid, xs, w_gu, b_gu, w_down, b_down):
    n_rows = xs.shape[0]
    g = n_rows // EXPERT_TILE
    return pl.pallas_call(
        _expert_kernel,
        out_shape=jax.ShapeDtypeStruct((n_rows, D), F32),
        grid_spec=pltpu.PrefetchScalarGridSpec(
            num_scalar_prefetch=2,
            grid=(g,),
            in_specs=[pl.BlockSpec((EXPERT_TILE, D), lambda j, te, nv: (jnp.minimum(j, nv[0] - 1), 0)),
                      pl.BlockSpec((1, 1, D, 2 * D_FF), lambda j, te, nv: (layer, te[j], 0, 0)),
                      pl.BlockSpec((1, 1, 1, 2 * D_FF), lambda j, te, nv: (layer, te[j], 0, 0)),
                      pl.BlockSpec((1, 1, D_FF, D), lambda j, te, nv: (layer, te[j], 0, 0)),
                      pl.BlockSpec((1, 1, 1, D), lambda j, te, nv: (layer, te[j], 0, 0))],
            out_specs=pl.BlockSpec((EXPERT_TILE, D), lambda j, te, nv: (j, 0)),
            scratch_shapes=[pltpu.VMEM((D, 2 * D_FF), BF16), pltpu.VMEM((D_FF, D), BF16)]),
        compiler_params=_cparams(("arbitrary",)),
        name="moe_experts",
    )(tile_expert, n_valid, xs, w_gu, b_gu.reshape(-1, N_EXPERTS, 1, 2 * D_FF), w_down,
      b_down.reshape(-1, N_EXPERTS, 1, D))


def _combine_kernel(pos_ref, x_ref, gate_ref, gt_ref, ys_ref, o_ref, buf, sem):
    def issue(t, c):
        for k in range(TOP_K):
            pltpu.make_async_copy(ys_ref.at[pl.ds(pos_ref[t * TOP_K + k], 1)], buf.at[k, pl.ds(t, 1)],
                                  sem).start()
        return c

    lax.fori_loop(0, TOKEN_BLOCK, issue, 0, unroll=8)
    for k in range(TOP_K):
        pltpu.make_async_copy(ys_ref.at[pl.ds(0, TOKEN_BLOCK)], buf.at[k], sem).wait()
    gate = gate_ref[...]
    f = gate[:, 0:1] * buf[0]
    for k in range(1, TOP_K):
        f = f + gate[:, k:k + 1] * buf[k]
    o_ref[...] = x_ref[...] + gt_ref[0] * f


def _combine_call(grp, pos_flat, x, gate_pad, gate2, ys):
    gt, gt_spec = grp.rowmod(gate2)
    assert grp.tm == TOKEN_BLOCK
    return pl.pallas_call(
        _combine_kernel,
        out_shape=jax.ShapeDtypeStruct((grp.n, D), F32),
        grid=(grp.tiles,),
        in_specs=[pl.BlockSpec((TOKEN_BLOCK * TOP_K,), lambda i: (i,), memory_space=pltpu.SMEM),
                  grp.rows(D), grp.rows(LANES), gt_spec, pl.BlockSpec(memory_space=pl.ANY)],
        out_specs=grp.rows(D),
        scratch_shapes=[pltpu.VMEM((TOP_K, TOKEN_BLOCK, D), F32), pltpu.SemaphoreType.DMA(())],
        compiler_params=_cparams(("arbitrary",)),
        name="moe_combine",
    )(pos_flat, x, gate_pad, gt, ys)


def _moe(layer, groups, xs_in, norm_g, mods, w_router, b_router, w_gu, b_gu, w_down, b_down):
    counts = jnp.zeros((1, N_EXPERTS), F32)
    hs, idxs, gates, ranks = [], [], [], []
    for grp, x, (sh2, sc2, _) in zip(groups, xs_in, mods):
        h, idx, gate, rank, counts = _router_call(grp, counts, x, norm_g, sh2, sc2, w_router, b_router)
        hs.append(h)
        idxs.append(idx[:, :TOP_K])
        gates.append(gate)
        ranks.append(rank[:, :TOP_K])
    h_all = jnp.concatenate(hs, axis=0)
    idx_all = jnp.concatenate(idxs, axis=0)
    rank_all = jnp.concatenate(ranks, axis=0)
    n = h_all.shape[0]
    cnt = counts[0].astype(I32)
    padded = ((cnt + EXPERT_TILE - 1) // EXPERT_TILE) * EXPERT_TILE
    ends = jnp.cumsum(padded)
    offsets = ends - padded
    n_tiles = (n * TOP_K + N_EXPERTS * (EXPERT_TILE - 1)) // EXPERT_TILE
    n_rows = n_tiles * EXPERT_TILE
    pos = (jnp.take(offsets, idx_all) + rank_all).astype(I32)
    pos_flat = pos.reshape(n * TOP_K)
    n_valid = (ends[-1] // EXPERT_TILE).astype(I32)
    tile_start = jnp.arange(n_tiles, dtype=I32) * EXPERT_TILE
    tile_start = jnp.minimum(tile_start, ends[-1] - EXPERT_TILE)
    tile_expert = jnp.sum(tile_start[:, None] >= ends[None, :], axis=1).astype(I32)
    n_valid = n_valid.reshape(1)
    xs = _dispatch_call(cnt, offsets.astype(I32), n_valid, pos_flat, h_all, n_rows)
    ys = _expert_call(layer, tile_expert, n_valid, xs, w_gu, b_gu, w_down, b_down)
    outs = []
    start = 0
    for grp, x, gate, (_, _, gt2) in zip(groups, xs_in, gates, mods):
        cgrp = _Group(grp.b, grp.t, TOKEN_BLOCK)
        p = lax.dynamic_slice_in_dim(pos_flat, start * TOP_K, grp.n * TOP_K)
        outs.append(_combine_call(cgrp, p, x, gate, gt2, ys))
        start += grp.n
    return outs


def _rwkv_proj_kernel(t_len, x_ref, xp_ref, g_ref, sh_ref, sc_ref, s0_ref, mu_ref, wrkv_ref, w0_ref,
                      w1_ref, w2_ref, a0_ref, a1_ref, a2_ref, g1_ref, g2_ref,
                      r_ref, w_ref, k_ref, v_ref, a_ref, gg_ref):
    tm = x_ref.shape[0]
    g, sh, sc = g_ref[...], sh_ref[0], sc_ref[0]
    h = _modulate(x_ref[...], g, sh, sc)
    hp = _modulate(xp_ref[...], g, sh[0:8] if sh.shape[0] > 1 else sh, sc[0:8] if sc.shape[0] > 1 else sc)
    prev = jnp.where(_tpos(tm, t_len) == 0, s0_ref[0], _shift_rows(h, hp, 1))
    dx = prev - h
    mu = mu_ref[...]
    xr, xw, xk, xv, xa, xg = [h + dx * mu[n:n + 1] for n in range(6)]
    r_ref[...] = _bdot(xr, wrkv_ref[0])
    k_ref[...] = _bdot(xk, wrkv_ref[1])
    v_ref[...] = _bdot(xv, wrkv_ref[2])
    w_log = -_softplus(-(w0_ref[...] + _bdot(jnp.tanh(_bdot(xw, w1_ref[...])), w2_ref[...]))) - 0.5
    w_ref[...] = jnp.exp(-jnp.exp(w_log))
    a_ref[...] = _sigmoid(a0_ref[...] + _bdot(_bdot(xa, a1_ref[...]), a2_ref[...]))
    gg_ref[...] = _bdot(_sigmoid(_bdot(xg, g1_ref[...])), g2_ref[...]).astype(BF16)


def _rwkv_core_kernel(r_ref, w_ref, k_ref, v_ref, a_ref, kk_p, ka_p, rk_p, lnw_p, lnb_p, s0_ref,
                      y_ref, st_ref, state, kk_s, b_s, km_s):
    j = pl.program_id(1)
    tc = r_ref.shape[0]
    n = RW_N

    @pl.when(j == 0)
    def _():
        state[...] = s0_ref[...]

    def step(t, c):
        kt, at, vt, rt = k_ref[t], a_ref[t], v_ref[t], r_ref[t]
        kk = kt * kk_p[...]
        kk = kk * lax.rsqrt(jnp.sum(kk * kk, axis=0, keepdims=True) + 1e-6)
        km = kt * (1.0 + (at - 1.0) * ka_p[...])
        kk_s[...] = kk
        b_s[...] = kk * at
        km_s[...] = km
        sa = jnp.zeros((n, LANES), F32)
        for kx in range(n):
            sa = sa + state[kx] * kk_s[pl.ds(kx, 1), :]
        y = jnp.zeros((n, LANES), F32)
        for kx in range(n):
            s_new = (state[kx] * w_ref[t, pl.ds(kx, 1), :] - sa * b_s[pl.ds(kx, 1), :]
                     + vt * km_s[pl.ds(kx, 1), :])
            state[kx] = s_new
            y = y + s_new * r_ref[t, pl.ds(kx, 1), :]
        mean = jnp.mean(y, axis=0, keepdims=True)
        yc = y - mean
        var = jnp.mean(yc * yc, axis=0, keepdims=True)
        bonus = jnp.sum(rt * km * rk_p[...], axis=0, keepdims=True) * vt
        y_ref[t] = yc * lax.rsqrt(var + RW_GN_EPS) * lnw_p[...] + lnb_p[...] + bonus
        return c

    lax.fori_loop(0, tc, step, 0)

    @pl.when(j == pl.num_programs(1) - 1)
    def _():
        st_ref[...] = state[...]


def _rwkv_layer(grp, x, norm_g, mods, shift0, wkv0, mu, w_rkv, w0, w1, w2, a0, a1, a2, g1, g2,
                k_k, k_a, r_k, ln_w, ln_b, w_o):
    b, t = grp.b, grp.t
    shift, scale, gate = mods
    sh, sh_spec = grp.rowmod(shift)
    sc, sc_spec = grp.rowmod(scale)
    if shift0 is None:
        s0 = jnp.zeros((b, 1, D), F32)
    else:
        s0 = jnp.concatenate([shift0[:, None, :], jnp.zeros((b, t - 1, D), F32)], axis=1)
    s0, s0_spec = grp.rowseq(s0)
    bf = lambda z: z.astype(BF16)
    row = lambda z: z.reshape(1, -1)
    weights = [mu, bf(w_rkv), row(w0), bf(w1), bf(w2), row(a0), bf(a1), bf(a2), bf(g1), bf(g2)]
    outs = pl.pallas_call(
        functools.partial(_rwkv_proj_kernel, t),
        out_shape=tuple(jax.ShapeDtypeStruct((grp.n, D), F32) for _ in range(5))
        + (jax.ShapeDtypeStruct((grp.n, D), BF16),),
        grid=(grp.tiles,),
        in_specs=[grp.rows(D), grp.prev8(D), _full((1, D)), sh_spec, sc_spec, s0_spec]
        + [_full(z.shape) for z in weights],
        out_specs=tuple(grp.rows(D) for _ in range(6)),
        compiler_params=_cparams(("arbitrary",)),
        name="rwkv_proj",
    )(x, x, row(norm_g), sh, sc, s0, *weights)
    r, w, k, v, a, gg = outs
    bh = b * RW_H

    def to_core(z):
        return z.reshape(b, t, RW_H, RW_N).transpose(1, 3, 0, 2).reshape(t, RW_N, bh)

    def ptile(p):
        return jnp.tile(p.reshape(RW_H, RW_N).T, (1, b))

    if wkv0 is None:
        st0 = jnp.zeros((RW_N, RW_N, bh), F32)
    else:
        st0 = wkv0.transpose(3, 2, 0, 1).reshape(RW_N, RW_N, bh)
    tc = min(t, 16)
    seq = pl.BlockSpec((tc, RW_N, LANES), lambda q, j: (j, 0, q))
    par = pl.BlockSpec((RW_N, LANES), lambda q, j: (0, q))
    stt = pl.BlockSpec((RW_N, RW_N, LANES), lambda q, j: (0, 0, q))
    y, st = pl.pallas_call(
        _rwkv_core_kernel,
        out_shape=(jax.ShapeDtypeStruct((t, RW_N, bh), F32),
                   jax.ShapeDtypeStruct((RW_N, RW_N, bh), F32)),
        grid=(bh // LANES, t // tc),
        in_specs=[seq] * 5 + [par] * 5 + [stt],
        out_specs=(seq, stt),
        scratch_shapes=[pltpu.VMEM((RW_N, RW_N, LANES), F32)] + [pltpu.VMEM((RW_N, LANES), F32)] * 3,
        compiler_params=_cparams(("arbitrary", "arbitrary")),
        name="rwkv_core",
    )(to_core(r), to_core(w), to_core(k), to_core(v), to_core(a),
      ptile(k_k), ptile(k_a), ptile(r_k.reshape(-1)), ptile(ln_w), ptile(ln_b), st0)
    y_rows = y.reshape(t, RW_N, b, RW_H).transpose(2, 0, 3, 1).reshape(grp.n, D)
    x_new = _outproj_call(grp, x, y_rows, bf(w_o), gate, mul=gg)
    new_wkv = st.reshape(RW_N, RW_N, b, RW_H).transpose(2, 3, 1, 0)
    x_last = x.reshape(b, t, D)[:, -1]
    new_shift = _modrows_call(x_last, norm_g, shift, scale)
    return x_new, new_shift, new_wkv


def _pad_time(z, b, t, tp):
    if tp == t:
        return z
    w = z.shape[-1]
    return jnp.pad(z.reshape(b, t, w), ((0, 0), (0, tp - t), (0, 0))).reshape(b * tp, w)


def _unpad_time(z, b, t, tp):
    if tp == t:
        return z
    w = z.shape[-1]
    return z.reshape(b, tp, w)[:, :t].reshape(b * t, w)


def _gdn_proj_kernel(t_len, x_ref, xp_ref, g_ref, sh_ref, sc_ref, c1_ref, c2_ref, c3_ref, wqkv_ref,
                     wz_ref, wb_ref, wa_ref, cw_ref, alog_ref, dtb_ref,
                     qkv_ref, z_ref, beta_ref, gdec_ref):
    tm = x_ref.shape[0]
    g, sh, sc = g_ref[...], sh_ref[0], sc_ref[0]
    h = _modulate(x_ref[...], g, sh, sc)
    hp = _modulate(xp_ref[...], g, sh[0:8] if sh.shape[0] > 1 else sh, sc[0:8] if sc.shape[0] > 1 else sc)
    hb = h.astype(BF16)
    pre = jnp.dot(hb, wqkv_ref[...], preferred_element_type=F32)
    pre8 = _bdot(hp, wqkv_ref[...])
    tpos = _tpos(tm, t_len)
    cw = cw_ref[...]
    conv = pre * cw[3:4]
    for d, cref in ((1, c1_ref), (2, c2_ref), (3, c3_ref)):
        past = jnp.where(tpos >= d, _shift_rows(pre, pre8, d), cref[0])
        conv = conv + past * cw[3 - d:4 - d]
    act = _silu(conv)
    nh = GD_H
    for hh in range(2 * nh):
        sl = slice(hh * GD_DK, (hh + 1) * GD_DK)
        seg = act[:, sl]
        seg = seg * lax.rsqrt(jnp.sum(seg * seg, axis=-1, keepdims=True) + 1e-6)
        if hh < nh:
            seg = seg * (GD_DK ** -0.5)
        qkv_ref[:, sl] = seg.astype(BF16)
    qkv_ref[:, 2 * nh * GD_DK:] = act[:, 2 * nh * GD_DK:].astype(BF16)
    z_ref[...] = jnp.dot(hb, wz_ref[...], preferred_element_type=F32).astype(BF16)
    beta_ref[...] = _sigmoid(jnp.dot(hb, wb_ref[...], preferred_element_type=F32))
    a_logit = jnp.dot(hb, wa_ref[...], preferred_element_type=F32)
    gdec_ref[...] = -jnp.exp(alog_ref[...]) * _softplus(a_logit + dtb_ref[...])


def _unit_lower_inverse(a, eye, masks):
    blk8, offs = masks
    n = range(len(a))
    a8 = [jnp.where(blk8, a[i], 0.0) for i in n]
    x = [eye - a8[i] for i in n]
    y = [_bdot(a8[i], a8[i]) for i in n]
    x = [x[i] + _bdot(x[i], y[i]) for i in n]
    y = [_bdot(y[i], y[i]) for i in n]
    x = [x[i] + _bdot(x[i], y[i]) for i in n]
    for off in offs:
        t = [_bdot(jnp.where(off, a[i], 0.0), x[i]) for i in n]
        x = [x[i] - _bdot(x[i], t[i]) for i in n]
    return x


def _inverse_masks(c):
    ri = lax.broadcasted_iota(I32, (c, c), 0)
    ci = lax.broadcasted_iota(I32, (c, c), 1)
    sr = lambda z, s: lax.shift_right_logical(z, jnp.full(z.shape, s, I32))
    blk8 = sr(ri, 3) == sr(ci, 3)
    offs = []
    m, lg = 8, 3
    while m < c:
        same = sr(ri, lg + 1) == sr(ci, lg + 1)
        lower = jnp.logical_and((sr(ri, lg) & 1) == 1, (sr(ci, lg) & 1) == 0)
        offs.append(jnp.logical_and(same, lower))
        m, lg = m * 2, lg + 1
    return ri, ci, (blk8, offs)


def _gdn_core_kernel(q_ref, k_ref, v_ref, z_ref, beta_ref, g_ref, s0_ref, nw_ref, y_ref, st_ref, state):
    cidx = pl.program_id(1)
    c = q_ref.shape[0]

    @pl.when(cidx == 0)
    def _():
        state[...] = s0_ref[0]

    ri, ci, masks = _inverse_masks(c)
    incl = ri >= ci
    strict = ri > ci
    eye = (ri == ci).astype(F32)
    g = g_ref[...]
    cum = _fdot(incl.astype(F32), g)
    cum_t = lax.dot_general(g, (ci >= ri).astype(F32), (((0,), (0,)), ((), ())),
                            precision=HIGHEST, preferred_element_type=F32)
    beta = beta_ref[...]
    nw = nw_ref[...]
    heads = range(GD_H)
    sls = [slice(h * GD_DK, (h + 1) * GD_DK) for h in heads]
    q = [q_ref[:, sl] for sl in sls]
    k = [k_ref[:, sl] for sl in sls]
    kf = [z.astype(F32) for z in k]
    v = [v_ref[:, sl].astype(F32) for sl in sls]
    s = [state[h] for h in heads]
    cum_c = [cum[:, h:h + 1] for h in heads]
    dec = [jnp.where(incl, jnp.exp(jnp.where(incl, cum_c[h] - cum_t[h:h + 1, :], 0.0)), 0.0) for h in heads]
    bcol = [beta[:, h:h + 1] for h in heads]
    kb = [kf[h] * bcol[h] for h in heads]
    a = [jnp.where(strict, _bdot_nt(kb[h], k[h]) * dec[h], 0.0) for h in heads]
    attn = [_bdot_nt(q[h], k[h]) * dec[h] for h in heads]
    x = _unit_lower_inverse(a, eye, masks)
    ecum = [jnp.exp(cum_c[h]) for h in heads]
    sol = [_bdot(x[h], jnp.concatenate([v[h] * bcol[h], kb[h] * ecum[h]], axis=1)) for h in heads]
    u = [sol[h][:, :GD_DV] - _bdot(sol[h][:, GD_DV:], s[h]) for h in heads]
    o = [_bdot(q[h].astype(F32) * ecum[h], s[h]) + _bdot(attn[h], u[h]) for h in heads]
    last = [cum[c - 1:c, h:h + 1] for h in heads]
    s_new = [s[h] * jnp.exp(last[h]) + _bdot_tn(kf[h] * jnp.exp(last[h] - cum_c[h]), u[h]) for h in heads]
    for h in heads:
        state[h] = s_new[h]
        on = o[h] * lax.rsqrt(jnp.mean(o[h] * o[h], axis=-1, keepdims=True) + NORM_EPS) * nw
        y_ref[:, sls[h]] = (on * _silu(z_ref[:, sls[h]].astype(F32))).astype(BF16)

    @pl.when(cidx == pl.num_programs(1) - 1)
    def _():
        st_ref[0] = state[...]


def _gdn_layer(grp, x, norm_g, mods, conv0, ssm0, w_in, conv_w, a_log, dt_bias, norm_w, w_o, chunk):
    b, t = grp.b, grp.t
    shift, scale, gate = mods
    sh, sh_spec = grp.rowmod(shift)
    sc, sc_spec = grp.rowmod(scale)
    kd = GD_H * GD_DK
    cstates, cspecs = [], []
    for d in (1, 2, 3):
        if conv0 is None:
            cs = jnp.zeros((b, 1, GD_C), F32)
        else:
            cs = jnp.concatenate([conv0[:, 3 - d:, :], jnp.zeros((b, t - d, GD_C), F32)], axis=1)
        cs, spec = grp.rowseq(cs)
        cstates.append(cs)
        cspecs.append(spec)
    bf = lambda z: z.astype(BF16)
    pad128 = lambda z: jnp.pad(z, ((0, 0), (0, LANES - z.shape[1])))
    w_qkv = bf(w_in[:, :GD_C])
    w_z = bf(w_in[:, GD_C:GD_C + kd])
    w_b = bf(pad128(w_in[:, GD_C + kd:GD_C + kd + GD_H]))
    w_a = bf(pad128(w_in[:, GD_C + kd + GD_H:]))
    weights = [w_qkv, w_z, w_b, w_a, conv_w, pad128(a_log.reshape(1, GD_H)), pad128(dt_bias.reshape(1, GD_H))]
    qkv, z, beta, gdec = pl.pallas_call(
        functools.partial(_gdn_proj_kernel, t),
        out_shape=(jax.ShapeDtypeStruct((grp.n, GD_C), BF16), jax.ShapeDtypeStruct((grp.n, kd), BF16),
                   jax.ShapeDtypeStruct((grp.n, LANES), F32), jax.ShapeDtypeStruct((grp.n, LANES), F32)),
        grid=(grp.tiles,),
        in_specs=[grp.rows(D), grp.prev8(D), _full((1, D)), sh_spec, sc_spec] + cspecs
        + [_full(z_.shape) for z_ in weights],
        out_specs=(grp.rows(GD_C), grp.rows(kd), grp.rows(LANES), grp.rows(LANES)),
        compiler_params=_cparams(("arbitrary",)),
        name="gdn_proj",
    )(x, x, norm_g.reshape(1, D), sh, sc, *cstates, *weights)
    tp = ((t + chunk - 1) // chunk) * chunk
    nc = tp // chunk
    qkv_p, z_p = _pad_time(qkv, b, t, tp), _pad_time(z, b, t, tp)
    beta_p, g_p = _pad_time(beta, b, t, tp), _pad_time(gdec, b, t, tp)
    if ssm0 is None:
        ssm0 = jnp.zeros((b, GD_H, GD_DK, GD_DV), F32)
    col = lambda j: pl.BlockSpec((chunk, kd), lambda bi, c: (bi * nc + c, j))
    lan = pl.BlockSpec((chunk, LANES), lambda bi, c: (bi * nc + c, 0))
    stt = pl.BlockSpec((1, GD_H, GD_DK, GD_DV), lambda bi, c: (bi, 0, 0, 0))
    y, st = pl.pallas_call(
        _gdn_core_kernel,
        out_shape=(jax.ShapeDtypeStruct((b * tp, kd), BF16),
                   jax.ShapeDtypeStruct((b, GD_H, GD_DK, GD_DV), F32)),
        grid=(b, nc),
        in_specs=[col(0), col(1), col(2), col(0), lan, lan, stt, _full((1, GD_DV))],
        out_specs=(col(0), stt),
        scratch_shapes=[pltpu.VMEM((GD_H, GD_DK, GD_DV), F32)],
        compiler_params=_cparams(("arbitrary", "arbitrary")),
        name="gdn_core",
    )(qkv_p, qkv_p, qkv_p, z_p, beta_p, g_p, ssm0, norm_w.reshape(1, GD_DV))
    x_new = _outproj_call(grp, x, _unpad_time(y, b, t, tp), bf(w_o), gate)
    nl = min(t, GD_CONV - 1)
    x_last = x.reshape(b, t, D)[:, t - nl:].reshape(b * nl, D)
    rep = lambda m: jnp.repeat(m, nl, axis=0)
    pre_last = _modrows_call(x_last, norm_g, rep(shift), rep(scale), w_qkv).reshape(b, nl, GD_C)
    if nl < GD_CONV - 1:
        pre_last = jnp.concatenate([conv0[:, nl:], pre_last], axis=1)
    return x_new, pre_last, st


def _ret_proj_kernel(x_ref, g_ref, sh_ref, sc_ref, cos_ref, sin_ref, w_ref, q_ref, k_ref, v_ref, gate_ref):
    h = _modulate(x_ref[...], g_ref[...], sh_ref[0], sc_ref[0]).astype(BF16)
    kd = RT_H * RT_DK
    vd = RT_H * RT_DV
    cos, sin = cos_ref[0], sin_ref[0]
    even = (lax.broadcasted_iota(I32, (1, kd), 1) & 1) == 0

    def rotary(z):
        swapped = jnp.where(even, pltpu.roll(z, kd - 1, 1), pltpu.roll(z, 1, 1))
        return z * cos + swapped * sin

    q_ref[...] = rotary(jnp.dot(h, w_ref[:, 0:kd], preferred_element_type=F32)).astype(BF16)
    k = rotary(jnp.dot(h, w_ref[:, kd:2 * kd], preferred_element_type=F32))
    k_ref[...] = (k * (RT_DK ** -0.5)).astype(BF16)
    v_ref[...] = jnp.dot(h, w_ref[:, 2 * kd:2 * kd + vd], preferred_element_type=F32).astype(BF16)
    gate_ref[...] = jnp.dot(h, w_ref[:, 2 * kd + vd:], preferred_element_type=F32).astype(BF16)


def _ret_core_kernel(q_ref, k_ref, v_ref, gate_ref, dm_ref, qd_ref, kd_ref, cd_ref, s0_ref, nw_ref,
                     y_ref, st_ref, state):
    cidx = pl.program_id(1)

    @pl.when(cidx == 0)
    def _():
        state[...] = s0_ref[0]

    heads = range(RT_H)
    kss = [slice(h * RT_DK, (h + 1) * RT_DK) for h in heads]
    vss = [slice(h * RT_DV, (h + 1) * RT_DV) for h in heads]
    q = [q_ref[:, ks] for ks in kss]
    k = [k_ref[:, ks] for ks in kss]
    v = [v_ref[:, vs] for vs in vss]
    s = [state[h] for h in heads]
    inner = [_bdot_nt(q[h], k[h]) * dm_ref[h] for h in heads]
    cross = [_bdot(q[h], s[h]) * qd_ref[h] for h in heads]
    o = [_bdot(inner[h], v[h]) + cross[h] for h in heads]
    s_new = [s[h] * cd_ref[h] + _bdot_tn(k[h].astype(F32) * kd_ref[h], v[h]) for h in heads]
    for h in heads:
        state[h] = s_new[h]
        on = o[h] * lax.rsqrt(jnp.mean(o[h] * o[h], axis=-1, keepdims=True) + NORM_EPS) * nw_ref[:, vss[h]]
        y_ref[:, vss[h]] = (on * _silu(gate_ref[:, vss[h]].astype(F32))).astype(BF16)

    @pl.when(cidx == pl.num_programs(1) - 1)
    def _():
        st_ref[0] = state[...]


def _ret_layer(grp, x, norm_g, mods, s0, pos0, w_in, norm_w, w_o, chunk):
    b, t = grp.b, grp.t
    shift, scale, gate = mods
    sh, sh_spec = grp.rowmod(shift)
    sc, sc_spec = grp.rowmod(scale)
    kd, vd = RT_H * RT_DK, RT_H * RT_DV
    half = RT_DK // 2
    inv = 1.0 / (10000.0 ** jnp.linspace(0.0, 1.0, half, dtype=F32))
    pos = jnp.arange(t, dtype=F32) + float(pos0)
    ang = pos[:, None] * inv[None, :]
    cos = jnp.repeat(jnp.cos(ang), 2, axis=1)
    sin = jnp.stack([-jnp.sin(ang), jnp.sin(ang)], axis=-1).reshape(t, RT_DK)
    cos4, cos_spec = grp.postab(jnp.tile(cos, (1, RT_H)))
    sin4, sin_spec = grp.postab(jnp.tile(sin, (1, RT_H)))
    wb = w_in.astype(BF16)
    q, k, v, gt = pl.pallas_call(
        _ret_proj_kernel,
        out_shape=(jax.ShapeDtypeStruct((grp.n, kd), BF16), jax.ShapeDtypeStruct((grp.n, kd), BF16),
                   jax.ShapeDtypeStruct((grp.n, vd), BF16), jax.ShapeDtypeStruct((grp.n, vd), BF16)),
        grid=(grp.tiles,),
        in_specs=[grp.rows(D), _full((1, D)), sh_spec, sc_spec, cos_spec, sin_spec, _full(wb.shape)],
        out_specs=(grp.rows(kd), grp.rows(kd), grp.rows(vd), grp.rows(vd)),
        compiler_params=_cparams(("arbitrary",)),
        name="ret_proj",
    )(x, norm_g.reshape(1, D), sh, sc, cos4, sin4, wb)
    tp = ((t + chunk - 1) // chunk) * chunk
    nc = tp // chunk
    nv = min(t, chunk)
    assert tp == t or nc == 1
    log_gamma = jnp.log1p(-jnp.exp2(-5.0 - jnp.arange(RT_H, dtype=F32)))
    idx = jnp.arange(chunk, dtype=F32)
    diff = idx[:, None] - idx[None, :]
    dmask = jnp.where(diff >= 0, jnp.exp(log_gamma[:, None, None] * jnp.maximum(diff, 0.0)), 0.0)
    q_dec = jnp.exp(log_gamma[:, None] * (idx + 1.0))[:, :, None]
    k_dec = jnp.exp(log_gamma[:, None] * jnp.maximum(nv - 1.0 - idx, 0.0))[:, :, None]
    c_dec = jnp.exp(log_gamma * nv)[:, None, None]
    if s0 is None:
        s0 = jnp.zeros((b, RT_H, RT_DK, RT_DV), F32)
    rowk = pl.BlockSpec((chunk, kd), lambda bi, c: (bi * nc + c, 0))
    rowv = pl.BlockSpec((chunk, vd), lambda bi, c: (bi * nc + c, 0))
    stt = pl.BlockSpec((1, RT_H, RT_DK, RT_DV), lambda bi, c: (bi, 0, 0, 0))
    y, st = pl.pallas_call(
        _ret_core_kernel,
        out_shape=(jax.ShapeDtypeStruct((b * tp, vd), BF16),
                   jax.ShapeDtypeStruct((b, RT_H, RT_DK, RT_DV), F32)),
        grid=(b, nc),
        in_specs=[rowk, rowk, rowv, rowv, _full(dmask.shape), _full(q_dec.shape), _full(k_dec.shape),
                  _full(c_dec.shape), stt, _full((1, vd))],
        out_specs=(rowv, stt),
        scratch_shapes=[pltpu.VMEM((RT_H, RT_DK, RT_DV), F32)],
        compiler_params=_cparams(("arbitrary", "arbitrary")),
        name="ret_core",
    )(_pad_time(q, b, t, tp), _pad_time(k, b, t, tp), _pad_time(v, b, t, tp), _pad_time(gt, b, t, tp),
      dmask, q_dec, k_dec, c_dec, s0, norm_w.reshape(1, vd))
    x_new = _outproj_call(grp, x, _unpad_time(y, b, t, tp), w_o.astype(BF16), gate)
    return x_new, st


def _hgrn_proj_kernel(layer, x_ref, g_ref, sh_ref, sc_ref, lbl_ref, w_ref,
                      q_ref, k_ref, lf_ref, v_ref, gate_ref):
    h = _modulate(x_ref[...], g_ref[...], sh_ref[0], sc_ref[0]).astype(BF16)
    ed = HG_H * HG_E
    logits = lbl_ref[...]
    e = jnp.exp(logits - jnp.max(logits, axis=0, keepdims=True))
    lrow = lax.broadcasted_iota(I32, logits.shape, 0)
    part = jnp.where(jnp.logical_and(lrow >= 1, lrow <= layer), e, 0.0)
    lb = jnp.sum(part, axis=0, keepdims=True) / jnp.sum(e, axis=0, keepdims=True)
    q_ref[...] = jnp.dot(h, w_ref[:, 0:ed], preferred_element_type=F32)
    f = lb + (1.0 - lb) * _sigmoid(jnp.dot(h, w_ref[:, ed:2 * ed], preferred_element_type=F32))
    k_ref[...] = 1.0 - f
    lf_ref[...] = jnp.log(f)
    v_ref[...] = jnp.dot(h, w_ref[:, 2 * ed:3 * ed], preferred_element_type=F32).astype(BF16)
    gate_ref[...] = jnp.dot(h, w_ref[:, 3 * ed:], preferred_element_type=F32).astype(BF16)


def _hgrn_core_kernel(q_ref, k_ref, lf_ref, v_ref, gate_ref, s0_ref, nw_ref, y_ref, st_ref, state):
    cidx = pl.program_id(1)
    c = q_ref.shape[0]

    @pl.when(cidx == 0)
    def _():
        state[...] = s0_ref[0]

    ri = lax.broadcasted_iota(I32, (c, c), 0)
    ci = lax.broadcasted_iota(I32, (c, c), 1)
    ltri = (ri >= ci).astype(F32)
    row8 = lax.broadcasted_iota(I32, (8, 1), 0)
    heads = range(HG_H)
    sls = [slice(h * HG_E, (h + 1) * HG_E) for h in heads]
    q = [q_ref[:, sl] for sl in sls]
    k = [k_ref[:, sl] for sl in sls]
    v = [v_ref[:, sl].astype(F32) for sl in sls]
    st = [state[h] for h in heads]
    cum = [_fdot(ltri, lf_ref[:, sl]) for sl in sls]
    inter = [_bdot_nt(q[h] * jnp.exp(cum[h]), st[h]) for h in heads]
    last = [cum[h][c - 1:c, :] for h in heads]
    s_new = [st[h] * jnp.exp(last[h]) + _bdot_tn(v[h], k[h] * jnp.exp(last[h] - cum[h])) for h in heads]
    for h in heads:
        state[h] = s_new[h]
        parts = []
        for g0 in range(0, c, 8):
            qg, cg = q[h][g0:g0 + 8], cum[h][g0:g0 + 8]
            acc = inter[h][g0:g0 + 8]
            for j in range(g0 + 8):
                diff = cg - cum[h][j:j + 1, :]
                if j >= g0:
                    causal = row8 >= (j - g0)
                    diff = jnp.where(causal, diff, 0.0)
                col = jnp.sum(qg * k[h][j:j + 1, :] * jnp.exp(diff), axis=-1, keepdims=True)
                if j >= g0:
                    col = jnp.where(causal, col, 0.0)
                acc = acc + col * v[h][j:j + 1, :]
            parts.append(acc)
        o = parts[0] if len(parts) == 1 else jnp.concatenate(parts, axis=0)
        on = o * lax.rsqrt(jnp.mean(o * o, axis=-1, keepdims=True) + NORM_EPS) * nw_ref[:, sls[h]]
        y_ref[:, sls[h]] = (on * _silu(gate_ref[:, sls[h]].astype(F32))).astype(BF16)

    @pl.when(cidx == pl.num_programs(1) - 1)
    def _():
        st_ref[0] = state[...]


def _hgrn_layer(grp, x, norm_g, mods, s0, layer, lb_logits, w_in, norm_w, w_o, chunk):
    b, t = grp.b, grp.t
    shift, scale, gate = mods
    sh, sh_spec = grp.rowmod(shift)
    sc, sc_spec = grp.rowmod(scale)
    ed, vd = HG_H * HG_E, HG_H * HG_DV
    wb = w_in.astype(BF16)
    q, k, lf, v, gt = pl.pallas_call(
        functools.partial(_hgrn_proj_kernel, layer),
        out_shape=(jax.ShapeDtypeStruct((grp.n, ed), F32), jax.ShapeDtypeStruct((grp.n, ed), F32),
                   jax.ShapeDtypeStruct((grp.n, ed), F32), jax.ShapeDtypeStruct((grp.n, vd), BF16),
                   jax.ShapeDtypeStruct((grp.n, vd), BF16)),
        grid=(grp.tiles,),
        in_specs=[grp.rows(D), _full((1, D)), sh_spec, sc_spec, _full(lb_logits.shape), _full(wb.shape)],
        out_specs=(grp.rows(ed), grp.rows(ed), grp.rows(ed), grp.rows(vd), grp.rows(vd)),
        compiler_params=_cparams(("arbitrary",)),
        name="hgrn_proj",
    )(x, norm_g.reshape(1, D), sh, sc, lb_logits, wb)
    tp = ((t + chunk - 1) // chunk) * chunk
    nc = tp // chunk
    if s0 is None:
        s0 = jnp.zeros((b, HG_H, HG_E, HG_DV), F32)
    row = pl.BlockSpec((chunk, ed), lambda bi, c: (bi * nc + c, 0))
    stt = pl.BlockSpec((1, HG_H, HG_E, HG_DV), lambda bi, c: (bi, 0, 0, 0))
    y, st = pl.pallas_call(
        _hgrn_core_kernel,
        out_shape=(jax.ShapeDtypeStruct((b * tp, vd), BF16),
                   jax.ShapeDtypeStruct((b, HG_H, HG_E, HG_DV), F32)),
        grid=(b, nc),
        in_specs=[row, row, row, row, row, stt, _full((1, vd))],
        out_specs=(row, stt),
        scratch_shapes=[pltpu.VMEM((HG_H, HG_E, HG_DV), F32)],
        compiler_params=_cparams(("arbitrary", "arbitrary")),
        name="hgrn_core",
    )(_pad_time(q, b, t, tp), _pad_time(k, b, t, tp), _pad_time(lf, b, t, tp), _pad_time(v, b, t, tp),
      _pad_time(gt, b, t, tp), jnp.swapaxes(s0, 2, 3), norm_w.reshape(1, vd))
    x_new = _outproj_call(grp, x, _unpad_time(y, b, t, tp), w_o.astype(BF16), gate)
    return x_new, jnp.swapaxes(st, 2, 3)


ROW_TILE = 256
MOE_TILE = 512
GDN_CHUNK, RET_CHUNK, HGRN_CHUNK = 64, 128, 16
SAMPLE_CHUNK = 16
PAST_LEN = 16384


def kernel(x_prompt, x_sample, c_prompt, c_sample, state_rwkv_wkv, state_rwkv_shift, state_gdn_ssm, state_gdn_conv, state_ret, state_hgrn, ada_w, ada_b, norm_mix, norm_ffn, norm_final, rwkv_mu, rwkv_w_rkv, rwkv_w0, rwkv_w1, rwkv_w2, rwkv_a0, rwkv_a1, rwkv_a2, rwkv_g1, rwkv_g2, rwkv_k_k, rwkv_k_a, rwkv_r_k, rwkv_ln_w, rwkv_ln_b, rwkv_w_o, gdn_w_in, gdn_conv_w, gdn_a_log, gdn_dt_bias, gdn_norm_w, gdn_w_o, ret_w_in, ret_norm_w, ret_w_o, hgrn_w_in, hgrn_lb_logits, hgrn_norm_w, hgrn_w_o, moe_w_router, moe_b_router, moe_w_gu, moe_b_gu, moe_w_down, moe_b_down):
    bp, tp, _ = x_prompt.shape
    bs, ts, _ = x_sample.shape
    gp, gs = _Group(bp, tp, ROW_TILE), _Group(bs, ts, ROW_TILE)
    mp, msg = _Group(bp, tp, MOE_TILE), _Group(bs, ts, MOE_TILE)
    ada = _ada_call(jnp.concatenate([c_prompt, c_sample], axis=0), ada_w, ada_b)
    xp = x_prompt.reshape(bp * tp, D)
    xs = x_sample.reshape(bs * ts, D)
    outs_p = {k: [] for k in ("wkv", "shift", "ssm", "conv", "ret", "hgrn")}
    outs_s = {k: [] for k in ("wkv", "shift", "ssm", "conv", "ret", "hgrn")}
    for i in range(DEPTH):
        kind, j = i % 4, i // 4
        m = ada[i].reshape(bp + bs, 6, D)
        mod_p = [m[:bp, n] for n in range(6)]
        mod_s = [m[bp:, n] for n in range(6)]
        g = norm_mix[i]
        if kind == 0:
            prm = (rwkv_mu[j], rwkv_w_rkv[j], rwkv_w0[j], rwkv_w1[j], rwkv_w2[j], rwkv_a0[j], rwkv_a1[j],
                   rwkv_a2[j], rwkv_g1[j], rwkv_g2[j], rwkv_k_k[j], rwkv_k_a[j], rwkv_r_k[j],
                   rwkv_ln_w[j], rwkv_ln_b[j], rwkv_w_o[j])
            xp, sh_p, wkv_p = _rwkv_layer(gp, xp, g, mod_p[:3], None, None, *prm)
            xs, sh_s, wkv_s = _rwkv_layer(gs, xs, g, mod_s[:3], state_rwkv_shift[j], state_rwkv_wkv[j], *prm)
            outs_p["wkv"].append(wkv_p); outs_p["shift"].append(sh_p)
            outs_s["wkv"].append(wkv_s); outs_s["shift"].append(sh_s)
        elif kind == 1:
            prm = (gdn_w_in[j], gdn_conv_w[j], gdn_a_log[j], gdn_dt_bias[j], gdn_norm_w[j], gdn_w_o[j])
            xp, cv_p, ss_p = _gdn_layer(gp, xp, g, mod_p[:3], None, None, *prm, GDN_CHUNK)
            xs, cv_s, ss_s = _gdn_layer(gs, xs, g, mod_s[:3], state_gdn_conv[j], state_gdn_ssm[j], *prm,
                                        SAMPLE_CHUNK)
            outs_p["ssm"].append(ss_p); outs_p["conv"].append(cv_p)
            outs_s["ssm"].append(ss_s); outs_s["conv"].append(cv_s)
        elif kind == 2:
            prm = (ret_w_in[j], ret_norm_w[j], ret_w_o[j])
            xp, r_p = _ret_layer(gp, xp, g, mod_p[:3], None, 0, *prm, RET_CHUNK)
            xs, r_s = _ret_layer(gs, xs, g, mod_s[:3], state_ret[j], PAST_LEN, *prm, SAMPLE_CHUNK)
            outs_p["ret"].append(r_p); outs_s["ret"].append(r_s)
        else:
            prm = (i, hgrn_lb_logits, hgrn_w_in[j], hgrn_norm_w[j], hgrn_w_o[j])
            xp, h_p = _hgrn_layer(gp, xp, g, mod_p[:3], None, *prm, HGRN_CHUNK)
            xs, h_s = _hgrn_layer(gs, xs, g, mod_s[:3], state_hgrn[j], *prm, SAMPLE_CHUNK)
            outs_p["hgrn"].append(h_p); outs_s["hgrn"].append(h_s)
        xp, xs = _moe(i, [mp, msg], [xp, xs], norm_ffn[i], [mod_p[3:], mod_s[3:]], moe_w_router[i],
                      moe_b_router[i], moe_w_gu, moe_b_gu, moe_w_down, moe_b_down)
    y_prompt = _final_call(gp, xp, norm_final).reshape(bp, tp, D)
    y_sample = _final_call(gs, xs, norm_final).reshape(bs, ts, D)
    order = ("wkv", "shift", "ssm", "conv", "ret", "hgrn")
    return ((y_prompt, y_sample) + tuple(jnp.stack(outs_p[k]) for k in order)
            + tuple(jnp.stack(outs_s[k]) for k in order))
```

```python
import functools
import math

import jax
import jax.numpy as jnp
from jax import lax
from jax.experimental import pallas as pl
from jax.experimental.pallas import tpu as pltpu

F32 = jnp.float32
BF16 = jnp.bfloat16
I32 = jnp.int32
HIGHEST = lax.Precision.HIGHEST

D = 1024
DEPTH = 4
NORM_EPS = 1e-6
RW_H, RW_N = 16, 64
RW_GN_EPS = 64e-5
GD_H, GD_DK, GD_DV, GD_CONV = 8, 128, 128, 4
GD_C = 3 * GD_H * GD_DK
RT_H, RT_DK, RT_DV = 4, 256, 512
HG_H, HG_E, HG_DV = 8, 128, 128
N_EXPERTS, TOP_K, D_FF = 32, 4, 1024
SWIGLU_LIMIT, SWIGLU_ALPHA = 7.0, 1.702

LANES = 128
EXPERT_TILE = 512
TOKEN_BLOCK = 256
VMEM_LIMIT = 56 * 1024 * 1024


def _cparams(sem, vmem=VMEM_LIMIT):
    return pltpu.CompilerParams(dimension_semantics=sem, vmem_limit_bytes=vmem)


def _sigmoid(x):
    return 1.0 / (1.0 + jnp.exp(-x))


def _silu(x):
    return x * _sigmoid(x)


def _softplus(x):
    return jnp.maximum(x, 0.0) + jnp.log(1.0 + jnp.exp(-jnp.abs(x)))


def _modulate(x, g, shift, scale):
    ms = jnp.mean(x * x, axis=-1, keepdims=True)
    return (x * lax.rsqrt(ms + NORM_EPS) * g) * (1.0 + scale) + shift


def _bdot(a, b):
    return jnp.dot(a.astype(BF16), b.astype(BF16), preferred_element_type=F32)


def _bdot_nt(a, b):
    return lax.dot_general(a.astype(BF16), b.astype(BF16), (((1,), (1,)), ((), ())),
                           preferred_element_type=F32)


def _bdot_tn(a, b):
    return lax.dot_general(a.astype(BF16), b.astype(BF16), (((0,), (0,)), ((), ())),
                           preferred_element_type=F32)


def _fdot(a, b):
    return jnp.dot(a, b, precision=HIGHEST, preferred_element_type=F32)


class _Group:
    def __init__(self, b, t, tm):
        self.b, self.t, self.n = b, t, b * t
        self.tm = min(tm, self.n)
        assert self.n % self.tm == 0
        assert (self.t % self.tm == 0) or (self.tm % self.t == 0)
        self.per_batch = self.t % self.tm == 0
        self.tiles = self.n // self.tm

    def rows(self, width):
        return pl.BlockSpec((self.tm, width), lambda i: (i, 0))

    def rowmod(self, arr):
        w = arr.shape[-1]
        if self.per_batch:
            k = self.t // self.tm
            return arr.reshape(self.b, 1, w), pl.BlockSpec((1, 1, w), lambda i: (i // k, 0, 0))
        rep = jnp.repeat(arr, self.t, axis=0).reshape(self.tiles, self.tm, w)
        return rep, pl.BlockSpec((1, self.tm, w), lambda i: (i, 0, 0))

    def rowseq(self, arr):
        w = arr.shape[-1]
        if self.per_batch:
            assert arr.shape[1] == 1
            k = self.t // self.tm
            return arr, pl.BlockSpec((1, 1, w), lambda i: (i // k, 0, 0))
        return arr.reshape(self.tiles, self.tm, w), pl.BlockSpec((1, self.tm, w), lambda i: (i, 0, 0))

    def postab(self, tab):
        w = tab.shape[-1]
        if self.per_batch:
            k = self.t // self.tm
            return tab.reshape(k, self.tm, w), pl.BlockSpec((1, self.tm, w), lambda i: (i % k, 0, 0))
        rep = jnp.tile(tab, (self.tm // self.t, 1)).reshape(1, self.tm, w)
        return rep, pl.BlockSpec((1, self.tm, w), lambda i: (0, 0, 0))

    def prev8(self, width):
        k = self.tm // 8
        return pl.BlockSpec((8, width), lambda i: (jnp.maximum(i * k - 1, 0), 0))


def _full(shape):
    nd = len(shape)
    return pl.BlockSpec(shape, lambda *a: (0,) * nd)


def _tpos(tm, t):
    row = pl.program_id(0) * tm + lax.broadcasted_iota(I32, (tm, 1), 0)
    return row % t


def _shift_rows(cur, prev8, d):
    rolled = pltpu.roll(cur, d, 0)
    head = jnp.where(lax.broadcasted_iota(I32, (8, 1), 0) < d, pltpu.roll(prev8, d, 0), rolled[0:8])
    if cur.shape[0] == 8:
        return head
    return jnp.concatenate([head, rolled[8:]], axis=0)


def _ada_kernel(c_ref, w_ref, b_ref, o_ref):
    o_ref[0] = _bdot(_silu(c_ref[...]), w_ref[0]) + b_ref[0]


def _ada_call(c_all, ada_w, ada_b):
    nb = c_all.shape[0]
    tn = 1536
    return pl.pallas_call(
        _ada_kernel,
        out_shape=jax.ShapeDtypeStruct((DEPTH, nb, 6 * D), F32),
        grid=(DEPTH, 6 * D // tn),
        in_specs=[pl.BlockSpec((nb, D), lambda l, j: (0, 0)),
                  pl.BlockSpec((1, D, tn), lambda l, j: (l, 0, j)),
                  pl.BlockSpec((1, 1, tn), lambda l, j: (l, 0, j))],
        out_specs=pl.BlockSpec((1, nb, tn), lambda l, j: (l, 0, j)),
        compiler_params=_cparams(("arbitrary", "arbitrary")),
        name="adaln",
    )(c_all, ada_w, ada_b.reshape(DEPTH, 1, 6 * D))


def _modrows_kernel(x_ref, g_ref, sh_ref, sc_ref, *rest):
    h = _modulate(x_ref[...], g_ref[...], sh_ref[...], sc_ref[...])
    if len(rest) == 2:
        w_ref, o_ref = rest
        o_ref[...] = _bdot(h, w_ref[...])
    else:
        rest[0][...] = h


def _modrows_call(x, g, shift, scale, w=None):
    n = x.shape[0]
    args = [x, g.reshape(1, D), shift, scale]
    specs = [_full((n, D)), _full((1, D)), _full((n, D)), _full((n, D))]
    width = D
    if w is not None:
        args.append(w)
        specs.append(_full(w.shape))
        width = w.shape[1]
    return pl.pallas_call(
        _modrows_kernel,
        out_shape=jax.ShapeDtypeStruct((n, width), F32),
        grid=(1,),
        in_specs=specs,
        out_specs=_full((n, width)),
        compiler_params=_cparams(("arbitrary",)),
        name="modrows",
    )(*args)


def _outproj_kernel(has_mul, x_ref, y_ref, *rest):
    if has_mul:
        m_ref, w_ref, gt_ref, o_ref = rest
        y = y_ref[...].astype(F32) * m_ref[...].astype(F32)
    else:
        w_ref, gt_ref, o_ref = rest
        y = y_ref[...]
    o_ref[...] = x_ref[...] + gt_ref[0] * _bdot(y, w_ref[...])


def _outproj_call(grp, x, y, w_o, gate, mul=None):
    dy = y.shape[1]
    gt, gt_spec = grp.rowmod(gate)
    args = [x, y]
    specs = [grp.rows(D), grp.rows(dy)]
    if mul is not None:
        args.append(mul)
        specs.append(grp.rows(dy))
    args += [w_o, gt]
    specs += [_full(w_o.shape), gt_spec]
    return pl.pallas_call(
        functools.partial(_outproj_kernel, mul is not None),
        out_shape=jax.ShapeDtypeStruct((grp.n, D), F32),
        grid=(grp.tiles,),
        in_specs=specs,
        out_specs=grp.rows(D),
        compiler_params=_cparams(("arbitrary",)),
        name="outproj",
    )(*args)


def _router_kernel(cin_ref, x_ref, g_ref, sh_ref, sc_ref, wr_ref, br_ref,
                   h_ref, idx_ref, gate_ref, rank_ref, cnt_ref, carry):
    i = pl.program_id(0)

    @pl.when(i == 0)
    def _():
        carry[...] = cin_ref[...]

    tm = x_ref.shape[0]
    h = _modulate(x_ref[...], g_ref[...], sh_ref[0], sc_ref[0])
    h_ref[...] = h
    logits = _fdot(h, wr_ref[...]) + br_ref[...]
    lane = lax.broadcasted_iota(I32, logits.shape, 1)
    work = logits
    sel = jnp.zeros(logits.shape, jnp.bool_)
    picks, vals = [], []
    for _ in range(TOP_K):
        m = jnp.max(work, axis=-1, keepdims=True)
        idx = jnp.min(jnp.where(work == m, lane, N_EXPERTS), axis=-1, keepdims=True)
        pick = lane == idx
        picks.append((idx, pick))
        vals.append(m)
        sel = jnp.logical_or(sel, pick)
        work = jnp.where(pick, -jnp.inf, work)
    es = [jnp.exp(v - vals[0]) for v in vals]
    denom = es[0] + es[1] + es[2] + es[3]
    self_f = sel.astype(F32)
    tri = (lax.broadcasted_iota(I32, (tm, tm), 0) > lax.broadcasted_iota(I32, (tm, tm), 1))
    local = jnp.dot(tri.astype(BF16), self_f.astype(BF16), preferred_element_type=F32)
    rank = local + carry[...]
    carry[...] = carry[...] + jnp.sum(self_f, axis=0, keepdims=True)
    cnt_ref[...] = carry[...]
    lane_o = lax.broadcasted_iota(I32, (tm, LANES), 1)
    idx_o = jnp.zeros((tm, LANES), I32)
    gate_o = jnp.zeros((tm, LANES), F32)
    rank_o = jnp.zeros((tm, LANES), I32)
    for k in range(TOP_K):
        idx, pick = picks[k]
        rk = jnp.sum(jnp.where(pick, rank, 0.0), axis=-1, keepdims=True)
        idx_o = jnp.where(lane_o == k, idx, idx_o)
        gate_o = jnp.where(lane_o == k, es[k] / denom, gate_o)
        rank_o = jnp.where(lane_o == k, rk.astype(I32), rank_o)
    idx_ref[...] = idx_o
    gate_ref[...] = gate_o
    rank_ref[...] = rank_o


def _router_call(grp, counts_in, x, g, shift, scale, w_router, b_router):
    sh, sh_spec = grp.rowmod(shift)
    sc, sc_spec = grp.rowmod(scale)
    pad = pl.BlockSpec((grp.tm, LANES), lambda i: (i, 0))
    return pl.pallas_call(
        _router_kernel,
        out_shape=(jax.ShapeDtypeStruct((grp.n, D), F32),
                   jax.ShapeDtypeStruct((grp.n, LANES), I32),
                   jax.ShapeDtypeStruct((grp.n, LANES), F32),
                   jax.ShapeDtypeStruct((grp.n, LANES), I32),
                   jax.ShapeDtypeStruct((1, N_EXPERTS), F32)),
        grid=(grp.tiles,),
        in_specs=[_full((1, N_EXPERTS)), grp.rows(D), _full((1, D)), sh_spec, sc_spec,
                  _full((D, N_EXPERTS)), _full((1, N_EXPERTS))],
        out_specs=(grp.rows(D), pad, pad, pad, _full((1, N_EXPERTS))),
        scratch_shapes=[pltpu.VMEM((1, N_EXPERTS), F32)],
        compiler_params=_cparams(("arbitrary",)),
        name="moe_router",
    )(counts_in, x, g.reshape(1, D), sh, sc, w_router, b_router.reshape(1, N_EXPERTS))


def _dispatch_kernel(cnt_ref, off_ref, nv_ref, pos_ref, h_ref, xs_ref, zbuf, sem, sem_z):
    i = pl.program_id(0)

    for t in range(TOKEN_BLOCK):
        for k in range(TOP_K):
            pltpu.make_async_copy(h_ref.at[pl.ds(t, 1)], xs_ref.at[pl.ds(pos_ref[t * TOP_K + k], 1)],
                                  sem).start()
    for k in range(TOP_K):
        pltpu.make_async_copy(h_ref, xs_ref.at[pl.ds(0, TOKEN_BLOCK)], sem).wait()

    @pl.when(i == pl.num_programs(0) - 1)
    def _():
        zbuf[...] = jnp.zeros(zbuf.shape, F32)
        bits = [1 << s for s in range(EXPERT_TILE.bit_length() - 2, 2, -1)]

        def pad_copies(e, wait):
            n = cnt_ref[e]
            start = off_ref[e] + n
            end = off_ref[e] + ((n + EXPERT_TILE - 1) // EXPERT_TILE) * EXPERT_TILE
            head = (-start) & 7

            def one(r, c):
                cp = pltpu.make_async_copy(zbuf.at[pl.ds(0, 1)], xs_ref.at[pl.ds(start + r, 1)], sem_z)
                if wait:
                    cp.wait()
                else:
                    cp.start()
                return c

            lax.fori_loop(0, head, one, 0)
            start8 = start + head
            rem = end - start8
            for bit in bits:
                @pl.when((rem & bit) != 0)
                def _():
                    s = pl.multiple_of(start8 + (rem & ~(2 * bit - 1)), 8)
                    cp = pltpu.make_async_copy(zbuf.at[pl.ds(0, bit)], xs_ref.at[pl.ds(s, bit)], sem_z)
                    if wait:
                        cp.wait()
                    else:
                        cp.start()

        def tail_copy(j):
            return pltpu.make_async_copy(zbuf, xs_ref.at[pl.ds(j * EXPERT_TILE, EXPERT_TILE)], sem_z)

        n_tiles = xs_ref.shape[0] // EXPERT_TILE
        for wait in (False, True):
            def per_expert(e, c):
                pad_copies(e, wait)
                return c

            def per_tail(j, c):
                if wait:
                    tail_copy(j).wait()
                else:
                    tail_copy(j).start()
                return c

            lax.fori_loop(0, N_EXPERTS, per_expert, 0)
            lax.fori_loop(nv_ref[0], n_tiles, per_tail, 0)


def _dispatch_call(counts, offsets, n_valid, pos_flat, h, n_rows):
    n = h.shape[0]
    return pl.pallas_call(
        _dispatch_kernel,
        out_shape=jax.ShapeDtypeStruct((n_rows, D), F32),
        grid_spec=pltpu.PrefetchScalarGridSpec(
            num_scalar_prefetch=3,
            grid=(n // TOKEN_BLOCK,),
            in_specs=[pl.BlockSpec((TOKEN_BLOCK * TOP_K,), lambda i, c, o, v: (i,), memory_space=pltpu.SMEM),
                      pl.BlockSpec((TOKEN_BLOCK, D), lambda i, c, o, v: (i, 0))],
            out_specs=pl.BlockSpec(memory_space=pl.ANY),
            scratch_shapes=[pltpu.VMEM((EXPERT_TILE, D), F32), pltpu.SemaphoreType.DMA(()),
                            pltpu.SemaphoreType.DMA(())]),
        compiler_params=_cparams(("arbitrary",)),
        name="moe_dispatch",
    )(counts, offsets, n_valid, pos_flat, h)


def _expert_kernel(te_ref, nv_ref, x_ref, wgu_ref, bgu_ref, wd_ref, bd_ref, o_ref, wgu_s, wd_s):
    j = pl.program_id(0)
    fresh = jnp.logical_or(j == 0, te_ref[j] != te_ref[jnp.maximum(j - 1, 0)])

    @pl.when(jnp.logical_and(j < nv_ref[0], fresh))
    def _():
        wgu_s[...] = wgu_ref[0, 0].astype(BF16)
        wd_s[...] = wd_ref[0, 0].astype(BF16)

    @pl.when(j < nv_ref[0])
    def _():
        x = x_ref[...].astype(BF16)
        gu = jnp.dot(x, wgu_s[...], preferred_element_type=F32) + bgu_ref[0, 0]
        gl = jnp.minimum(gu[:, :D_FF], SWIGLU_LIMIT)
        up = jnp.clip(gu[:, D_FF:], -SWIGLU_LIMIT, SWIGLU_LIMIT)
        act = (up + 1.0) * gl * _sigmoid(SWIGLU_ALPHA * gl)
        o_ref[...] = jnp.dot(act.astype(BF16), wd_s[...], preferred_element_type=F32) + bd_ref[0, 0]

    @pl.when(j >= nv_ref[0])
    def _():
        o_ref[...] = jnp.zeros(o_ref.shape, F32)


def _expert_call(layer, tile_expert, n_valid, xs, w_gu, b_gu, w_down, b_down):
    n_rows = xs.shape[0]
    g = n_rows // EXPERT_TILE
    return pl.pallas_call(
        _expert_kernel,
        out_shape=jax.ShapeDtypeStruct((n_rows, D), F32),
        grid_spec=pltpu.PrefetchScalarGridSpec(
            num_scalar_prefetch=2,
            grid=(g,),
            in_specs=[pl.BlockSpec((EXPERT_TILE, D), lambda j, te, nv: (jnp.minimum(j, nv[0] - 1), 0)),
                      pl.BlockSpec((1, 1, D, 2 * D_FF), lambda j, te, nv: (layer, te[j], 0, 0)),
                      pl.BlockSpec((1, 1, 1, 2 * D_FF), lambda j, te, nv: (layer, te[j], 0, 0)),
                      pl.BlockSpec((1, 1, D_FF, D), lambda j, te, nv: (layer, te[j], 0, 0)),
                      pl.BlockSpec((1, 1, 1, D), lambda j, te, nv: (layer, te[j], 0, 0))],
            out_specs=pl.BlockSpec((EXPERT_TILE, D), lambda j, te, nv: (j, 0)),
            scratch_shapes=[pltpu.VMEM((D, 2 * D_FF), BF16), pltpu.VMEM((D_FF, D), BF16)]),
        compiler_params=_cparams(("arbitrary",)),
        name="moe_experts",
    )(tile_expert, n_valid, xs, w_gu, b_gu.reshape(-1, N_EXPERTS, 1, 2 * D_FF), w_down,
      b_down.reshape(-1, N_EXPERTS, 1, D))


def _combine_kernel(with_norm, pos_ref, x_ref, gate_ref, gt_ref, ys_ref, *rest):
    if with_norm:
        ng_ref, o_ref, y_ref, buf, sems = rest
    else:
        o_ref, buf, sems = rest
    half = TOKEN_BLOCK // 2
    for hf in range(2):
        for t in range(hf * half, (hf + 1) * half):
            for k in range(TOP_K):
                pltpu.make_async_copy(ys_ref.at[pl.ds(pos_ref[t * TOP_K + k], 1)], buf.at[k, pl.ds(t, 1)],
                                      sems.at[hf]).start()
    for hf in range(2):
        rows = pl.ds(hf * half, half)
        for k in range(TOP_K):
            pltpu.make_async_copy(ys_ref.at[pl.ds(0, half)], buf.at[k, rows], sems.at[hf]).wait()
        gate = gate_ref[rows, :]
        f = gate[:, 0:1] * buf[0, rows]
        for k in range(1, TOP_K):
            f = f + gate[:, k:k + 1] * buf[k, rows]
        gt = gt_ref[0]
        x_new = x_ref[rows, :] + (gt if gt.shape[0] == 1 else gt[hf * half:(hf + 1) * half]) * f
        o_ref[rows, :] = x_new
        if with_norm:
            ms = jnp.mean(x_new * x_new, axis=-1, keepdims=True)
            y_ref[rows, :] = x_new * lax.rsqrt(ms + NORM_EPS) * ng_ref[...]


def _combine_call(grp, pos_flat, x, gate_pad, gate2, ys, final_g=None):
    gt, gt_spec = grp.rowmod(gate2)
    assert grp.tm == TOKEN_BLOCK
    with_norm = final_g is not None
    args = [pos_flat, x, gate_pad, gt, ys]
    specs = [pl.BlockSpec((TOKEN_BLOCK * TOP_K,), lambda i: (i,), memory_space=pltpu.SMEM),
             grp.rows(D), grp.rows(LANES), gt_spec, pl.BlockSpec(memory_space=pl.ANY)]
    shape = jax.ShapeDtypeStruct((grp.n, D), F32)
    if with_norm:
        args.append(final_g.reshape(1, D))
        specs.append(_full((1, D)))
    return pl.pallas_call(
        functools.partial(_combine_kernel, with_norm),
        out_shape=(shape, shape) if with_norm else shape,
        grid=(grp.tiles,),
        in_specs=specs,
        out_specs=(grp.rows(D), grp.rows(D)) if with_norm else grp.rows(D),
        scratch_shapes=[pltpu.VMEM((TOP_K, TOKEN_BLOCK, D), F32), pltpu.SemaphoreType.DMA((2,))],
        compiler_params=_cparams(("arbitrary",)),
        name="moe_combine",
    )(*args)


def _moe(layer, groups, xs_in, norm_g, mods, w_router, b_router, w_gu, b_gu, w_down, b_down, final_g=None):
    counts = jnp.zeros((1, N_EXPERTS), F32)
    hs, idxs, gates, ranks = [], [], [], []
    for grp, x, (sh2, sc2, _) in zip(groups, xs_in, mods):
        h, idx, gate, rank, counts = _router_call(grp, counts, x, norm_g, sh2, sc2, w_router, b_router)
        hs.append(h)
        idxs.append(idx[:, :TOP_K])
        gates.append(gate)
        ranks.append(rank[:, :TOP_K])
    h_all = jnp.concatenate(hs, axis=0)
    idx_all = jnp.concatenate(idxs, axis=0)
    rank_all = jnp.concatenate(ranks, axis=0)
    n = h_all.shape[0]
    cnt = counts[0].astype(I32)
    padded = ((cnt + EXPERT_TILE - 1) // EXPERT_TILE) * EXPERT_TILE
    ends = jnp.cumsum(padded)
    offsets = ends - padded
    n_tiles = (n * TOP_K + N_EXPERTS * (EXPERT_TILE - 1)) // EXPERT_TILE
    n_rows = n_tiles * EXPERT_TILE
    pos = (jnp.take(offsets, idx_all) + rank_all).astype(I32)
    pos_flat = pos.reshape(n * TOP_K)
    n_valid = (ends[-1] // EXPERT_TILE).astype(I32)
    tile_start = jnp.arange(n_tiles, dtype=I32) * EXPERT_TILE
    tile_start = jnp.minimum(tile_start, ends[-1] - EXPERT_TILE)
    tile_expert = jnp.sum(tile_start[:, None] >= ends[None, :], axis=1).astype(I32)
    n_valid = n_valid.reshape(1)
    xs = _dispatch_call(cnt, offsets.astype(I32), n_valid, pos_flat, h_all, n_rows)
    ys = _expert_call(layer, tile_expert, n_valid, xs, w_gu, b_gu, w_down, b_down)
    outs = []
    start = 0
    for grp, x, gate, (_, _, gt2) in zip(groups, xs_in, gates, mods):
        cgrp = _Group(grp.b, grp.t, TOKEN_BLOCK)
        p = lax.dynamic_slice_in_dim(pos_flat, start * TOP_K, grp.n * TOP_K)
        outs.append(_combine_call(cgrp, p, x, gate, gt2, ys, final_g))
        start += grp.n
    return outs


def _rwkv_proj_kernel(t_len, x_ref, xp_ref, g_ref, sh_ref, sc_ref, s0_ref, mu_ref, wrkv_ref, w0_ref,
                      w1_ref, w2_ref, a0_ref, a1_ref, a2_ref, g1_ref, g2_ref,
                      r_ref, w_ref, k_ref, v_ref, a_ref, gg_ref):
    tm = x_ref.shape[0]
    g, sh, sc = g_ref[...], sh_ref[0], sc_ref[0]
    h = _modulate(x_ref[...], g, sh, sc)
    hp = _modulate(xp_ref[...], g, sh[0:8] if sh.shape[0] > 1 else sh, sc[0:8] if sc.shape[0] > 1 else sc)
    prev = jnp.where(_tpos(tm, t_len) == 0, s0_ref[0], _shift_rows(h, hp, 1))
    dx = prev - h
    mu = mu_ref[...]
    xr, xw, xk, xv, xa, xg = [h + dx * mu[n:n + 1] for n in range(6)]
    r_ref[...] = _bdot(xr, wrkv_ref[0]).astype(BF16)
    k_ref[...] = _bdot(xk, wrkv_ref[1]).astype(BF16)
    v_ref[...] = _bdot(xv, wrkv_ref[2]).astype(BF16)
    w_log = -_softplus(-(w0_ref[...] + _bdot(jnp.tanh(_bdot(xw, w1_ref[...])), w2_ref[...]))) - 0.5
    w_ref[...] = jnp.exp(-jnp.exp(w_log))
    a_ref[...] = _sigmoid(a0_ref[...] + _bdot(_bdot(xa, a1_ref[...]), a2_ref[...])).astype(BF16)
    gg_ref[...] = _bdot(_sigmoid(_bdot(xg, g1_ref[...])), g2_ref[...]).astype(BF16)


def _rwkv_core_kernel(r_ref, w_ref, k_ref, v_ref, a_ref, kk_p, ka_p, rk_p, lnw_p, lnb_p, s0_ref,
                      y_ref, st_ref, state, kk_s, b_s, km_s, r_s):
    j = pl.program_id(1)
    tc = r_ref.shape[0]
    n = RW_N

    @pl.when(j == 0)
    def _():
        state[...] = s0_ref[...]

    def step(t, c):
        kt, at, vt, rt = [z[t].astype(F32) for z in (k_ref, a_ref, v_ref, r_ref)]
        r_s[...] = rt
        kk = kt * kk_p[...]
        kk = kk * lax.rsqrt(jnp.sum(kk * kk, axis=0, keepdims=True) + 1e-6)
        km = kt * (1.0 + (at - 1.0) * ka_p[...])
        kk_s[...] = kk
        b_s[...] = kk * at
        km_s[...] = km
        sa = jnp.zeros((n, LANES), F32)
        for kx in range(n):
            sa = sa + state[kx] * kk_s[pl.ds(kx, 1), :]
        y = jnp.zeros((n, LANES), F32)
        for kx in range(n):
            s_new = (state[kx] * w_ref[t, pl.ds(kx, 1), :] - sa * b_s[pl.ds(kx, 1), :]
                     + vt * km_s[pl.ds(kx, 1), :])
            state[kx] = s_new
            y = y + s_new * r_s[pl.ds(kx, 1), :]
        mean = jnp.mean(y, axis=0, keepdims=True)
        yc = y - mean
        var = jnp.mean(yc * yc, axis=0, keepdims=True)
        bonus = jnp.sum(rt * km * rk_p[...], axis=0, keepdims=True) * vt
        y_ref[t] = (yc * lax.rsqrt(var + RW_GN_EPS) * lnw_p[...] + lnb_p[...] + bonus).astype(BF16)
        return c

    lax.fori_loop(0, tc, step, 0)

    @pl.when(j == pl.num_programs(1) - 1)
    def _():
        st_ref[...] = state[...]


def _rwkv_layer(grp, x, norm_g, mods, shift0, wkv0, mu, w_rkv, w0, w1, w2, a0, a1, a2, g1, g2,
                k_k, k_a, r_k, ln_w, ln_b, w_o):
    b, t = grp.b, grp.t
    shift, scale, gate = mods
    sh, sh_spec = grp.rowmod(shift)
    sc, sc_spec = grp.rowmod(scale)
    if shift0 is None:
        s0 = jnp.zeros((b, 1, D), F32)
    else:
        s0 = jnp.concatenate([shift0[:, None, :], jnp.zeros((b, t - 1, D), F32)], axis=1)
    s0, s0_spec = grp.rowseq(s0)
    bf = lambda z: z.astype(BF16)
    row = lambda z: z.reshape(1, -1)
    weights = [mu, bf(w_rkv), row(w0), bf(w1), bf(w2), row(a0), bf(a1), bf(a2), bf(g1), bf(g2)]
    outs = pl.pallas_call(
        functools.partial(_rwkv_proj_kernel, t),
        out_shape=tuple(jax.ShapeDtypeStruct((grp.n, D), dt) for dt in (BF16, F32, BF16, BF16, BF16, BF16)),
        grid=(grp.tiles,),
        in_specs=[grp.rows(D), grp.prev8(D), _full((1, D)), sh_spec, sc_spec, s0_spec]
        + [_full(z.shape) for z in weights],
        out_specs=tuple(grp.rows(D) for _ in range(6)),
        compiler_params=_cparams(("arbitrary",)),
        name="rwkv_proj",
    )(x, x, row(norm_g), sh, sc, s0, *weights)
    r, w, k, v, a, gg = outs
    bh = b * RW_H

    def to_core(z):
        return z.reshape(b, t, RW_H, RW_N).transpose(1, 3, 0, 2).reshape(t, RW_N, bh)

    def ptile(p):
        return jnp.tile(p.reshape(RW_H, RW_N).T, (1, b))

    if wkv0 is None:
        st0 = jnp.zeros((RW_N, RW_N, bh), F32)
    else:
        st0 = wkv0.transpose(3, 2, 0, 1).reshape(RW_N, RW_N, bh)
    tc = min(t, 16)
    seq = pl.BlockSpec((tc, RW_N, LANES), lambda q, j: (j, 0, q))
    par = pl.BlockSpec((RW_N, LANES), lambda q, j: (0, q))
    stt = pl.BlockSpec((RW_N, RW_N, LANES), lambda q, j: (0, 0, q))
    y, st = pl.pallas_call(
        _rwkv_core_kernel,
        out_shape=(jax.ShapeDtypeStruct((t, RW_N, bh), BF16),
                   jax.ShapeDtypeStruct((RW_N, RW_N, bh), F32)),
        grid=(bh // LANES, t // tc),
        in_specs=[seq] * 5 + [par] * 5 + [stt],
        out_specs=(seq, stt),
        scratch_shapes=[pltpu.VMEM((RW_N, RW_N, LANES), F32)] + [pltpu.VMEM((RW_N, LANES), F32)] * 4,
        compiler_params=_cparams(("arbitrary", "arbitrary")),
        name="rwkv_core",
    )(to_core(r), to_core(w), to_core(k), to_core(v), to_core(a),
      ptile(k_k), ptile(k_a), ptile(r_k.reshape(-1)), ptile(ln_w), ptile(ln_b), st0)
    y_rows = y.reshape(t, RW_N, b, RW_H).transpose(2, 0, 3, 1).reshape(grp.n, D)
    x_new = _outproj_call(grp, x, y_rows, bf(w_o), gate, mul=gg)
    new_wkv = st.reshape(RW_N, RW_N, b, RW_H).transpose(2, 3, 1, 0)
    x_last = x.reshape(b, t, D)[:, -1]
    new_shift = _modrows_call(x_last, norm_g, shift, scale)
    return x_new, new_shift, new_wkv


def _pad_time(z, b, t, tp):
    if tp == t:
        return z
    w = z.shape[-1]
    return jnp.pad(z.reshape(b, t, w), ((0, 0), (0, tp - t), (0, 0))).reshape(b * tp, w)


def _unpad_time(z, b, t, tp):
    if tp == t:
        return z
    w = z.shape[-1]
    return z.reshape(b, tp, w)[:, :t].reshape(b * t, w)


def _gdn_proj_kernel(t_len, x_ref, xp_ref, g_ref, sh_ref, sc_ref, c1_ref, c2_ref, c3_ref, wqkv_ref,
                     wz_ref, wb_ref, wa_ref, cw_ref, alog_ref, dtb_ref,
                     qkv_ref, z_ref, beta_ref, gdec_ref):
    tm = x_ref.shape[0]
    g, sh, sc = g_ref[...], sh_ref[0], sc_ref[0]
    h = _modulate(x_ref[...], g, sh, sc)
    hp = _modulate(xp_ref[...], g, sh[0:8] if sh.shape[0] > 1 else sh, sc[0:8] if sc.shape[0] > 1 else sc)
    hb = h.astype(BF16)
    pre = jnp.dot(hb, wqkv_ref[...], preferred_element_type=F32)
    pre8 = _bdot(hp, wqkv_ref[...])
    tpos = _tpos(tm, t_len)
    cw = cw_ref[...]
    conv = pre * cw[3:4]
    for d, cref in ((1, c1_ref), (2, c2_ref), (3, c3_ref)):
        past = jnp.where(tpos >= d, _shift_rows(pre, pre8, d), cref[0])
        conv = conv + past * cw[3 - d:4 - d]
    act = _silu(conv)
    nh = GD_H
    for hh in range(2 * nh):
        sl = slice(hh * GD_DK, (hh + 1) * GD_DK)
        seg = act[:, sl]
        seg = seg * lax.rsqrt(jnp.sum(seg * seg, axis=-1, keepdims=True) + 1e-6)
        if hh < nh:
            seg = seg * (GD_DK ** -0.5)
        qkv_ref[:, sl] = seg.astype(BF16)
    qkv_ref[:, 2 * nh * GD_DK:] = act[:, 2 * nh * GD_DK:].astype(BF16)
    z_ref[...] = jnp.dot(hb, wz_ref[...], preferred_element_type=F32).astype(BF16)
    beta_ref[...] = _sigmoid(jnp.dot(hb, wb_ref[...], preferred_element_type=F32))
    a_logit = jnp.dot(hb, wa_ref[...], preferred_element_type=F32)
    gdec_ref[...] = -jnp.exp(alog_ref[...]) * _softplus(a_logit + dtb_ref[...])


def _unit_lower_inverse(a, eye, masks):
    blk8, offs = masks
    n = range(len(a))
    a8 = [jnp.where(blk8, a[i], 0.0) for i in n]
    x = [eye - a8[i] for i in n]
    y = [_bdot(a8[i], a8[i]) for i in n]
    x = [x[i] + _bdot(x[i], y[i]) for i in n]
    y = [_bdot(y[i], y[i]) for i in n]
    x = [x[i] + _bdot(x[i], y[i]) for i in n]
    for off in offs:
        t = [_bdot(jnp.where(off, a[i], 0.0), x[i]) for i in n]
        x = [x[i] - _bdot(x[i], t[i]) for i in n]
    return x


def _inverse_masks(c):
    ri = lax.broadcasted_iota(I32, (c, c), 0)
    ci = lax.broadcasted_iota(I32, (c, c), 1)
    sr = lambda z, s: lax.shift_right_logical(z, jnp.full(z.shape, s, I32))
    blk8 = sr(ri, 3) == sr(ci, 3)
    offs = []
    m, lg = 8, 3
    while m < c:
        same = sr(ri, lg + 1) == sr(ci, lg + 1)
        lower = jnp.logical_and((sr(ri, lg) & 1) == 1, (sr(ci, lg) & 1) == 0)
        offs.append(jnp.logical_and(same, lower))
        m, lg = m * 2, lg + 1
    return ri, ci, (blk8, offs)


def _gdn_core_kernel(q_ref, k_ref, v_ref, z_ref, beta_ref, g_ref, s0_ref, nw_ref, y_ref, st_ref, state):
    cidx = pl.program_id(1)
    c = q_ref.shape[0]

    @pl.when(cidx == 0)
    def _():
        state[...] = s0_ref[0]

    ri, ci, masks = _inverse_masks(c)
    incl = ri >= ci
    strict = ri > ci
    eye = (ri == ci).astype(F32)
    g = g_ref[...]
    cum = _fdot(incl.astype(F32), g)
    cum_t = lax.dot_general(g, (ci >= ri).astype(F32), (((0,), (0,)), ((), ())),
                            precision=HIGHEST, preferred_element_type=F32)
    beta = beta_ref[...]
    nw = nw_ref[...]
    heads = range(GD_H)
    sls = [slice(h * GD_DK, (h + 1) * GD_DK) for h in heads]
    q = [q_ref[:, sl] for sl in sls]
    k = [k_ref[:, sl] for sl in sls]
    kf = [z.astype(F32) for z in k]
    v = [v_ref[:, sl].astype(F32) for sl in sls]
    s = [state[h] for h in heads]
    cum_c = [cum[:, h:h + 1] for h in heads]
    dec = [jnp.where(incl, jnp.exp(jnp.where(incl, cum_c[h] - cum_t[h:h + 1, :], 0.0)), 0.0) for h in heads]
    bcol = [beta[:, h:h + 1] for h in heads]
    kb = [kf[h] * bcol[h] for h in heads]
    a = [jnp.where(strict, _bdot_nt(kb[h], k[h]) * dec[h], 0.0) for h in heads]
    attn = [_bdot_nt(q[h], k[h]) * dec[h] for h in heads]
    x = _unit_lower_inverse(a, eye, masks)
    ecum = [jnp.exp(cum_c[h]) for h in heads]
    sol = [_bdot(x[h], jnp.concatenate([v[h] * bcol[h], kb[h] * ecum[h]], axis=1)) for h in heads]
    u = [sol[h][:, :GD_DV] - _bdot(sol[h][:, GD_DV:], s[h]) for h in heads]
    o = [_bdot(q[h].astype(F32) * ecum[h], s[h]) + _bdot(attn[h], u[h]) for h in heads]
    last = [cum[c - 1:c, h:h + 1] for h in heads]
    s_new = [s[h] * jnp.exp(last[h]) + _bdot_tn(kf[h] * jnp.exp(last[h] - cum_c[h]), u[h]) for h in heads]
    for h in heads:
        state[h] = s_new[h]
        on = o[h] * lax.rsqrt(jnp.mean(o[h] * o[h], axis=-1, keepdims=True) + NORM_EPS) * nw
        y_ref[:, sls[h]] = (on * _silu(z_ref[:, sls[h]].astype(F32))).astype(BF16)

    @pl.when(cidx == pl.num_programs(1) - 1)
    def _():
        st_ref[0] = state[...]


def _gdn_layer(grp, x, norm_g, mods, conv0, ssm0, w_in, conv_w, a_log, dt_bias, norm_w, w_o, chunk):
    b, t = grp.b, grp.t
    shift, scale, gate = mods
    sh, sh_spec = grp.rowmod(shift)
    sc, sc_spec = grp.rowmod(scale)
    kd = GD_H * GD_DK
    cstates, cspecs = [], []
    for d in (1, 2, 3):
        if conv0 is None:
            cs = jnp.zeros((b, 1, GD_C), F32)
        else:
            cs = jnp.concatenate([conv0[:, 3 - d:, :], jnp.zeros((b, t - d, GD_C), F32)], axis=1)
        cs, spec = grp.rowseq(cs)
        cstates.append(cs)
        cspecs.append(spec)
    bf = lambda z: z.astype(BF16)
    pad128 = lambda z: jnp.pad(z, ((0, 0), (0, LANES - z.shape[1])))
    w_qkv = bf(w_in[:, :GD_C])
    w_z = bf(w_in[:, GD_C:GD_C + kd])
    w_b = bf(pad128(w_in[:, GD_C + kd:GD_C + kd + GD_H]))
    w_a = bf(pad128(w_in[:, GD_C + kd + GD_H:]))
    weights = [w_qkv, w_z, w_b, w_a, conv_w, pad128(a_log.reshape(1, GD_H)), pad128(dt_bias.reshape(1, GD_H))]
    qkv, z, beta, gdec = pl.pallas_call(
        functools.partial(_gdn_proj_kernel, t),
        out_shape=(jax.ShapeDtypeStruct((grp.n, GD_C), BF16), jax.ShapeDtypeStruct((grp.n, kd), BF16),
                   jax.ShapeDtypeStruct((grp.n, LANES), F32), jax.ShapeDtypeStruct((grp.n, LANES), F32)),
        grid=(grp.tiles,),
        in_specs=[grp.rows(D), grp.prev8(D), _full((1, D)), sh_spec, sc_spec] + cspecs
        + [_full(z_.shape) for z_ in weights],
        out_specs=(grp.rows(GD_C), grp.rows(kd), grp.rows(LANES), grp.rows(LANES)),
        compiler_params=_cparams(("arbitrary",)),
        name="gdn_proj",
    )(x, x, norm_g.reshape(1, D), sh, sc, *cstates, *weights)
    tp = ((t + chunk - 1) // chunk) * chunk
    nc = tp // chunk
    qkv_p, z_p = _pad_time(qkv, b, t, tp), _pad_time(z, b, t, tp)
    beta_p, g_p = _pad_time(beta, b, t, tp), _pad_time(gdec, b, t, tp)
    if ssm0 is None:
        ssm0 = jnp.zeros((b, GD_H, GD_DK, GD_DV), F32)
    col = lambda j: pl.BlockSpec((chunk, kd), lambda bi, c: (bi * nc + c, j))
    lan = pl.BlockSpec((chunk, LANES), lambda bi, c: (bi * nc + c, 0))
    stt = pl.BlockSpec((1, GD_H, GD_DK, GD_DV), lambda bi, c: (bi, 0, 0, 0))
    y, st = pl.pallas_call(
        _gdn_core_kernel,
        out_shape=(jax.ShapeDtypeStruct((b * tp, kd), BF16),
                   jax.ShapeDtypeStruct((b, GD_H, GD_DK, GD_DV), F32)),
        grid=(b, nc),
        in_specs=[col(0), col(1), col(2), col(0), lan, lan, stt, _full((1, GD_DV))],
        out_specs=(col(0), stt),
        scratch_shapes=[pltpu.VMEM((GD_H, GD_DK, GD_DV), F32)],
        compiler_params=_cparams(("arbitrary", "arbitrary")),
        name="gdn_core",
    )(qkv_p, qkv_p, qkv_p, z_p, beta_p, g_p, ssm0, norm_w.reshape(1, GD_DV))
    x_new = _outproj_call(grp, x, _unpad_time(y, b, t, tp), bf(w_o), gate)
    nl = min(t, GD_CONV - 1)
    x_last = x.reshape(b, t, D)[:, t - nl:].reshape(b * nl, D)
    rep = lambda m: jnp.repeat(m, nl, axis=0)
    pre_last = _modrows_call(x_last, norm_g, rep(shift), rep(scale), w_qkv).reshape(b, nl, GD_C)
    if nl < GD_CONV - 1:
        pre_last = jnp.concatenate([conv0[:, nl:], pre_last], axis=1)
    return x_new, pre_last, st


def _ret_proj_kernel(x_ref, g_ref, sh_ref, sc_ref, cos_ref, sin_ref, w_ref, q_ref, k_ref, v_ref, gate_ref):
    h = _modulate(x_ref[...], g_ref[...], sh_ref[0], sc_ref[0]).astype(BF16)
    kd = RT_H * RT_DK
    vd = RT_H * RT_DV
    cos, sin = cos_ref[0], sin_ref[0]
    even = (lax.broadcasted_iota(I32, (1, kd), 1) & 1) == 0

    def rotary(z):
        swapped = jnp.where(even, pltpu.roll(z, kd - 1, 1), pltpu.roll(z, 1, 1))
        return z * cos + swapped * sin

    q_ref[...] = rotary(jnp.dot(h, w_ref[:, 0:kd], preferred_element_type=F32)).astype(BF16)
    k = rotary(jnp.dot(h, w_ref[:, kd:2 * kd], preferred_element_type=F32))
    k_ref[...] = (k * (RT_DK ** -0.5)).astype(BF16)
    v_ref[...] = jnp.dot(h, w_ref[:, 2 * kd:2 * kd + vd], preferred_element_type=F32).astype(BF16)
    gate_ref[...] = jnp.dot(h, w_ref[:, 2 * kd + vd:], preferred_element_type=F32).astype(BF16)


def _ret_core_kernel(q_ref, k_ref, v_ref, gate_ref, dm_ref, qd_ref, kd_ref, cd_ref, s0_ref, nw_ref,
                     y_ref, st_ref, state):
    cidx = pl.program_id(1)

    @pl.when(cidx == 0)
    def _():
        state[...] = s0_ref[0]

    heads = range(RT_H)
    kss = [slice(h * RT_DK, (h + 1) * RT_DK) for h in heads]
    vss = [slice(h * RT_DV, (h + 1) * RT_DV) for h in heads]
    q = [q_ref[:, ks] for ks in kss]
    k = [k_ref[:, ks] for ks in kss]
    v = [v_ref[:, vs] for vs in vss]
    s = [state[h] for h in heads]
    inner = [_bdot_nt(q[h], k[h]) * dm_ref[h] for h in heads]
    cross = [_bdot(q[h], s[h]) * qd_ref[h] for h in heads]
    o = [_bdot(inner[h], v[h]) + cross[h] for h in heads]
    s_new = [s[h] * cd_ref[h] + _bdot_tn(k[h].astype(F32) * kd_ref[h], v[h]) for h in heads]
    for h in heads:
        state[h] = s_new[h]
        on = o[h] * lax.rsqrt(jnp.mean(o[h] * o[h], axis=-1, keepdims=True) + NORM_EPS) * nw_ref[:, vss[h]]
        y_ref[:, vss[h]] = (on * _silu(gate_ref[:, vss[h]].astype(F32))).astype(BF16)

    @pl.when(cidx == pl.num_programs(1) - 1)
    def _():
        st_ref[0] = state[...]


def _ret_layer(grp, x, norm_g, mods, s0, pos0, w_in, norm_w, w_o, chunk):
    b, t = grp.b, grp.t
    shift, scale, gate = mods
    sh, sh_spec = grp.rowmod(shift)
    sc, sc_spec = grp.rowmod(scale)
    kd, vd = RT_H * RT_DK, RT_H * RT_DV
    half = RT_DK // 2
    inv = 1.0 / (10000.0 ** jnp.linspace(0.0, 1.0, half, dtype=F32))
    pos = jnp.arange(t, dtype=F32) + float(pos0)
    ang = pos[:, None] * inv[None, :]
    cos = jnp.repeat(jnp.cos(ang), 2, axis=1)
    sin = jnp.stack([-jnp.sin(ang), jnp.sin(ang)], axis=-1).reshape(t, RT_DK)
    cos4, cos_spec = grp.postab(jnp.tile(cos, (1, RT_H)))
    sin4, sin_spec = grp.postab(jnp.tile(sin, (1, RT_H)))
    wb = w_in.astype(BF16)
    q, k, v, gt = pl.pallas_call(
        _ret_proj_kernel,
        out_shape=(jax.ShapeDtypeStruct((grp.n, kd), BF16), jax.ShapeDtypeStruct((grp.n, kd), BF16),
                   jax.ShapeDtypeStruct((grp.n, vd), BF16), jax.ShapeDtypeStruct((grp.n, vd), BF16)),
        grid=(grp.tiles,),
        in_specs=[grp.rows(D), _full((1, D)), sh_spec, sc_spec, cos_spec, sin_spec, _full(wb.shape)],
        out_specs=(grp.rows(kd), grp.rows(kd), grp.rows(vd), grp.rows(vd)),
        compiler_params=_cparams(("arbitrary",)),
        name="ret_proj",
    )(x, norm_g.reshape(1, D), sh, sc, cos4, sin4, wb)
    tp = ((t + chunk - 1) // chunk) * chunk
    nc = tp // chunk
    nv = min(t, chunk)
    assert tp == t or nc == 1
    log_gamma = jnp.log1p(-jnp.exp2(-5.0 - jnp.arange(RT_H, dtype=F32)))
    idx = jnp.arange(chunk, dtype=F32)
    diff = idx[:, None] - idx[None, :]
    dmask = jnp.where(diff >= 0, jnp.exp(log_gamma[:, None, None] * jnp.maximum(diff, 0.0)), 0.0)
    q_dec = jnp.exp(log_gamma[:, None] * (idx + 1.0))[:, :, None]
    k_dec = jnp.exp(log_gamma[:, None] * jnp.maximum(nv - 1.0 - idx, 0.0))[:, :, None]
    c_dec = jnp.exp(log_gamma * nv)[:, None, None]
    if s0 is None:
        s0 = jnp.zeros((b, RT_H, RT_DK, RT_DV), F32)
    rowk = pl.BlockSpec((chunk, kd), lambda bi, c: (bi * nc + c, 0))
    rowv = pl.BlockSpec((chunk, vd), lambda bi, c: (bi * nc + c, 0))
    stt = pl.BlockSpec((1, RT_H, RT_DK, RT_DV), lambda bi, c: (bi, 0, 0, 0))
    y, st = pl.pallas_call(
        _ret_core_kernel,
        out_shape=(jax.ShapeDtypeStruct((b * tp, vd), BF16),
                   jax.ShapeDtypeStruct((b, RT_H, RT_DK, RT_DV), F32)),
        grid=(b, nc),
        in_specs=[rowk, rowk, rowv, rowv, _full(dmask.shape), _full(q_dec.shape), _full(k_dec.shape),
                  _full(c_dec.shape), stt, _full((1, vd))],
        out_specs=(rowv, stt),
        scratch_shapes=[pltpu.VMEM((RT_H, RT_DK, RT_DV), F32)],
        compiler_params=_cparams(("arbitrary", "arbitrary")),
        name="ret_core",
    )(_pad_time(q, b, t, tp), _pad_time(k, b, t, tp), _pad_time(v, b, t, tp), _pad_time(gt, b, t, tp),
      dmask, q_dec, k_dec, c_dec, s0, norm_w.reshape(1, vd))
    x_new = _outproj_call(grp, x, _unpad_time(y, b, t, tp), w_o.astype(BF16), gate)
    return x_new, st


def _hgrn_proj_kernel(layer, x_ref, g_ref, sh_ref, sc_ref, lbl_ref, w_ref,
                      q_ref, k_ref, lf_ref, v_ref, gate_ref):
    h = _modulate(x_ref[...], g_ref[...], sh_ref[0], sc_ref[0]).astype(BF16)
    ed = HG_H * HG_E
    logits = lbl_ref[...]
    e = jnp.exp(logits - jnp.max(logits, axis=0, keepdims=True))
    lrow = lax.broadcasted_iota(I32, logits.shape, 0)
    part = jnp.where(jnp.logical_and(lrow >= 1, lrow <= layer), e, 0.0)
    lb = jnp.sum(part, axis=0, keepdims=True) / jnp.sum(e, axis=0, keepdims=True)
    q_ref[...] = jnp.dot(h, w_ref[:, 0:ed], preferred_element_type=F32)
    f = lb + (1.0 - lb) * _sigmoid(jnp.dot(h, w_ref[:, ed:2 * ed], preferred_element_type=F32))
    k_ref[...] = 1.0 - f
    lf_ref[...] = jnp.log(f)
    v_ref[...] = jnp.dot(h, w_ref[:, 2 * ed:3 * ed], preferred_element_type=F32).astype(BF16)
    gate_ref[...] = jnp.dot(h, w_ref[:, 3 * ed:], preferred_element_type=F32).astype(BF16)


def _hgrn_core_kernel(q_ref, k_ref, lf_ref, v_ref, gate_ref, s0_ref, nw_ref, y_ref, st_ref, state):
    cidx = pl.program_id(1)
    c = q_ref.shape[0]

    @pl.when(cidx == 0)
    def _():
        state[...] = s0_ref[0]

    ri = lax.broadcasted_iota(I32, (c, c), 0)
    ci = lax.broadcasted_iota(I32, (c, c), 1)
    ltri = (ri >= ci).astype(F32)
    row8 = lax.broadcasted_iota(I32, (8, 1), 0)
    heads = range(HG_H)
    sls = [slice(h * HG_E, (h + 1) * HG_E) for h in heads]
    q = [q_ref[:, sl] for sl in sls]
    k = [k_ref[:, sl] for sl in sls]
    v = [v_ref[:, sl].astype(F32) for sl in sls]
    st = [state[h] for h in heads]
    cum = [_fdot(ltri, lf_ref[:, sl]) for sl in sls]
    inter = [_bdot_nt(q[h] * jnp.exp(cum[h]), st[h]) for h in heads]
    last = [cum[h][c - 1:c, :] for h in heads]
    s_new = [st[h] * jnp.exp(last[h]) + _bdot_tn(v[h], k[h] * jnp.exp(last[h] - cum[h])) for h in heads]
    for h in heads:
        state[h] = s_new[h]
        parts = []
        for g0 in range(0, c, 8):
            qg, cg = q[h][g0:g0 + 8], cum[h][g0:g0 + 8]
            acc = inter[h][g0:g0 + 8]
            for j in range(g0 + 8):
                diff = cg - cum[h][j:j + 1, :]
                if j >= g0:
                    causal = row8 >= (j - g0)
                    diff = jnp.where(causal, diff, 0.0)
                col = jnp.sum(qg * k[h][j:j + 1, :] * jnp.exp(diff), axis=-1, keepdims=True)
                if j >= g0:
                    col = jnp.where(causal, col, 0.0)
                acc = acc + col * v[h][j:j + 1, :]
            parts.append(acc)
        o = parts[0] if len(parts) == 1 else jnp.concatenate(parts, axis=0)
        on = o * lax.rsqrt(jnp.mean(o * o, axis=-1, keepdims=True) + NORM_EPS) * nw_ref[:, sls[h]]
        y_ref[:, sls[h]] = (on * _silu(gate_ref[:, sls[h]].astype(F32))).astype(BF16)

    @pl.when(cidx == pl.num_programs(1) - 1)
    def _():
        st_ref[0] = state[...]


def _hgrn_layer(grp, x, norm_g, mods, s0, layer, lb_logits, w_in, norm_w, w_o, chunk):
    b, t = grp.b, grp.t
    shift, scale, gate = mods
    sh, sh_spec = grp.rowmod(shift)
    sc, sc_spec = grp.rowmod(scale)
    ed, vd = HG_H * HG_E, HG_H * HG_DV
    wb = w_in.astype(BF16)
    q, k, lf, v, gt = pl.pallas_call(
        functools.partial(_hgrn_proj_kernel, layer),
        out_shape=(jax.ShapeDtypeStruct((grp.n, ed), F32), jax.ShapeDtypeStruct((grp.n, ed), F32),
                   jax.ShapeDtypeStruct((grp.n, ed), F32), jax.ShapeDtypeStruct((grp.n, vd), BF16),
                   jax.ShapeDtypeStruct((grp.n, vd), BF16)),
        grid=(grp.tiles,),
        in_specs=[grp.rows(D), _full((1, D)), sh_spec, sc_spec, _full(lb_logits.shape), _full(wb.shape)],
        out_specs=(grp.rows(ed), grp.rows(ed), grp.rows(ed), grp.rows(vd), grp.rows(vd)),
        compiler_params=_cparams(("arbitrary",)),
        name="hgrn_proj",
    )(x, norm_g.reshape(1, D), sh, sc, lb_logits, wb)
    tp = ((t + chunk - 1) // chunk) * chunk
    nc = tp // chunk
    if s0 is None:
        s0 = jnp.zeros((b, HG_H, HG_E, HG_DV), F32)
    row = pl.BlockSpec((chunk, ed), lambda bi, c: (bi * nc + c, 0))
    stt = pl.BlockSpec((1, HG_H, HG_E, HG_DV), lambda bi, c: (bi, 0, 0, 0))
    y, st = pl.pallas_call(
        _hgrn_core_kernel,
        out_shape=(jax.ShapeDtypeStruct((b * tp, vd), BF16),
                   jax.ShapeDtypeStruct((b, HG_H, HG_E, HG_DV), F32)),
        grid=(b, nc),
        in_specs=[row, row, row, row, row, stt, _full((1, vd))],
        out_specs=(row, stt),
        scratch_shapes=[pltpu.VMEM((HG_H, HG_E, HG_DV), F32)],
        compiler_params=_cparams(("arbitrary", "arbitrary")),
        name="hgrn_core",
    )(_pad_time(q, b, t, tp), _pad_time(k, b, t, tp), _pad_time(lf, b, t, tp), _pad_time(v, b, t, tp),
      _pad_time(gt, b, t, tp), jnp.swapaxes(s0, 2, 3), norm_w.reshape(1, vd))
    x_new = _outproj_call(grp, x, _unpad_time(y, b, t, tp), w_o.astype(BF16), gate)
    return x_new, jnp.swapaxes(st, 2, 3)


ROW_TILE = 256
MOE_TILE = 512
GDN_CHUNK, RET_CHUNK, HGRN_CHUNK = 64, 128, 16
SAMPLE_CHUNK = 16
PAST_LEN = 16384


def kernel(x_prompt, x_sample, c_prompt, c_sample, state_rwkv_wkv, state_rwkv_shift, state_gdn_ssm, state_gdn_conv, state_ret, state_hgrn, ada_w, ada_b, norm_mix, norm_ffn, norm_final, rwkv_mu, rwkv_w_rkv, rwkv_w0, rwkv_w1, rwkv_w2, rwkv_a0, rwkv_a1, rwkv_a2, rwkv_g1, rwkv_g2, rwkv_k_k, rwkv_k_a, rwkv_r_k, rwkv_ln_w, rwkv_ln_b, rwkv_w_o, gdn_w_in, gdn_conv_w, gdn_a_log, gdn_dt_bias, gdn_norm_w, gdn_w_o, ret_w_in, ret_norm_w, ret_w_o, hgrn_w_in, hgrn_lb_logits, hgrn_norm_w, hgrn_w_o, moe_w_router, moe_b_router, moe_w_gu, moe_b_gu, moe_w_down, moe_b_down):
    bp, tp, _ = x_prompt.shape
    bs, ts, _ = x_sample.shape
    gp, gs = _Group(bp, tp, ROW_TILE), _Group(bs, ts, ROW_TILE)
    mp, msg = _Group(bp, tp, MOE_TILE), _Group(bs, ts, MOE_TILE)
    ada = _ada_call(jnp.concatenate([c_prompt, c_sample], axis=0), ada_w, ada_b)
    xp = x_prompt.reshape(bp * tp, D)
    xs = x_sample.reshape(bs * ts, D)
    outs_p = {k: [] for k in ("wkv", "shift", "ssm", "conv", "ret", "hgrn")}
    outs_s = {k: [] for k in ("wkv", "shift", "ssm", "conv", "ret", "hgrn")}
    for i in range(DEPTH):
        kind, j = i % 4, i // 4
        m = ada[i].reshape(bp + bs, 6, D)
        mod_p = [m[:bp, n] for n in range(6)]
        mod_s = [m[bp:, n] for n in range(6)]
        g = norm_mix[i]
        if kind == 0:
            prm = (rwkv_mu[j], rwkv_w_rkv[j], rwkv_w0[j], rwkv_w1[j], rwkv_w2[j], rwkv_a0[j], rwkv_a1[j],
                   rwkv_a2[j], rwkv_g1[j], rwkv_g2[j], rwkv_k_k[j], rwkv_k_a[j], rwkv_r_k[j],
                   rwkv_ln_w[j], rwkv_ln_b[j], rwkv_w_o[j])
            xp, sh_p, wkv_p = _rwkv_layer(gp, xp, g, mod_p[:3], None, None, *prm)
            xs, sh_s, wkv_s = _rwkv_layer(gs, xs, g, mod_s[:3], state_rwkv_shift[j], state_rwkv_wkv[j], *prm)
            outs_p["wkv"].append(wkv_p); outs_p["shift"].append(sh_p)
            outs_s["wkv"].append(wkv_s); outs_s["shift"].append(sh_s)
        elif kind == 1:
            prm = (gdn_w_in[j], gdn_conv_w[j], gdn_a_log[j], gdn_dt_bias[j], gdn_norm_w[j], gdn_w_o[j])
            xp, cv_p, ss_p = _gdn_layer(gp, xp, g, mod_p[:3], None, None, *prm, GDN_CHUNK)
            xs, cv_s, ss_s = _gdn_layer(gs, xs, g, mod_s[:3], state_gdn_conv[j], state_gdn_ssm[j], *prm,
                                        SAMPLE_CHUNK)
            outs_p["ssm"].append(ss_p); outs_p["conv"].append(cv_p)
            outs_s["ssm"].append(ss_s); outs_s["conv"].append(cv_s)
        elif kind == 2:
            prm = (ret_w_in[j], ret_norm_w[j], ret_w_o[j])
            xp, r_p = _ret_layer(gp, xp, g, mod_p[:3], None, 0, *prm, RET_CHUNK)
            xs, r_s = _ret_layer(gs, xs, g, mod_s[:3], state_ret[j], PAST_LEN, *prm, SAMPLE_CHUNK)
            outs_p["ret"].append(r_p); outs_s["ret"].append(r_s)
        else:
            prm = (i, hgrn_lb_logits, hgrn_w_in[j], hgrn_norm_w[j], hgrn_w_o[j])
            xp, h_p = _hgrn_layer(gp, xp, g, mod_p[:3], None, *prm, HGRN_CHUNK)
            xs, h_s = _hgrn_layer(gs, xs, g, mod_s[:3], state_hgrn[j], *prm, SAMPLE_CHUNK)
            outs_p["hgrn"].append(h_p); outs_s["hgrn"].append(h_s)
        last = i == DEPTH - 1
        xp, xs = _moe(i, [mp, msg], [xp, xs], norm_ffn[i], [mod_p[3:], mod_s[3:]], moe_w_router[i],
                      moe_b_router[i], moe_w_gu, moe_b_gu, moe_w_down, moe_b_down,
                      norm_final if last else None)
    y_prompt = xp[1].reshape(bp, tp, D)
    y_sample = xs[1].reshape(bs, ts, D)
    order = ("wkv", "shift", "ssm", "conv", "ret", "hgrn")
    return ((y_prompt, y_sample) + tuple(jnp.stack(outs_p[k]) for k in order)
            + tuple(jnp.stack(outs_s[k]) for k in order))
```

```python
import functools
import math

import jax
import jax.numpy as jnp
from jax import lax
from jax.experimental import pallas as pl
from jax.experimental.pallas import tpu as pltpu

F32 = jnp.float32
BF16 = jnp.bfloat16
I32 = jnp.int32
HIGHEST = lax.Precision.HIGHEST

D = 1024
DEPTH = 4
NORM_EPS = 1e-6
RW_H, RW_N = 16, 64
RW_GN_EPS = 64e-5
GD_H, GD_DK, GD_DV, GD_CONV = 8, 128, 128, 4
GD_C = 3 * GD_H * GD_DK
RT_H, RT_DK, RT_DV = 4, 256, 512
HG_H, HG_E, HG_DV = 8, 128, 128
N_EXPERTS, TOP_K, D_FF = 32, 4, 1024
SWIGLU_LIMIT, SWIGLU_ALPHA = 7.0, 1.702

LANES = 128
EXPERT_TILE = 512
TOKEN_BLOCK = 256
VMEM_LIMIT = 56 * 1024 * 1024


def _cparams(sem, vmem=VMEM_LIMIT):
    return pltpu.CompilerParams(dimension_semantics=sem, vmem_limit_bytes=vmem)


def _sigmoid(x):
    return 1.0 / (1.0 + jnp.exp(-x))


def _silu(x):
    return x * _sigmoid(x)


def _softplus(x):
    return jnp.maximum(x, 0.0) + jnp.log(1.0 + jnp.exp(-jnp.abs(x)))


def _modulate(x, g, shift, scale):
    ms = jnp.mean(x * x, axis=-1, keepdims=True)
    return (x * lax.rsqrt(ms + NORM_EPS) * g) * (1.0 + scale) + shift


def _bdot(a, b):
    return jnp.dot(a.astype(BF16), b.astype(BF16), preferred_element_type=F32)


def _bdot_nt(a, b):
    return lax.dot_general(a.astype(BF16), b.astype(BF16), (((1,), (1,)), ((), ())),
                           preferred_element_type=F32)


def _bdot_tn(a, b):
    return lax.dot_general(a.astype(BF16), b.astype(BF16), (((0,), (0,)), ((), ())),
                           preferred_element_type=F32)


def _fdot(a, b):
    return jnp.dot(a, b, precision=HIGHEST, preferred_element_type=F32)


class _Group:
    def __init__(self, b, t, tm):
        self.b, self.t, self.n = b, t, b * t
        self.tm = min(tm, self.n)
        assert self.n % self.tm == 0
        assert (self.t % self.tm == 0) or (self.tm % self.t == 0)
        self.per_batch = self.t % self.tm == 0
        self.tiles = self.n // self.tm

    def rows(self, width):
        return pl.BlockSpec((self.tm, width), lambda i: (i, 0))

    def rowmod(self, arr):
        w = arr.shape[-1]
        if self.per_batch:
            k = self.t // self.tm
            return arr.reshape(self.b, 1, w), pl.BlockSpec((1, 1, w), lambda i: (i // k, 0, 0))
        rep = jnp.repeat(arr, self.t, axis=0).reshape(self.tiles, self.tm, w)
        return rep, pl.BlockSpec((1, self.tm, w), lambda i: (i, 0, 0))

    def rowseq(self, arr):
        w = arr.shape[-1]
        if self.per_batch:
            assert arr.shape[1] == 1
            k = self.t // self.tm
            return arr, pl.BlockSpec((1, 1, w), lambda i: (i // k, 0, 0))
        return arr.reshape(self.tiles, self.tm, w), pl.BlockSpec((1, self.tm, w), lambda i: (i, 0, 0))

    def postab(self, tab):
        w = tab.shape[-1]
        if self.per_batch:
            k = self.t // self.tm
            return tab.reshape(k, self.tm, w), pl.BlockSpec((1, self.tm, w), lambda i: (i % k, 0, 0))
        rep = jnp.tile(tab, (self.tm // self.t, 1)).reshape(1, self.tm, w)
        return rep, pl.BlockSpec((1, self.tm, w), lambda i: (0, 0, 0))

    def prev8(self, width):
        k = self.tm // 8
        return pl.BlockSpec((8, width), lambda i: (jnp.maximum(i * k - 1, 0), 0))


def _full(shape):
    nd = len(shape)
    return pl.BlockSpec(shape, lambda *a: (0,) * nd)


def _tpos(tm, t):
    row = pl.program_id(0) * tm + lax.broadcasted_iota(I32, (tm, 1), 0)
    return row % t


def _stage_rows(scr, cur, prev8):
    scr[0:8, :] = prev8
    scr[8:, :] = cur


def _shifted_rows(scr, d):
    return scr[pl.ds(8 - d, scr.shape[0] - 8), :]


def _ada_kernel(c_ref, w_ref, b_ref, o_ref):
    o_ref[0] = _bdot(_silu(c_ref[...]), w_ref[0]) + b_ref[0]


def _ada_call(c_all, ada_w, ada_b):
    nb = c_all.shape[0]
    tn = 1536
    return pl.pallas_call(
        _ada_kernel,
        out_shape=jax.ShapeDtypeStruct((DEPTH, nb, 6 * D), F32),
        grid=(DEPTH, 6 * D // tn),
        in_specs=[pl.BlockSpec((nb, D), lambda l, j: (0, 0)),
                  pl.BlockSpec((1, D, tn), lambda l, j: (l, 0, j)),
                  pl.BlockSpec((1, 1, tn), lambda l, j: (l, 0, j))],
        out_specs=pl.BlockSpec((1, nb, tn), lambda l, j: (l, 0, j)),
        compiler_params=_cparams(("arbitrary", "arbitrary")),
        name="adaln",
    )(c_all, ada_w, ada_b.reshape(DEPTH, 1, 6 * D))


def _modrows_kernel(x_ref, g_ref, sh_ref, sc_ref, *rest):
    h = _modulate(x_ref[...], g_ref[...], sh_ref[...], sc_ref[...])
    if len(rest) == 2:
        w_ref, o_ref = rest
        o_ref[...] = _bdot(h, w_ref[...])
    else:
        rest[0][...] = h


def _modrows_call(x, g, shift, scale, w=None):
    n = x.shape[0]
    args = [x, g.reshape(1, D), shift, scale]
    specs = [_full((n, D)), _full((1, D)), _full((n, D)), _full((n, D))]
    width = D
    if w is not None:
        args.append(w)
        specs.append(_full(w.shape))
        width = w.shape[1]
    return pl.pallas_call(
        _modrows_kernel,
        out_shape=jax.ShapeDtypeStruct((n, width), F32),
        grid=(1,),
        in_specs=specs,
        out_specs=_full((n, width)),
        compiler_params=_cparams(("arbitrary",)),
        name="modrows",
    )(*args)


def _outproj_kernel(has_mul, x_ref, y_ref, *rest):
    if has_mul:
        m_ref, w_ref, gt_ref, o_ref = rest
        y = y_ref[...].astype(F32) * m_ref[...].astype(F32)
    else:
        w_ref, gt_ref, o_ref = rest
        y = y_ref[...]
    o_ref[...] = x_ref[...] + gt_ref[0] * _bdot(y, w_ref[...])


def _outproj_call(grp, x, y, w_o, gate, mul=None):
    dy = y.shape[1]
    gt, gt_spec = grp.rowmod(gate)
    args = [x, y]
    specs = [grp.rows(D), grp.rows(dy)]
    if mul is not None:
        args.append(mul)
        specs.append(grp.rows(dy))
    args += [w_o, gt]
    specs += [_full(w_o.shape), gt_spec]
    return pl.pallas_call(
        functools.partial(_outproj_kernel, mul is not None),
        out_shape=jax.ShapeDtypeStruct((grp.n, D), F32),
        grid=(grp.tiles,),
        in_specs=specs,
        out_specs=grp.rows(D),
        compiler_params=_cparams(("arbitrary",)),
        name="outproj",
    )(*args)


def _router_kernel(cin_ref, x_ref, g_ref, sh_ref, sc_ref, wr_ref, br_ref,
                   h_ref, idx_ref, gate_ref, rank_ref, cnt_ref, carry):
    i = pl.program_id(0)

    @pl.when(i == 0)
    def _():
        carry[...] = cin_ref[...]

    tm = x_ref.shape[0]
    h = _modulate(x_ref[...], g_ref[...], sh_ref[0], sc_ref[0])
    h_ref[...] = h
    logits = _fdot(h, wr_ref[...]) + br_ref[...]
    lane = lax.broadcasted_iota(I32, logits.shape, 1)
    work = logits
    sel = jnp.zeros(logits.shape, jnp.bool_)
    picks, vals = [], []
    for _ in range(TOP_K):
        m = jnp.max(work, axis=-1, keepdims=True)
        idx = jnp.min(jnp.where(work == m, lane, N_EXPERTS), axis=-1, keepdims=True)
        pick = lane == idx
        picks.append((idx, pick))
        vals.append(m)
        sel = jnp.logical_or(sel, pick)
        work = jnp.where(pick, -jnp.inf, work)
    es = [jnp.exp(v - vals[0]) for v in vals]
    denom = es[0] + es[1] + es[2] + es[3]
    self_f = sel.astype(F32)
    tri = (lax.broadcasted_iota(I32, (tm, tm), 0) > lax.broadcasted_iota(I32, (tm, tm), 1))
    local = jnp.dot(tri.astype(BF16), self_f.astype(BF16), preferred_element_type=F32)
    rank = local + carry[...]
    carry[...] = carry[...] + jnp.sum(self_f, axis=0, keepdims=True)
    cnt_ref[...] = carry[...]
    lane_o = lax.broadcasted_iota(I32, (tm, LANES), 1)
    idx_o = jnp.zeros((tm, LANES), I32)
    gate_o = jnp.zeros((tm, LANES), F32)
    rank_o = jnp.zeros((tm, LANES), I32)
    for k in range(TOP_K):
        idx, pick = picks[k]
        rk = jnp.sum(jnp.where(pick, rank, 0.0), axis=-1, keepdims=True)
        idx_o = jnp.where(lane_o == k, idx, idx_o)
        gate_o = jnp.where(lane_o == k, es[k] / denom, gate_o)
        rank_o = jnp.where(lane_o == k, rk.astype(I32), rank_o)
    idx_ref[...] = idx_o
    gate_ref[...] = gate_o
    rank_ref[...] = rank_o


def _router_call(grp, counts_in, x, g, shift, scale, w_router, b_router):
    sh, sh_spec = grp.rowmod(shift)
    sc, sc_spec = grp.rowmod(scale)
    pad = pl.BlockSpec((grp.tm, LANES), lambda i: (i, 0))
    return pl.pallas_call(
        _router_kernel,
        out_shape=(jax.ShapeDtypeStruct((grp.n, D), F32),
                   jax.ShapeDtypeStruct((grp.n, LANES), I32),
                   jax.ShapeDtypeStruct((grp.n, LANES), F32),
                   jax.ShapeDtypeStruct((grp.n, LANES), I32),
                   jax.ShapeDtypeStruct((1, N_EXPERTS), F32)),
        grid=(grp.tiles,),
        in_specs=[_full((1, N_EXPERTS)), grp.rows(D), _full((1, D)), sh_spec, sc_spec,
                  _full((D, N_EXPERTS)), _full((1, N_EXPERTS))],
        out_specs=(grp.rows(D), pad, pad, pad, _full((1, N_EXPERTS))),
        scratch_shapes=[pltpu.VMEM((1, N_EXPERTS), F32)],
        compiler_params=_cparams(("arbitrary",)),
        name="moe_router",
    )(counts_in, x, g.reshape(1, D), sh, sc, w_router, b_router.reshape(1, N_EXPERTS))


def _dispatch_kernel(cnt_ref, off_ref, nv_ref, pos_ref, h_ref, xs_ref, zbuf, sem, sem_z):
    i = pl.program_id(0)

    for t in range(TOKEN_BLOCK):
        for k in range(TOP_K):
            pltpu.make_async_copy(h_ref.at[pl.ds(t, 1)], xs_ref.at[pl.ds(pos_ref[t * TOP_K + k], 1)],
                                  sem).start(priority=k % 2)
    for k in range(TOP_K):
        pltpu.make_async_copy(h_ref, xs_ref.at[pl.ds(0, TOKEN_BLOCK)], sem).wait()

    @pl.when(i == pl.num_programs(0) - 1)
    def _():
        zbuf[...] = jnp.zeros(zbuf.shape, F32)
        bits = [1 << s for s in range(EXPERT_TILE.bit_length() - 2, 2, -1)]

        def pad_copies(e, wait):
            n = cnt_ref[e]
            start = off_ref[e] + n
            end = off_ref[e] + ((n + EXPERT_TILE - 1) // EXPERT_TILE) * EXPERT_TILE
            head = (-start) & 7

            def one(r, c):
                cp = pltpu.make_async_copy(zbuf.at[pl.ds(0, 1)], xs_ref.at[pl.ds(start + r, 1)], sem_z)
                if wait:
                    cp.wait()
                else:
                    cp.start()
                return c

            lax.fori_loop(0, head, one, 0)
            start8 = start + head
            rem = end - start8
            for bit in bits:
                @pl.when((rem & bit) != 0)
                def _():
                    s = pl.multiple_of(start8 + (rem & ~(2 * bit - 1)), 8)
                    cp = pltpu.make_async_copy(zbuf.at[pl.ds(0, bit)], xs_ref.at[pl.ds(s, bit)], sem_z)
                    if wait:
                        cp.wait()
                    else:
                        cp.start()

        def tail_copy(j):
            return pltpu.make_async_copy(zbuf, xs_ref.at[pl.ds(j * EXPERT_TILE, EXPERT_TILE)], sem_z)

        n_tiles = xs_ref.shape[0] // EXPERT_TILE
        for wait in (False, True):
            def per_expert(e, c):
                pad_copies(e, wait)
                return c

            def per_tail(j, c):
                if wait:
                    tail_copy(j).wait()
                else:
                    tail_copy(j).start()
                return c

            lax.fori_loop(0, N_EXPERTS, per_expert, 0)
            lax.fori_loop(nv_ref[0], n_tiles, per_tail, 0)


def _dispatch_call(counts, offsets, n_valid, pos_flat, h, n_rows):
    n = h.shape[0]
    return pl.pallas_call(
        _dispatch_kernel,
        out_shape=jax.ShapeDtypeStruct((n_rows, D), F32),
        grid_spec=pltpu.PrefetchScalarGridSpec(
            num_scalar_prefetch=3,
            grid=(n // TOKEN_BLOCK,),
            in_specs=[pl.BlockSpec((TOKEN_BLOCK * TOP_K,), lambda i, c, o, v: (i,), memory_space=pltpu.SMEM),
                      pl.BlockSpec((TOKEN_BLOCK, D), lambda i, c, o, v: (i, 0))],
            out_specs=pl.BlockSpec(memory_space=pl.ANY),
            scratch_shapes=[pltpu.VMEM((EXPERT_TILE, D), F32), pltpu.SemaphoreType.DMA(()),
                            pltpu.SemaphoreType.DMA(())]),
        compiler_params=_cparams(("arbitrary",)),
        name="moe_dispatch",
    )(counts, offsets, n_valid, pos_flat, h)


def _expert_kernel(te_ref, nv_ref, x_ref, wgu_ref, bgu_ref, wd_ref, bd_ref, o_ref, wgu_s, wd_s):
    j = pl.program_id(0)
    fresh = jnp.logical_or(j == 0, te_ref[j] != te_ref[jnp.maximum(j - 1, 0)])

    @pl.when(jnp.logical_and(j < nv_ref[0], fresh))
    def _():
        wgu_s[...] = wgu_ref[0, 0].astype(BF16)
        wd_s[...] = wd_ref[0, 0].astype(BF16)

    @pl.when(j < nv_ref[0])
    def _():
        x = x_ref[...].astype(BF16)
        gu = jnp.dot(x, wgu_s[...], preferred_element_type=F32) + bgu_ref[0, 0]
        gl = jnp.minimum(gu[:, :D_FF], SWIGLU_LIMIT)
        up = jnp.clip(gu[:, D_FF:], -SWIGLU_LIMIT, SWIGLU_LIMIT)
        act = (up + 1.0) * gl * _sigmoid(SWIGLU_ALPHA * gl)
        o_ref[...] = jnp.dot(act.astype(BF16), wd_s[...], preferred_element_type=F32) + bd_ref[0, 0]

    @pl.when(j >= nv_ref[0])
    def _():
        o_ref[...] = jnp.zeros(o_ref.shape, F32)


def _expert_call(layer, tile_expert, n_valid, xs, w_gu, b_gu, w_down, b_down):
    n_rows = xs.shape[0]
    g = n_rows // EXPERT_TILE
    return pl.pallas_call(
        _expert_kernel,
        out_shape=jax.ShapeDtypeStruct((n_rows, D), F32),
        grid_spec=pltpu.PrefetchScalarGridSpec(
            num_scalar_prefetch=2,
            grid=(g,),
            in_specs=[pl.BlockSpec((EXPERT_TILE, D), lambda j, te, nv: (jnp.minimum(j, nv[0] - 1), 0)),
                      pl.BlockSpec((1, 1, D, 2 * D_FF), lambda j, te, nv: (layer, te[j], 0, 0)),
                      pl.BlockSpec((1, 1, 1, 2 * D_FF), lambda j, te, nv: (layer, te[j], 0, 0)),
                      pl.BlockSpec((1, 1, D_FF, D), lambda j, te, nv: (layer, te[j], 0, 0)),
                      pl.BlockSpec((1, 1, 1, D), lambda j, te, nv: (layer, te[j], 0, 0))],
            out_specs=pl.BlockSpec((EXPERT_TILE, D), lambda j, te, nv: (j, 0)),
            scratch_shapes=[pltpu.VMEM((D, 2 * D_FF), BF16), pltpu.VMEM((D_FF, D), BF16)]),
        compiler_params=_cparams(("arbitrary",)),
        name="moe_experts",
    )(tile_expert, n_valid, xs, w_gu, b_gu.reshape(-1, N_EXPERTS, 1, 2 * D_FF), w_down,
      b_down.reshape(-1, N_EXPERTS, 1, D))


def _combine_kernel(with_norm, pos_ref, x_ref, gate_ref, gt_ref, ys_ref, *rest):
    if with_norm:
        ng_ref, o_ref, y_ref, buf, sems = rest
    else:
        o_ref, buf, sems = rest
    half = TOKEN_BLOCK // 2
    for hf in range(2):
        for t in range(hf * half, (hf + 1) * half):
            for k in range(TOP_K):
                pltpu.make_async_copy(ys_ref.at[pl.ds(pos_ref[t * TOP_K + k], 1)], buf.at[k, pl.ds(t, 1)],
                                      sems.at[hf]).start(priority=k % 2)
    for hf in range(2):
        rows = pl.ds(hf * half, half)
        for k in range(TOP_K):
            pltpu.make_async_copy(ys_ref.at[pl.ds(0, half)], buf.at[k, rows], sems.at[hf]).wait()
        gate = gate_ref[rows, :]
        f = gate[:, 0:1] * buf[0, rows]
        for k in range(1, TOP_K):
            f = f + gate[:, k:k + 1] * buf[k, rows]
        gt = gt_ref[0]
        x_new = x_ref[rows, :] + (gt if gt.shape[0] == 1 else gt[hf * half:(hf + 1) * half]) * f
        o_ref[rows, :] = x_new
        if with_norm:
            ms = jnp.mean(x_new * x_new, axis=-1, keepdims=True)
            y_ref[rows, :] = x_new * lax.rsqrt(ms + NORM_EPS) * ng_ref[...]


def _combine_call(grp, pos_flat, x, gate_pad, gate2, ys, final_g=None):
    gt, gt_spec = grp.rowmod(gate2)
    assert grp.tm == TOKEN_BLOCK
    with_norm = final_g is not None
    args = [pos_flat, x, gate_pad, gt, ys]
    specs = [pl.BlockSpec((TOKEN_BLOCK * TOP_K,), lambda i: (i,), memory_space=pltpu.SMEM),
             grp.rows(D), grp.rows(LANES), gt_spec, pl.BlockSpec(memory_space=pl.ANY)]
    shape = jax.ShapeDtypeStruct((grp.n, D), F32)
    if with_norm:
        args.append(final_g.reshape(1, D))
        specs.append(_full((1, D)))
    return pl.pallas_call(
        functools.partial(_combine_kernel, with_norm),
        out_shape=(shape, shape) if with_norm else shape,
        grid=(grp.tiles,),
        in_specs=specs,
        out_specs=(grp.rows(D), grp.rows(D)) if with_norm else grp.rows(D),
        scratch_shapes=[pltpu.VMEM((TOP_K, TOKEN_BLOCK, D), F32), pltpu.SemaphoreType.DMA((2,))],
        compiler_params=_cparams(("arbitrary",)),
        name="moe_combine",
    )(*args)


def _moe(layer, groups, xs_in, norm_g, mods, w_router, b_router, w_gu, b_gu, w_down, b_down, final_g=None):
    counts = jnp.zeros((1, N_EXPERTS), F32)
    hs, idxs, gates, ranks = [], [], [], []
    for grp, x, (sh2, sc2, _) in zip(groups, xs_in, mods):
        h, idx, gate, rank, counts = _router_call(grp, counts, x, norm_g, sh2, sc2, w_router, b_router)
        hs.append(h)
        idxs.append(idx[:, :TOP_K])
        gates.append(gate)
        ranks.append(rank[:, :TOP_K])
    h_all = jnp.concatenate(hs, axis=0)
    idx_all = jnp.concatenate(idxs, axis=0)
    rank_all = jnp.concatenate(ranks, axis=0)
    n = h_all.shape[0]
    cnt = counts[0].astype(I32)
    padded = ((cnt + EXPERT_TILE - 1) // EXPERT_TILE) * EXPERT_TILE
    ends = jnp.cumsum(padded)
    offsets = ends - padded
    n_tiles = (n * TOP_K + N_EXPERTS * (EXPERT_TILE - 1)) // EXPERT_TILE
    n_rows = n_tiles * EXPERT_TILE
    pos = (jnp.take(offsets, idx_all) + rank_all).astype(I32)
    pos_flat = pos.reshape(n * TOP_K)
    n_valid = (ends[-1] // EXPERT_TILE).astype(I32)
    tile_start = jnp.arange(n_tiles, dtype=I32) * EXPERT_TILE
    tile_start = jnp.minimum(tile_start, ends[-1] - EXPERT_TILE)
    tile_expert = jnp.sum(tile_start[:, None] >= ends[None, :], axis=1).astype(I32)
    n_valid = n_valid.reshape(1)
    xs = _dispatch_call(cnt, offsets.astype(I32), n_valid, pos_flat, h_all, n_rows)
    ys = _expert_call(layer, tile_expert, n_valid, xs, w_gu, b_gu, w_down, b_down)
    outs = []
    start = 0
    for grp, x, gate, (_, _, gt2) in zip(groups, xs_in, gates, mods):
        cgrp = _Group(grp.b, grp.t, TOKEN_BLOCK)
        p = lax.dynamic_slice_in_dim(pos_flat, start * TOP_K, grp.n * TOP_K)
        outs.append(_combine_call(cgrp, p, x, gate, gt2, ys, final_g))
        start += grp.n
    return outs


def _rwkv_proj_kernel(t_len, x_ref, xp_ref, g_ref, sh_ref, sc_ref, s0_ref, mu_ref, wrkv_ref, w0_ref,
                      w1_ref, w2_ref, a0_ref, a1_ref, a2_ref, g1_ref, g2_ref,
                      r_ref, w_ref, k_ref, v_ref, a_ref, gg_ref, scr):
    tm = x_ref.shape[0]
    g, sh, sc = g_ref[...], sh_ref[0], sc_ref[0]
    h = _modulate(x_ref[...], g, sh, sc)
    hp = _modulate(xp_ref[...], g, sh[0:8] if sh.shape[0] > 1 else sh, sc[0:8] if sc.shape[0] > 1 else sc)
    _stage_rows(scr, h, hp)
    prev = jnp.where(_tpos(tm, t_len) == 0, s0_ref[0], _shifted_rows(scr, 1))
    dx = prev - h
    mu = mu_ref[...]
    xr, xw, xk, xv, xa, xg = [h + dx * mu[n:n + 1] for n in range(6)]
    r_ref[...] = _bdot(xr, wrkv_ref[0]).astype(BF16)
    k_ref[...] = _bdot(xk, wrkv_ref[1]).astype(BF16)
    v_ref[...] = _bdot(xv, wrkv_ref[2]).astype(BF16)
    w_log = -_softplus(-(w0_ref[...] + _bdot(jnp.tanh(_bdot(xw, w1_ref[...])), w2_ref[...]))) - 0.5
    w_ref[...] = jnp.exp(-jnp.exp(w_log))
    a_ref[...] = _sigmoid(a0_ref[...] + _bdot(_bdot(xa, a1_ref[...]), a2_ref[...])).astype(BF16)
    gg_ref[...] = _bdot(_sigmoid(_bdot(xg, g1_ref[...])), g2_ref[...]).astype(BF16)


def _rwkv_core_kernel(r_ref, w_ref, k_ref, v_ref, a_ref, kk_p, ka_p, rk_p, lnw_p, lnb_p, s0_ref,
                      y_ref, st_ref, state, kk_s, b_s, km_s, r_s):
    j = pl.program_id(1)
    tc = r_ref.shape[0]
    n = RW_N

    @pl.when(j == 0)
    def _():
        state[...] = s0_ref[...]

    def step(t, c):
        kt, at, vt, rt = [z[t].astype(F32) for z in (k_ref, a_ref, v_ref, r_ref)]
        r_s[...] = rt
        kk = kt * kk_p[...]
        kk = kk * lax.rsqrt(jnp.sum(kk * kk, axis=0, keepdims=True) + 1e-6)
        km = kt * (1.0 + (at - 1.0) * ka_p[...])
        kk_s[...] = kk
        b_s[...] = kk * at
        km_s[...] = km
        sa = jnp.zeros((n, LANES), F32)
        for kx in range(n):
            sa = sa + state[kx] * kk_s[pl.ds(kx, 1), :]
        y = jnp.zeros((n, LANES), F32)
        for kx in range(n):
            s_new = (state[kx] * w_ref[t, pl.ds(kx, 1), :] - sa * b_s[pl.ds(kx, 1), :]
                     + vt * km_s[pl.ds(kx, 1), :])
            state[kx] = s_new
            y = y + s_new * r_s[pl.ds(kx, 1), :]
        mean = jnp.mean(y, axis=0, keepdims=True)
        yc = y - mean
        var = jnp.mean(yc * yc, axis=0, keepdims=True)
        bonus = jnp.sum(rt * km * rk_p[...], axis=0, keepdims=True) * vt
        y_ref[t] = (yc * lax.rsqrt(var + RW_GN_EPS) * lnw_p[...] + lnb_p[...] + bonus).astype(BF16)
        return c

    lax.fori_loop(0, tc, step, 0)

    @pl.when(j == pl.num_programs(1) - 1)
    def _():
        st_ref[...] = state[...]


def _rwkv_layer(grp, x, norm_g, mods, shift0, wkv0, mu, w_rkv, w0, w1, w2, a0, a1, a2, g1, g2,
                k_k, k_a, r_k, ln_w, ln_b, w_o):
    b, t = grp.b, grp.t
    shift, scale, gate = mods
    sh, sh_spec = grp.rowmod(shift)
    sc, sc_spec = grp.rowmod(scale)
    if shift0 is None:
        s0 = jnp.zeros((b, 1, D), F32)
    else:
        s0 = jnp.concatenate([shift0[:, None, :], jnp.zeros((b, t - 1, D), F32)], axis=1)
    s0, s0_spec = grp.rowseq(s0)
    bf = lambda z: z.astype(BF16)
    row = lambda z: z.reshape(1, -1)
    weights = [mu, bf(w_rkv), row(w0), bf(w1), bf(w2), row(a0), bf(a1), bf(a2), bf(g1), bf(g2)]
    outs = pl.pallas_call(
        functools.partial(_rwkv_proj_kernel, t),
        out_shape=tuple(jax.ShapeDtypeStruct((grp.n, D), dt) for dt in (BF16, F32, BF16, BF16, BF16, BF16)),
        grid=(grp.tiles,),
        in_specs=[grp.rows(D), grp.prev8(D), _full((1, D)), sh_spec, sc_spec, s0_spec]
        + [_full(z.shape) for z in weights],
        out_specs=tuple(grp.rows(D) for _ in range(6)),
        scratch_shapes=[pltpu.VMEM((grp.tm + 8, D), F32)],
        compiler_params=_cparams(("arbitrary",)),
        name="rwkv_proj",
    )(x, x, row(norm_g), sh, sc, s0, *weights)
    r, w, k, v, a, gg = outs
    bh = b * RW_H

    def to_core(z):
        return z.reshape(b, t, RW_H, RW_N).transpose(1, 3, 0, 2).reshape(t, RW_N, bh)

    def ptile(p):
        return jnp.tile(p.reshape(RW_H, RW_N).T, (1, b))

    if wkv0 is None:
        st0 = jnp.zeros((RW_N, RW_N, bh), F32)
    else:
        st0 = wkv0.transpose(3, 2, 0, 1).reshape(RW_N, RW_N, bh)
    tc = min(t, 16)
    seq = pl.BlockSpec((tc, RW_N, LANES), lambda q, j: (j, 0, q))
    par = pl.BlockSpec((RW_N, LANES), lambda q, j: (0, q))
    stt = pl.BlockSpec((RW_N, RW_N, LANES), lambda q, j: (0, 0, q))
    y, st = pl.pallas_call(
        _rwkv_core_kernel,
        out_shape=(jax.ShapeDtypeStruct((t, RW_N, bh), BF16),
                   jax.ShapeDtypeStruct((RW_N, RW_N, bh), F32)),
        grid=(bh // LANES, t // tc),
        in_specs=[seq] * 5 + [par] * 5 + [stt],
        out_specs=(seq, stt),
        scratch_shapes=[pltpu.VMEM((RW_N, RW_N, LANES), F32)] + [pltpu.VMEM((RW_N, LANES), F32)] * 4,
        compiler_params=_cparams(("arbitrary", "arbitrary")),
        name="rwkv_core",
    )(to_core(r), to_core(w), to_core(k), to_core(v), to_core(a),
      ptile(k_k), ptile(k_a), ptile(r_k.reshape(-1)), ptile(ln_w), ptile(ln_b), st0)
    y_rows = y.reshape(t, RW_N, b, RW_H).transpose(2, 0, 3, 1).reshape(grp.n, D)
    x_new = _outproj_call(grp, x, y_rows, bf(w_o), gate, mul=gg)
    new_wkv = st.reshape(RW_N, RW_N, b, RW_H).transpose(2, 3, 1, 0)
    x_last = x.reshape(b, t, D)[:, -1]
    new_shift = _modrows_call(x_last, norm_g, shift, scale)
    return x_new, new_shift, new_wkv


def _pad_time(z, b, t, tp):
    if tp == t:
        return z
    w = z.shape[-1]
    return jnp.pad(z.reshape(b, t, w), ((0, 0), (0, tp - t), (0, 0))).reshape(b * tp, w)


def _unpad_time(z, b, t, tp):
    if tp == t:
        return z
    w = z.shape[-1]
    return z.reshape(b, tp, w)[:, :t].reshape(b * t, w)


def _gdn_proj_kernel(t_len, x_ref, xp_ref, g_ref, sh_ref, sc_ref, c1_ref, c2_ref, c3_ref, wqkv_ref,
                     wz_ref, wb_ref, wa_ref, cw_ref, alog_ref, dtb_ref,
                     qkv_ref, z_ref, beta_ref, gdec_ref, scr):
    tm = x_ref.shape[0]
    g, sh, sc = g_ref[...], sh_ref[0], sc_ref[0]
    h = _modulate(x_ref[...], g, sh, sc)
    hp = _modulate(xp_ref[...], g, sh[0:8] if sh.shape[0] > 1 else sh, sc[0:8] if sc.shape[0] > 1 else sc)
    hb = h.astype(BF16)
    pre = jnp.dot(hb, wqkv_ref[...], preferred_element_type=F32)
    pre8 = _bdot(hp, wqkv_ref[...])
    tpos = _tpos(tm, t_len)
    cw = cw_ref[...]
    conv = pre * cw[3:4]
    _stage_rows(scr, pre, pre8)
    for d, cref in ((1, c1_ref), (2, c2_ref), (3, c3_ref)):
        past = jnp.where(tpos >= d, _shifted_rows(scr, d), cref[0])
        conv = conv + past * cw[3 - d:4 - d]
    act = _silu(conv)
    nh = GD_H
    for hh in range(2 * nh):
        sl = slice(hh * GD_DK, (hh + 1) * GD_DK)
        seg = act[:, sl]
        seg = seg * lax.rsqrt(jnp.sum(seg * seg, axis=-1, keepdims=True) + 1e-6)
        if hh < nh:
            seg = seg * (GD_DK ** -0.5)
        qkv_ref[:, sl] = seg.astype(BF16)
    qkv_ref[:, 2 * nh * GD_DK:] = act[:, 2 * nh * GD_DK:].astype(BF16)
    z_ref[...] = jnp.dot(hb, wz_ref[...], preferred_element_type=F32).astype(BF16)
    beta_ref[...] = _sigmoid(jnp.dot(hb, wb_ref[...], preferred_element_type=F32))
    a_logit = jnp.dot(hb, wa_ref[...], preferred_element_type=F32)
    gdec_ref[...] = -jnp.exp(alog_ref[...]) * _softplus(a_logit + dtb_ref[...])


def _unit_lower_inverse(a, eye, masks):
    blk8, offs = masks
    n = range(len(a))
    a8 = [jnp.where(blk8, a[i], 0.0) for i in n]
    x = [eye - a8[i] for i in n]
    y = [_bdot(a8[i], a8[i]) for i in n]
    x = [x[i] + _bdot(x[i], y[i]) for i in n]
    y = [_bdot(y[i], y[i]) for i in n]
    x = [x[i] + _bdot(x[i], y[i]) for i in n]
    for off in offs:
        t = [_bdot(jnp.where(off, a[i], 0.0), x[i]) for i in n]
        x = [x[i] - _bdot(x[i], t[i]) for i in n]
    return x


def _inverse_masks(c):
    ri = lax.broadcasted_iota(I32, (c, c), 0)
    ci = lax.broadcasted_iota(I32, (c, c), 1)
    sr = lambda z, s: lax.shift_right_logical(z, jnp.full(z.shape, s, I32))
    blk8 = sr(ri, 3) == sr(ci, 3)
    offs = []
    m, lg = 8, 3
    while m < c:
        same = sr(ri, lg + 1) == sr(ci, lg + 1)
        lower = jnp.logical_and((sr(ri, lg) & 1) == 1, (sr(ci, lg) & 1) == 0)
        offs.append(jnp.logical_and(same, lower))
        m, lg = m * 2, lg + 1
    return ri, ci, (blk8, offs)


def _gdn_core_kernel(q_ref, k_ref, v_ref, z_ref, beta_ref, g_ref, s0_ref, nw_ref, y_ref, st_ref, state):
    cidx = pl.program_id(1)
    c = q_ref.shape[0]

    @pl.when(cidx == 0)
    def _():
        state[...] = s0_ref[0]

    ri, ci, masks = _inverse_masks(c)
    incl = ri >= ci
    strict = ri > ci
    eye = (ri == ci).astype(F32)
    g = g_ref[...]
    cum = _fdot(incl.astype(F32), g)
    cum_t = lax.dot_general(g, (ci >= ri).astype(F32), (((0,), (0,)), ((), ())),
                            precision=HIGHEST, preferred_element_type=F32)
    beta = beta_ref[...]
    nw = nw_ref[...]
    heads = range(GD_H)
    sls = [slice(h * GD_DK, (h + 1) * GD_DK) for h in heads]
    q = [q_ref[:, sl] for sl in sls]
    k = [k_ref[:, sl] for sl in sls]
    kf = [z.astype(F32) for z in k]
    v = [v_ref[:, sl].astype(F32) for sl in sls]
    s = [state[h] for h in heads]
    cum_c = [cum[:, h:h + 1] for h in heads]
    dec = [jnp.where(incl, jnp.exp(jnp.where(incl, cum_c[h] - cum_t[h:h + 1, :], 0.0)), 0.0) for h in heads]
    bcol = [beta[:, h:h + 1] for h in heads]
    kb = [kf[h] * bcol[h] for h in heads]
    a = [jnp.where(strict, _bdot_nt(kb[h], k[h]) * dec[h], 0.0) for h in heads]
    attn = [_bdot_nt(q[h], k[h]) * dec[h] for h in heads]
    x = _unit_lower_inverse(a, eye, masks)
    ecum = [jnp.exp(cum_c[h]) for h in heads]
    sol = [_bdot(x[h], jnp.concatenate([v[h] * bcol[h], kb[h] * ecum[h]], axis=1)) for h in heads]
    u = [sol[h][:, :GD_DV] - _bdot(sol[h][:, GD_DV:], s[h]) for h in heads]
    o = [_bdot(q[h].astype(F32) * ecum[h], s[h]) + _bdot(attn[h], u[h]) for h in heads]
    last = [cum[c - 1:c, h:h + 1] for h in heads]
    s_new = [s[h] * jnp.exp(last[h]) + _bdot_tn(kf[h] * jnp.exp(last[h] - cum_c[h]), u[h]) for h in heads]
    for h in heads:
        state[h] = s_new[h]
        on = o[h] * lax.rsqrt(jnp.mean(o[h] * o[h], axis=-1, keepdims=True) + NORM_EPS) * nw
        y_ref[:, sls[h]] = (on * _silu(z_ref[:, sls[h]].astype(F32))).astype(BF16)

    @pl.when(cidx == pl.num_programs(1) - 1)
    def _():
        st_ref[0] = state[...]


def _gdn_layer(grp, x, norm_g, mods, conv0, ssm0, w_in, conv_w, a_log, dt_bias, norm_w, w_o, chunk):
    b, t = grp.b, grp.t
    shift, scale, gate = mods
    sh, sh_spec = grp.rowmod(shift)
    sc, sc_spec = grp.rowmod(scale)
    kd = GD_H * GD_DK
    cstates, cspecs = [], []
    for d in (1, 2, 3):
        if conv0 is None:
            cs = jnp.zeros((b, 1, GD_C), F32)
        else:
            cs = jnp.concatenate([conv0[:, 3 - d:, :], jnp.zeros((b, t - d, GD_C), F32)], axis=1)
        cs, spec = grp.rowseq(cs)
        cstates.append(cs)
        cspecs.append(spec)
    bf = lambda z: z.astype(BF16)
    pad128 = lambda z: jnp.pad(z, ((0, 0), (0, LANES - z.shape[1])))
    w_qkv = bf(w_in[:, :GD_C])
    w_z = bf(w_in[:, GD_C:GD_C + kd])
    w_b = bf(pad128(w_in[:, GD_C + kd:GD_C + kd + GD_H]))
    w_a = bf(pad128(w_in[:, GD_C + kd + GD_H:]))
    weights = [w_qkv, w_z, w_b, w_a, conv_w, pad128(a_log.reshape(1, GD_H)), pad128(dt_bias.reshape(1, GD_H))]
    qkv, z, beta, gdec = pl.pallas_call(
        functools.partial(_gdn_proj_kernel, t),
        out_shape=(jax.ShapeDtypeStruct((grp.n, GD_C), BF16), jax.ShapeDtypeStruct((grp.n, kd), BF16),
                   jax.ShapeDtypeStruct((grp.n, LANES), F32), jax.ShapeDtypeStruct((grp.n, LANES), F32)),
        grid=(grp.tiles,),
        in_specs=[grp.rows(D), grp.prev8(D), _full((1, D)), sh_spec, sc_spec] + cspecs
        + [_full(z_.shape) for z_ in weights],
        out_specs=(grp.rows(GD_C), grp.rows(kd), grp.rows(LANES), grp.rows(LANES)),
        scratch_shapes=[pltpu.VMEM((grp.tm + 8, GD_C), F32)],
        compiler_params=_cparams(("arbitrary",)),
        name="gdn_proj",
    )(x, x, norm_g.reshape(1, D), sh, sc, *cstates, *weights)
    tp = ((t + chunk - 1) // chunk) * chunk
    nc = tp // chunk
    qkv_p, z_p = _pad_time(qkv, b, t, tp), _pad_time(z, b, t, tp)
    beta_p, g_p = _pad_time(beta, b, t, tp), _pad_time(gdec, b, t, tp)
    if ssm0 is None:
        ssm0 = jnp.zeros((b, GD_H, GD_DK, GD_DV), F32)
    col = lambda j: pl.BlockSpec((chunk, kd), lambda bi, c: (bi * nc + c, j))
    lan = pl.BlockSpec((chunk, LANES), lambda bi, c: (bi * nc + c, 0))
    stt = pl.BlockSpec((1, GD_H, GD_DK, GD_DV), lambda bi, c: (bi, 0, 0, 0))
    y, st = pl.pallas_call(
        _gdn_core_kernel,
        out_shape=(jax.ShapeDtypeStruct((b * tp, kd), BF16),
                   jax.ShapeDtypeStruct((b, GD_H, GD_DK, GD_DV), F32)),
        grid=(b, nc),
        in_specs=[col(0), col(1), col(2), col(0), lan, lan, stt, _full((1, GD_DV))],
        out_specs=(col(0), stt),
        scratch_shapes=[pltpu.VMEM((GD_H, GD_DK, GD_DV), F32)],
        compiler_params=_cparams(("arbitrary", "arbitrary")),
        name="gdn_core",
    )(qkv_p, qkv_p, qkv_p, z_p, beta_p, g_p, ssm0, norm_w.reshape(1, GD_DV))
    x_new = _outproj_call(grp, x, _unpad_time(y, b, t, tp), bf(w_o), gate)
    nl = min(t, GD_CONV - 1)
    x_last = x.reshape(b, t, D)[:, t - nl:].reshape(b * nl, D)
    rep = lambda m: jnp.repeat(m, nl, axis=0)
    pre_last = _modrows_call(x_last, norm_g, rep(shift), rep(scale), w_qkv).reshape(b, nl, GD_C)
    if nl < GD_CONV - 1:
        pre_last = jnp.concatenate([conv0[:, nl:], pre_last], axis=1)
    return x_new, pre_last, st


def _ret_proj_kernel(x_ref, g_ref, sh_ref, sc_ref, cos_ref, sin_ref, w_ref, q_ref, k_ref, v_ref, gate_ref):
    h = _modulate(x_ref[...], g_ref[...], sh_ref[0], sc_ref[0]).astype(BF16)
    kd = RT_H * RT_DK
    vd = RT_H * RT_DV
    cos, sin = cos_ref[0], sin_ref[0]
    even = (lax.broadcasted_iota(I32, (1, kd), 1) & 1) == 0

    def rotary(z):
        swapped = jnp.where(even, pltpu.roll(z, kd - 1, 1), pltpu.roll(z, 1, 1))
        return z * cos + swapped * sin

    q_ref[...] = rotary(jnp.dot(h, w_ref[:, 0:kd], preferred_element_type=F32)).astype(BF16)
    k = rotary(jnp.dot(h, w_ref[:, kd:2 * kd], preferred_element_type=F32))
    k_ref[...] = (k * (RT_DK ** -0.5)).astype(BF16)
    v_ref[...] = jnp.dot(h, w_ref[:, 2 * kd:2 * kd + vd], preferred_element_type=F32).astype(BF16)
    gate_ref[...] = jnp.dot(h, w_ref[:, 2 * kd + vd:], preferred_element_type=F32).astype(BF16)


def _ret_core_kernel(q_ref, k_ref, v_ref, gate_ref, dm_ref, qd_ref, kd_ref, cd_ref, s0_ref, nw_ref,
                     y_ref, st_ref, state):
    cidx = pl.program_id(1)

    @pl.when(cidx == 0)
    def _():
        state[...] = s0_ref[0]

    heads = range(RT_H)
    kss = [slice(h * RT_DK, (h + 1) * RT_DK) for h in heads]
    vss = [slice(h * RT_DV, (h + 1) * RT_DV) for h in heads]
    q = [q_ref[:, ks] for ks in kss]
    k = [k_ref[:, ks] for ks in kss]
    v = [v_ref[:, vs] for vs in vss]
    s = [state[h] for h in heads]
    inner = [_bdot_nt(q[h], k[h]) * dm_ref[h] for h in heads]
    cross = [_bdot(q[h], s[h]) * qd_ref[h] for h in heads]
    o = [_bdot(inner[h], v[h]) + cross[h] for h in heads]
    s_new = [s[h] * cd_ref[h] + _bdot_tn(k[h].astype(F32) * kd_ref[h], v[h]) for h in heads]
    for h in heads:
        state[h] = s_new[h]
        on = o[h] * lax.rsqrt(jnp.mean(o[h] * o[h], axis=-1, keepdims=True) + NORM_EPS) * nw_ref[:, vss[h]]
        y_ref[:, vss[h]] = (on * _silu(gate_ref[:, vss[h]].astype(F32))).astype(BF16)

    @pl.when(cidx == pl.num_programs(1) - 1)
    def _():
        st_ref[0] = state[...]


def _ret_layer(grp, x, norm_g, mods, s0, pos0, w_in, norm_w, w_o, chunk):
    b, t = grp.b, grp.t
    shift, scale, gate = mods
    sh, sh_spec = grp.rowmod(shift)
    sc, sc_spec = grp.rowmod(scale)
    kd, vd = RT_H * RT_DK, RT_H * RT_DV
    half = RT_DK // 2
    inv = 1.0 / (10000.0 ** jnp.linspace(0.0, 1.0, half, dtype=F32))
    pos = jnp.arange(t, dtype=F32) + float(pos0)
    ang = pos[:, None] * inv[None, :]
    cos = jnp.repeat(jnp.cos(ang), 2, axis=1)
    sin = jnp.stack([-jnp.sin(ang), jnp.sin(ang)], axis=-1).reshape(t, RT_DK)
    cos4, cos_spec = grp.postab(jnp.tile(cos, (1, RT_H)))
    sin4, sin_spec = grp.postab(jnp.tile(sin, (1, RT_H)))
    wb = w_in.astype(BF16)
    q, k, v, gt = pl.pallas_call(
        _ret_proj_kernel,
        out_shape=(jax.ShapeDtypeStruct((grp.n, kd), BF16), jax.ShapeDtypeStruct((grp.n, kd), BF16),
                   jax.ShapeDtypeStruct((grp.n, vd), BF16), jax.ShapeDtypeStruct((grp.n, vd), BF16)),
        grid=(grp.tiles,),
        in_specs=[grp.rows(D), _full((1, D)), sh_spec, sc_spec, cos_spec, sin_spec, _full(wb.shape)],
        out_specs=(grp.rows(kd), grp.rows(kd), grp.rows(vd), grp.rows(vd)),
        compiler_params=_cparams(("arbitrary",)),
        name="ret_proj",
    )(x, norm_g.reshape(1, D), sh, sc, cos4, sin4, wb)
    tp = ((t + chunk - 1) // chunk) * chunk
    nc = tp // chunk
    nv = min(t, chunk)
    assert tp == t or nc == 1
    log_gamma = jnp.log1p(-jnp.exp2(-5.0 - jnp.arange(RT_H, dtype=F32)))
    idx = jnp.arange(chunk, dtype=F32)
    diff = idx[:, None] - idx[None, :]
    dmask = jnp.where(diff >= 0, jnp.exp(log_gamma[:, None, None] * jnp.maximum(diff, 0.0)), 0.0)
    q_dec = jnp.exp(log_gamma[:, None] * (idx + 1.0))[:, :, None]
    k_dec = jnp.exp(log_gamma[:, None] * jnp.maximum(nv - 1.0 - idx, 0.0))[:, :, None]
    c_dec = jnp.exp(log_gamma * nv)[:, None, None]
    if s0 is None:
        s0 = jnp.zeros((b, RT_H, RT_DK, RT_DV), F32)
    rowk = pl.BlockSpec((chunk, kd), lambda bi, c: (bi * nc + c, 0))
    rowv = pl.BlockSpec((chunk, vd), lambda bi, c: (bi * nc + c, 0))
    stt = pl.BlockSpec((1, RT_H, RT_DK, RT_DV), lambda bi, c: (bi, 0, 0, 0))
    y, st = pl.pallas_call(
        _ret_core_kernel,
        out_shape=(jax.ShapeDtypeStruct((b * tp, vd), BF16),
                   jax.ShapeDtypeStruct((b, RT_H, RT_DK, RT_DV), F32)),
        grid=(b, nc),
        in_specs=[rowk, rowk, rowv, rowv, _full(dmask.shape), _full(q_dec.shape), _full(k_dec.shape),
                  _full(c_dec.shape), stt, _full((1, vd))],
        out_specs=(rowv, stt),
        scratch_shapes=[pltpu.VMEM((RT_H, RT_DK, RT_DV), F32)],
        compiler_params=_cparams(("arbitrary", "arbitrary")),
        name="ret_core",
    )(_pad_time(q, b, t, tp), _pad_time(k, b, t, tp), _pad_time(v, b, t, tp), _pad_time(gt, b, t, tp),
      dmask, q_dec, k_dec, c_dec, s0, norm_w.reshape(1, vd))
    x_new = _outproj_call(grp, x, _unpad_time(y, b, t, tp), w_o.astype(BF16), gate)
    return x_new, st


def _hgrn_proj_kernel(layer, x_ref, g_ref, sh_ref, sc_ref, lbl_ref, w_ref,
                      q_ref, k_ref, lf_ref, v_ref, gate_ref):
    h = _modulate(x_ref[...], g_ref[...], sh_ref[0], sc_ref[0]).astype(BF16)
    ed = HG_H * HG_E
    logits = lbl_ref[...]
    e = jnp.exp(logits - jnp.max(logits, axis=0, keepdims=True))
    lrow = lax.broadcasted_iota(I32, logits.shape, 0)
    part = jnp.where(jnp.logical_and(lrow >= 1, lrow <= layer), e, 0.0)
    lb = jnp.sum(part, axis=0, keepdims=True) / jnp.sum(e, axis=0, keepdims=True)
    q_ref[...] = jnp.dot(h, w_ref[:, 0:ed], preferred_element_type=F32)
    f = lb + (1.0 - lb) * _sigmoid(jnp.dot(h, w_ref[:, ed:2 * ed], preferred_element_type=F32))
    k_ref[...] = 1.0 - f
    lf_ref[...] = jnp.log(f)
    v_ref[...] = jnp.dot(h, w_ref[:, 2 * ed:3 * ed], preferred_element_type=F32).astype(BF16)
    gate_ref[...] = jnp.dot(h, w_ref[:, 3 * ed:], preferred_element_type=F32).astype(BF16)


def _hgrn_core_kernel(q_ref, k_ref, lf_ref, v_ref, gate_ref, s0_ref, nw_ref, y_ref, st_ref, state):
    cidx = pl.program_id(1)
    c = q_ref.shape[0]

    @pl.when(cidx == 0)
    def _():
        state[...] = s0_ref[0]

    ri = lax.broadcasted_iota(I32, (c, c), 0)
    ci = lax.broadcasted_iota(I32, (c, c), 1)
    ltri = (ri >= ci).astype(F32)
    row8 = lax.broadcasted_iota(I32, (8, 1), 0)
    heads = range(HG_H)
    sls = [slice(h * HG_E, (h + 1) * HG_E) for h in heads]
    q = [q_ref[:, sl] for sl in sls]
    k = [k_ref[:, sl] for sl in sls]
    v = [v_ref[:, sl].astype(F32) for sl in sls]
    st = [state[h] for h in heads]
    cum = [_fdot(ltri, lf_ref[:, sl]) for sl in sls]
    inter = [_bdot_nt(q[h] * jnp.exp(cum[h]), st[h]) for h in heads]
    last = [cum[h][c - 1:c, :] for h in heads]
    s_new = [st[h] * jnp.exp(last[h]) + _bdot_tn(v[h], k[h] * jnp.exp(last[h] - cum[h])) for h in heads]
    for h in heads:
        state[h] = s_new[h]
        parts = []
        for g0 in range(0, c, 8):
            qg, cg = q[h][g0:g0 + 8], cum[h][g0:g0 + 8]
            acc = inter[h][g0:g0 + 8]
            for j in range(g0 + 8):
                diff = cg - cum[h][j:j + 1, :]
                if j >= g0:
                    causal = row8 >= (j - g0)
                    diff = jnp.where(causal, diff, 0.0)
                col = jnp.sum(qg * k[h][j:j + 1, :] * jnp.exp(diff), axis=-1, keepdims=True)
                if j >= g0:
                    col = jnp.where(causal, col, 0.0)
                acc = acc + col * v[h][j:j + 1, :]
            parts.append(acc)
        o = parts[0] if len(parts) == 1 else jnp.concatenate(parts, axis=0)
        on = o * lax.rsqrt(jnp.mean(o * o, axis=-1, keepdims=True) + NORM_EPS) * nw_ref[:, sls[h]]
        y_ref[:, sls[h]] = (on * _silu(gate_ref[:, sls[h]].astype(F32))).astype(BF16)

    @pl.when(cidx == pl.num_programs(1) - 1)
    def _():
        st_ref[0] = state[...]


def _hgrn_layer(grp, x, norm_g, mods, s0, layer, lb_logits, w_in, norm_w, w_o, chunk):
    b, t = grp.b, grp.t
    shift, scale, gate = mods
    sh, sh_spec = grp.rowmod(shift)
    sc, sc_spec = grp.rowmod(scale)
    ed, vd = HG_H * HG_E, HG_H * HG_DV
    wb = w_in.astype(BF16)
    q, k, lf, v, gt = pl.pallas_call(
        functools.partial(_hgrn_proj_kernel, layer),
        out_shape=(jax.ShapeDtypeStruct((grp.n, ed), F32), jax.ShapeDtypeStruct((grp.n, ed), F32),
                   jax.ShapeDtypeStruct((grp.n, ed), F32), jax.ShapeDtypeStruct((grp.n, vd), BF16),
                   jax.ShapeDtypeStruct((grp.n, vd), BF16)),
        grid=(grp.tiles,),
        in_specs=[grp.rows(D), _full((1, D)), sh_spec, sc_spec, _full(lb_logits.shape), _full(wb.shape)],
        out_specs=(grp.rows(ed), grp.rows(ed), grp.rows(ed), grp.rows(vd), grp.rows(vd)),
        compiler_params=_cparams(("arbitrary",)),
        name="hgrn_proj",
    )(x, norm_g.reshape(1, D), sh, sc, lb_logits, wb)
    tp = ((t + chunk - 1) // chunk) * chunk
    nc = tp // chunk
    if s0 is None:
        s0 = jnp.zeros((b, HG_H, HG_E, HG_DV), F32)
    row = pl.BlockSpec((chunk, ed), lambda bi, c: (bi * nc + c, 0))
    stt = pl.BlockSpec((1, HG_H, HG_E, HG_DV), lambda bi, c: (bi, 0, 0, 0))
    y, st = pl.pallas_call(
        _hgrn_core_kernel,
        out_shape=(jax.ShapeDtypeStruct((b * tp, vd), BF16),
                   jax.ShapeDtypeStruct((b, HG_H, HG_E, HG_DV), F32)),
        grid=(b, nc),
        in_specs=[row, row, row, row, row, stt, _full((1, vd))],
        out_specs=(row, stt),
        scratch_shapes=[pltpu.VMEM((HG_H, HG_E, HG_DV), F32)],
        compiler_params=_cparams(("arbitrary", "arbitrary")),
        name="hgrn_core",
    )(_pad_time(q, b, t, tp), _pad_time(k, b, t, tp), _pad_time(lf, b, t, tp), _pad_time(v, b, t, tp),
      _pad_time(gt, b, t, tp), jnp.swapaxes(s0, 2, 3), norm_w.reshape(1, vd))
    x_new = _outproj_call(grp, x, _unpad_time(y, b, t, tp), w_o.astype(BF16), gate)
    return x_new, jnp.swapaxes(st, 2, 3)


ROW_TILE = 256
MOE_TILE = 512
GDN_CHUNK, RET_CHUNK, HGRN_CHUNK = 64, 128, 16
SAMPLE_CHUNK = 16
PAST_LEN = 16384


def kernel(x_prompt, x_sample, c_prompt, c_sample, state_rwkv_wkv, state_rwkv_shift, state_gdn_ssm, state_gdn_conv, state_ret, state_hgrn, ada_w, ada_b, norm_mix, norm_ffn, norm_final, rwkv_mu, rwkv_w_rkv, rwkv_w0, rwkv_w1, rwkv_w2, rwkv_a0, rwkv_a1, rwkv_a2, rwkv_g1, rwkv_g2, rwkv_k_k, rwkv_k_a, rwkv_r_k, rwkv_ln_w, rwkv_ln_b, rwkv_w_o, gdn_w_in, gdn_conv_w, gdn_a_log, gdn_dt_bias, gdn_norm_w, gdn_w_o, ret_w_in, ret_norm_w, ret_w_o, hgrn_w_in, hgrn_lb_logits, hgrn_norm_w, hgrn_w_o, moe_w_router, moe_b_router, moe_w_gu, moe_b_gu, moe_w_down, moe_b_down):
    bp, tp, _ = x_prompt.shape
    bs, ts, _ = x_sample.shape
    gp, gs = _Group(bp, tp, ROW_TILE), _Group(bs, ts, ROW_TILE)
    mp, msg = _Group(bp, tp, MOE_TILE), _Group(bs, ts, MOE_TILE)
    ada = _ada_call(jnp.concatenate([c_prompt, c_sample], axis=0), ada_w, ada_b)
    xp = x_prompt.reshape(bp * tp, D)
    xs = x_sample.reshape(bs * ts, D)
    outs_p = {k: [] for k in ("wkv", "shift", "ssm", "conv", "ret", "hgrn")}
    outs_s = {k: [] for k in ("wkv", "shift", "ssm", "conv", "ret", "hgrn")}
    for i in range(DEPTH):
        kind, j = i % 4, i // 4
        m = ada[i].reshape(bp + bs, 6, D)
        mod_p = [m[:bp, n] for n in range(6)]
        mod_s = [m[bp:, n] for n in range(6)]
        g = norm_mix[i]
        if kind == 0:
            prm = (rwkv_mu[j], rwkv_w_rkv[j], rwkv_w0[j], rwkv_w1[j], rwkv_w2[j], rwkv_a0[j], rwkv_a1[j],
                   rwkv_a2[j], rwkv_g1[j], rwkv_g2[j], rwkv_k_k[j], rwkv_k_a[j], rwkv_r_k[j],
                   rwkv_ln_w[j], rwkv_ln_b[j], rwkv_w_o[j])
            xp, sh_p, wkv_p = _rwkv_layer(gp, xp, g, mod_p[:3], None, None, *prm)
            xs, sh_s, wkv_s = _rwkv_layer(gs, xs, g, mod_s[:3], state_rwkv_shift[j], state_rwkv_wkv[j], *prm)
            outs_p["wkv"].append(wkv_p); outs_p["shift"].append(sh_p)
            outs_s["wkv"].append(wkv_s); outs_s["shift"].append(sh_s)
        elif kind == 1:
            prm = (gdn_w_in[j], gdn_conv_w[j], gdn_a_log[j], gdn_dt_bias[j], gdn_norm_w[j], gdn_w_o[j])
            xp, cv_p, ss_p = _gdn_layer(gp, xp, g, mod_p[:3], None, None, *prm, GDN_CHUNK)
            xs, cv_s, ss_s = _gdn_layer(gs, xs, g, mod_s[:3], state_gdn_conv[j], state_gdn_ssm[j], *prm,
                                        SAMPLE_CHUNK)
            outs_p["ssm"].append(ss_p); outs_p["conv"].append(cv_p)
            outs_s["ssm"].append(ss_s); outs_s["conv"].append(cv_s)
        elif kind == 2:
            prm = (ret_w_in[j], ret_norm_w[j], ret_w_o[j])
            xp, r_p = _ret_layer(gp, xp, g, mod_p[:3], None, 0, *prm, RET_CHUNK)
            xs, r_s = _ret_layer(gs, xs, g, mod_s[:3], state_ret[j], PAST_LEN, *prm, SAMPLE_CHUNK)
            outs_p["ret"].append(r_p); outs_s["ret"].append(r_s)
        else:
            prm = (i, hgrn_lb_logits, hgrn_w_in[j], hgrn_norm_w[j], hgrn_w_o[j])
            xp, h_p = _hgrn_layer(gp, xp, g, mod_p[:3], None, *prm, HGRN_CHUNK)
            xs, h_s = _hgrn_layer(gs, xs, g, mod_s[:3], state_hgrn[j], *prm, SAMPLE_CHUNK)
            outs_p["hgrn"].append(h_p); outs_s["hgrn"].append(h_s)
        last = i == DEPTH - 1
        xp, xs = _moe(i, [mp, msg], [xp, xs], norm_ffn[i], [mod_p[3:], mod_s[3:]], moe_w_router[i],
                      moe_b_router[i], moe_w_gu, moe_b_gu, moe_w_down, moe_b_down,
                      norm_final if last else None)
    y_prompt = xp[1].reshape(bp, tp, D)
    y_sample = xs[1].reshape(bs, ts, D)
    order = ("wkv", "shift", "ssm", "conv", "ret", "hgrn")
    return ((y_prompt, y_sample) + tuple(jnp.stack(outs_p[k]) for k in order)
            + tuple(jnp.stack(outs_s[k]) for k in order))
```

```python
import functools
import math

import jax
import jax.numpy as jnp
from jax import lax
from jax.experimental import pallas as pl
from jax.experimental.pallas import tpu as pltpu

F32 = jnp.float32
BF16 = jnp.bfloat16
I32 = jnp.int32
HIGHEST = lax.Precision.HIGHEST

D = 1024
DEPTH = 4
NORM_EPS = 1e-6
RW_H, RW_N = 16, 64
RW_GN_EPS = 64e-5
GD_H, GD_DK, GD_DV, GD_CONV = 8, 128, 128, 4
GD_C = 3 * GD_H * GD_DK
RT_H, RT_DK, RT_DV = 4, 256, 512
HG_H, HG_E, HG_DV = 8, 128, 128
N_EXPERTS, TOP_K, D_FF = 32, 4, 1024
SWIGLU_LIMIT, SWIGLU_ALPHA = 7.0, 1.702

LANES = 128
EXPERT_TILE = 512
ROUTER_TILE = 512
TOKEN_BLOCK = 256
VMEM_LIMIT = 56 * 1024 * 1024


def _cparams(sem, vmem=VMEM_LIMIT):
    return pltpu.CompilerParams(dimension_semantics=sem, vmem_limit_bytes=vmem)


def _sigmoid(x):
    return 1.0 / (1.0 + jnp.exp(-x))


def _silu(x):
    return x * _sigmoid(x)


def _softplus(x):
    return jnp.maximum(x, 0.0) + jnp.log(1.0 + jnp.exp(-jnp.abs(x)))


def _modulate(x, g, shift, scale):
    ms = jnp.mean(x * x, axis=-1, keepdims=True)
    return (x * lax.rsqrt(ms + NORM_EPS) * g) * (1.0 + scale) + shift


def _bdot(a, b):
    return jnp.dot(a.astype(BF16), b.astype(BF16), preferred_element_type=F32)


def _bdot_nt(a, b):
    return lax.dot_general(a.astype(BF16), b.astype(BF16), (((1,), (1,)), ((), ())),
                           preferred_element_type=F32)


def _bdot_tn(a, b):
    return lax.dot_general(a.astype(BF16), b.astype(BF16), (((0,), (0,)), ((), ())),
                           preferred_element_type=F32)


def _fdot(a, b):
    return jnp.dot(a, b, precision=HIGHEST, preferred_element_type=F32)


class _Group:
    def __init__(self, b, t, tm):
        self.b, self.t, self.n = b, t, b * t
        self.tm = min(tm, self.n)
        assert self.n % self.tm == 0
        assert (self.t % self.tm == 0) or (self.tm % self.t == 0)
        self.per_batch = self.t % self.tm == 0
        self.tiles = self.n // self.tm

    def rows(self, width):
        return pl.BlockSpec((self.tm, width), lambda i: (i, 0))

    def rowmod(self, arr):
        w = arr.shape[-1]
        if self.per_batch:
            k = self.t // self.tm
            return arr.reshape(self.b, 1, w), pl.BlockSpec((1, 1, w), lambda i: (i // k, 0, 0))
        rep = jnp.repeat(arr, self.t, axis=0).reshape(self.tiles, self.tm, w)
        return rep, pl.BlockSpec((1, self.tm, w), lambda i: (i, 0, 0))

    def rowseq(self, arr):
        w = arr.shape[-1]
        if self.per_batch:
            assert arr.shape[1] == 1
            k = self.t // self.tm
            return arr, pl.BlockSpec((1, 1, w), lambda i: (i // k, 0, 0))
        return arr.reshape(self.tiles, self.tm, w), pl.BlockSpec((1, self.tm, w), lambda i: (i, 0, 0))

    def postab(self, tab):
        w = tab.shape[-1]
        if self.per_batch:
            k = self.t // self.tm
            return tab.reshape(k, self.tm, w), pl.BlockSpec((1, self.tm, w), lambda i: (i % k, 0, 0))
        rep = jnp.tile(tab, (self.tm // self.t, 1)).reshape(1, self.tm, w)
        return rep, pl.BlockSpec((1, self.tm, w), lambda i: (0, 0, 0))

    def prev8(self, width):
        k = self.tm // 8
        return pl.BlockSpec((8, width), lambda i: (jnp.maximum(i * k - 1, 0), 0))


def _full(shape):
    nd = len(shape)
    return pl.BlockSpec(shape, lambda *a: (0,) * nd)


def _tpos(tm, t):
    row = pl.program_id(0) * tm + lax.broadcasted_iota(I32, (tm, 1), 0)
    return row % t


def _stage_rows(scr, cur, prev8):
    scr[0:8, :] = prev8
    scr[8:, :] = cur


def _shifted_rows(scr, d):
    return scr[pl.ds(8 - d, scr.shape[0] - 8), :]


def _ada_kernel(c_ref, w_ref, b_ref, o_ref):
    o_ref[0] = _bdot(_silu(c_ref[...]), w_ref[0]) + b_ref[0]


def _ada_call(c_all, ada_w, ada_b):
    nb = c_all.shape[0]
    tn = 1536
    return pl.pallas_call(
        _ada_kernel,
        out_shape=jax.ShapeDtypeStruct((DEPTH, nb, 6 * D), F32),
        grid=(DEPTH, 6 * D // tn),
        in_specs=[pl.BlockSpec((nb, D), lambda l, j: (0, 0)),
                  pl.BlockSpec((1, D, tn), lambda l, j: (l, 0, j)),
                  pl.BlockSpec((1, 1, tn), lambda l, j: (l, 0, j))],
        out_specs=pl.BlockSpec((1, nb, tn), lambda l, j: (l, 0, j)),
        compiler_params=_cparams(("arbitrary", "arbitrary")),
        name="adaln",
    )(c_all, ada_w, ada_b.reshape(DEPTH, 1, 6 * D))


def _modrows_kernel(x_ref, g_ref, sh_ref, sc_ref, *rest):
    h = _modulate(x_ref[...], g_ref[...], sh_ref[...], sc_ref[...])
    if len(rest) == 2:
        w_ref, o_ref = rest
        o_ref[...] = _bdot(h, w_ref[...])
    else:
        rest[0][...] = h


def _modrows_call(x, g, shift, scale, w=None):
    n = x.shape[0]
    args = [x, g.reshape(1, D), shift, scale]
    specs = [_full((n, D)), _full((1, D)), _full((n, D)), _full((n, D))]
    width = D
    if w is not None:
        args.append(w)
        specs.append(_full(w.shape))
        width = w.shape[1]
    return pl.pallas_call(
        _modrows_kernel,
        out_shape=jax.ShapeDtypeStruct((n, width), F32),
        grid=(1,),
        in_specs=specs,
        out_specs=_full((n, width)),
        compiler_params=_cparams(("arbitrary",)),
        name="modrows",
    )(*args)


def _outproj_kernel(has_mul, x_ref, y_ref, *rest):
    if has_mul:
        m_ref, w_ref, gt_ref, o_ref = rest
        y = y_ref[...].astype(F32) * m_ref[...].astype(F32)
    else:
        w_ref, gt_ref, o_ref = rest
        y = y_ref[...]
    o_ref[...] = x_ref[...] + gt_ref[0] * _bdot(y, w_ref[...])


def _outproj_call(grp, x, y, w_o, gate, mul=None):
    dy = y.shape[1]
    gt, gt_spec = grp.rowmod(gate)
    args = [x, y]
    specs = [grp.rows(D), grp.rows(dy)]
    if mul is not None:
        args.append(mul)
        specs.append(grp.rows(dy))
    args += [w_o, gt]
    specs += [_full(w_o.shape), gt_spec]
    return pl.pallas_call(
        functools.partial(_outproj_kernel, mul is not None),
        out_shape=jax.ShapeDtypeStruct((grp.n, D), F32),
        grid=(grp.tiles,),
        in_specs=specs,
        out_specs=grp.rows(D),
        compiler_params=_cparams(("arbitrary",)),
        name="outproj",
    )(*args)


def _route_rows(x, cin_ref, g_ref, sh_ref, sc_ref, wr_ref, br_ref,
                h_ref, idx_ref, gate_ref, rank_ref, cnt_ref, carry):
    i = pl.program_id(0)

    @pl.when(i == 0)
    def _():
        carry[...] = cin_ref[...]

    tm = x.shape[0]
    h = _modulate(x, g_ref[...], sh_ref[0], sc_ref[0])
    h_ref[...] = h
    logits = _fdot(h, wr_ref[...]) + br_ref[...]
    lane = lax.broadcasted_iota(I32, logits.shape, 1)
    work = logits
    sel = jnp.zeros(logits.shape, jnp.bool_)
    picks, vals = [], []
    for _ in range(TOP_K):
        m = jnp.max(work, axis=-1, keepdims=True)
        idx = jnp.min(jnp.where(work == m, lane, N_EXPERTS), axis=-1, keepdims=True)
        pick = lane == idx
        picks.append((idx, pick))
        vals.append(m)
        sel = jnp.logical_or(sel, pick)
        work = jnp.where(pick, -jnp.inf, work)
    es = [jnp.exp(v - vals[0]) for v in vals]
    denom = es[0] + es[1] + es[2] + es[3]
    self_f = sel.astype(F32)
    tri = (lax.broadcasted_iota(I32, (tm, tm), 0) > lax.broadcasted_iota(I32, (tm, tm), 1))
    local = jnp.dot(tri.astype(BF16), self_f.astype(BF16), preferred_element_type=F32)
    rank = local + carry[...]
    carry[...] = carry[...] + jnp.sum(self_f, axis=0, keepdims=True)
    cnt_ref[...] = carry[...]
    lane_o = lax.broadcasted_iota(I32, (tm, LANES), 1)
    idx_o = jnp.zeros((tm, LANES), I32)
    gate_o = jnp.zeros((tm, LANES), F32)
    rank_o = jnp.zeros((tm, LANES), I32)
    for k in range(TOP_K):
        idx, pick = picks[k]
        rk = jnp.sum(jnp.where(pick, rank, 0.0), axis=-1, keepdims=True)
        idx_o = jnp.where(lane_o == k, idx, idx_o)
        gate_o = jnp.where(lane_o == k, es[k] / denom, gate_o)
        rank_o = jnp.where(lane_o == k, rk.astype(I32), rank_o)
    idx_ref[...] = idx_o
    gate_ref[...] = gate_o
    rank_ref[...] = rank_o


def _router_kernel(cin_ref, x_ref, g_ref, sh_ref, sc_ref, wr_ref, br_ref,
                   h_ref, idx_ref, gate_ref, rank_ref, cnt_ref, carry):
    _route_rows(x_ref[...], cin_ref, g_ref, sh_ref, sc_ref, wr_ref, br_ref,
                h_ref, idx_ref, gate_ref, rank_ref, cnt_ref, carry)


def _router_call(grp, route, x):
    counts_in, g2, shift2, scale2, w_router, b_router = route
    sh, sh_spec = grp.rowmod(shift2)
    sc, sc_spec = grp.rowmod(scale2)
    pad = pl.BlockSpec((grp.tm, LANES), lambda i: (i, 0))
    outs = pl.pallas_call(
        _router_kernel,
        out_shape=(jax.ShapeDtypeStruct((grp.n, D), F32),
                   jax.ShapeDtypeStruct((grp.n, LANES), I32),
                   jax.ShapeDtypeStruct((grp.n, LANES), F32),
                   jax.ShapeDtypeStruct((grp.n, LANES), I32),
                   jax.ShapeDtypeStruct((1, N_EXPERTS), F32)),
        grid=(grp.tiles,),
        in_specs=[_full((1, N_EXPERTS)), grp.rows(D), _full((1, D)), sh_spec, sc_spec,
                  _full((D, N_EXPERTS)), _full((1, N_EXPERTS))],
        out_specs=(grp.rows(D), pad, pad, pad, _full((1, N_EXPERTS))),
        scratch_shapes=[pltpu.VMEM((1, N_EXPERTS), F32)],
        compiler_params=_cparams(("arbitrary",)),
        name="moe_router",
    )(counts_in, x, g2.reshape(1, D), sh, sc, w_router, b_router.reshape(1, N_EXPERTS))
    return (x,) + tuple(outs)


def _dispatch_kernel(cnt_ref, off_ref, nv_ref, pos_ref, h_ref, xs_ref, zbuf, sem, sem_z):
    i = pl.program_id(0)

    for t in range(TOKEN_BLOCK):
        for k in range(TOP_K):
            pltpu.make_async_copy(h_ref.at[pl.ds(t, 1)], xs_ref.at[pl.ds(pos_ref[t * TOP_K + k], 1)],
                                  sem).start(priority=k % 2)
    for k in range(TOP_K):
        pltpu.make_async_copy(h_ref, xs_ref.at[pl.ds(0, TOKEN_BLOCK)], sem).wait()

    @pl.when(i == pl.num_programs(0) - 1)
    def _():
        zbuf[...] = jnp.zeros(zbuf.shape, F32)
        bits = [1 << s for s in range(EXPERT_TILE.bit_length() - 2, 2, -1)]

        def pad_copies(e, wait):
            n = cnt_ref[e]
            start = off_ref[e] + n
            end = off_ref[e] + ((n + EXPERT_TILE - 1) // EXPERT_TILE) * EXPERT_TILE
            head = (-start) & 7

            def one(r, c):
                cp = pltpu.make_async_copy(zbuf.at[pl.ds(0, 1)], xs_ref.at[pl.ds(start + r, 1)], sem_z)
                if wait:
                    cp.wait()
                else:
                    cp.start()
                return c

            lax.fori_loop(0, head, one, 0)
            start8 = start + head
            rem = end - start8
            for bit in bits:
                @pl.when((rem & bit) != 0)
                def _():
                    s = pl.multiple_of(start8 + (rem & ~(2 * bit - 1)), 8)
                    cp = pltpu.make_async_copy(zbuf.at[pl.ds(0, bit)], xs_ref.at[pl.ds(s, bit)], sem_z)
                    if wait:
                        cp.wait()
                    else:
                        cp.start()

        def tail_copy(j):
            return pltpu.make_async_copy(zbuf, xs_ref.at[pl.ds(j * EXPERT_TILE, EXPERT_TILE)], sem_z)

        n_tiles = xs_ref.shape[0] // EXPERT_TILE
        for wait in (False, True):
            def per_expert(e, c):
                pad_copies(e, wait)
                return c

            def per_tail(j, c):
                if wait:
                    tail_copy(j).wait()
                else:
                    tail_copy(j).start()
                return c

            lax.fori_loop(0, N_EXPERTS, per_expert, 0)
            lax.fori_loop(nv_ref[0], n_tiles, per_tail, 0)


def _dispatch_call(counts, offsets, n_valid, pos_flat, h, n_rows):
    n = h.shape[0]
    return pl.pallas_call(
        _dispatch_kernel,
        out_shape=jax.ShapeDtypeStruct((n_rows, D), F32),
        grid_spec=pltpu.PrefetchScalarGridSpec(
            num_scalar_prefetch=3,
            grid=(n // TOKEN_BLOCK,),
            in_specs=[pl.BlockSpec((TOKEN_BLOCK * TOP_K,), lambda i, c, o, v: (i,), memory_space=pltpu.SMEM),
                      pl.BlockSpec((TOKEN_BLOCK, D), lambda i, c, o, v: (i, 0))],
            out_specs=pl.BlockSpec(memory_space=pl.ANY),
            scratch_shapes=[pltpu.VMEM((EXPERT_TILE, D), F32), pltpu.SemaphoreType.DMA(()),
                            pltpu.SemaphoreType.DMA(())]),
        compiler_params=_cparams(("arbitrary",)),
        name="moe_dispatch",
    )(counts, offsets, n_valid, pos_flat, h)


def _expert_kernel(te_ref, nv_ref, x_ref, wgu_ref, bgu_ref, wd_ref, bd_ref, o_ref, wgu_s, wd_s):
    j = pl.program_id(0)
    fresh = jnp.logical_or(j == 0, te_ref[j] != te_ref[jnp.maximum(j - 1, 0)])

    @pl.when(jnp.logical_and(j < nv_ref[0], fresh))
    def _():
        wgu_s[...] = wgu_ref[0, 0].astype(BF16)
        wd_s[...] = wd_ref[0, 0].astype(BF16)

    @pl.when(j < nv_ref[0])
    def _():
        x = x_ref[...].astype(BF16)
        gu = jnp.dot(x, wgu_s[...], preferred_element_type=F32) + bgu_ref[0, 0]
        gl = jnp.minimum(gu[:, :D_FF], SWIGLU_LIMIT)
        up = jnp.clip(gu[:, D_FF:], -SWIGLU_LIMIT, SWIGLU_LIMIT)
        act = (up + 1.0) * gl * _sigmoid(SWIGLU_ALPHA * gl)
        o_ref[...] = jnp.dot(act.astype(BF16), wd_s[...], preferred_element_type=F32) + bd_ref[0, 0]

    @pl.when(j >= nv_ref[0])
    def _():
        o_ref[...] = jnp.zeros(o_ref.shape, F32)


def _expert_call(layer, tile_expert, n_valid, xs, w_gu, b_gu, w_down, b_down):
    n_rows = xs.shape[0]
    g = n_rows // EXPERT_TILE
    return pl.pallas_call(
        _expert_kernel,
        out_shape=jax.ShapeDtypeStruct((n_rows, D), F32),
        grid_spec=pltpu.PrefetchScalarGridSpec(
            num_scalar_prefetch=2,
            grid=(g,),
            in_specs=[pl.BlockSpec((EXPERT_TILE, D), lambda j, te, nv: (jnp.minimum(j, nv[0] - 1), 0)),
                      pl.BlockSpec((1, 1, D, 2 * D_FF), lambda j, te, nv: (layer, te[j], 0, 0)),
                      pl.BlockSpec((1, 1, 1, 2 * D_FF), lambda j, te, nv: (layer, te[j], 0, 0)),
                      pl.BlockSpec((1, 1, D_FF, D), lambda j, te, nv: (layer, te[j], 0, 0)),
                      pl.BlockSpec((1, 1, 1, D), lambda j, te, nv: (layer, te[j], 0, 0))],
            out_specs=pl.BlockSpec((EXPERT_TILE, D), lambda j, te, nv: (j, 0)),
            scratch_shapes=[pltpu.VMEM((D, 2 * D_FF), BF16), pltpu.VMEM((D_FF, D), BF16)]),
        compiler_params=_cparams(("arbitrary",)),
        name="moe_experts",
    )(tile_expert, n_valid, xs, w_gu, b_gu.reshape(-1, N_EXPERTS, 1, 2 * D_FF), w_down,
      b_down.reshape(-1, N_EXPERTS, 1, D))


def _combine_kernel(with_norm, pos_ref, x_ref, gate_ref, gt_ref, ys_ref, *rest):
    if with_norm:
        ng_ref, o_ref, y_ref, buf, sems = rest
    else:
        o_ref, buf, sems = rest
    half = TOKEN_BLOCK // 2
    for hf in range(2):
        for t in range(hf * half, (hf + 1) * half):
            for k in range(TOP_K):
                pltpu.make_async_copy(ys_ref.at[pl.ds(pos_ref[t * TOP_K + k], 1)], buf.at[k, pl.ds(t, 1)],
                                      sems.at[hf]).start(priority=k % 2)
    for hf in range(2):
        rows = pl.ds(hf * half, half)
        for k in range(TOP_K):
            pltpu.make_async_copy(ys_ref.at[pl.ds(0, half)], buf.at[k, rows], sems.at[hf]).wait()
        gate = gate_ref[rows, :]
        f = gate[:, 0:1] * buf[0, rows]
        for k in range(1, TOP_K):
            f = f + gate[:, k:k + 1] * buf[k, rows]
        gt = gt_ref[0]
        x_new = x_ref[rows, :] + (gt if gt.shape[0] == 1 else gt[hf * half:(hf + 1) * half]) * f
        o_ref[rows, :] = x_new
        if with_norm:
            ms = jnp.mean(x_new * x_new, axis=-1, keepdims=True)
            y_ref[rows, :] = x_new * lax.rsqrt(ms + NORM_EPS) * ng_ref[...]


def _combine_call(grp, pos_flat, x, gate_pad, gate2, ys, final_g=None):
    gt, gt_spec = grp.rowmod(gate2)
    assert grp.tm == TOKEN_BLOCK
    with_norm = final_g is not None
    args = [pos_flat, x, gate_pad, gt, ys]
    specs = [pl.BlockSpec((TOKEN_BLOCK * TOP_K,), lambda i: (i,), memory_space=pltpu.SMEM),
             grp.rows(D), grp.rows(LANES), gt_spec, pl.BlockSpec(memory_space=pl.ANY)]
    shape = jax.ShapeDtypeStruct((grp.n, D), F32)
    if with_norm:
        args.append(final_g.reshape(1, D))
        specs.append(_full((1, D)))
    return pl.pallas_call(
        functools.partial(_combine_kernel, with_norm),
        out_shape=(shape, shape) if with_norm else shape,
        grid=(grp.tiles,),
        in_specs=specs,
        out_specs=(grp.rows(D), grp.rows(D)) if with_norm else grp.rows(D),
        scratch_shapes=[pltpu.VMEM((TOP_K, TOKEN_BLOCK, D), F32), pltpu.SemaphoreType.DMA((2,))],
        compiler_params=_cparams(("arbitrary",)),
        name="moe_combine",
    )(*args)


def _finish(grp, route, x, y, w_o, gate, mul=None):
    x_new = _outproj_call(grp, x, y, w_o, gate, mul)
    if route is None:
        return x_new
    return _router_call(_Group(grp.b, grp.t, ROUTER_TILE), route, x_new)


def _moe(layer, groups, routed, gates2, w_gu, b_gu, w_down, b_down, final_g=None):
    counts = routed[-1][5]
    xs_in = [r[0] for r in routed]
    hs = [r[1] for r in routed]
    idxs = [r[2][:, :TOP_K] for r in routed]
    gates = [r[3] for r in routed]
    ranks = [r[4][:, :TOP_K] for r in routed]
    mods = [(None, None, g2) for g2 in gates2]
    h_all = jnp.concatenate(hs, axis=0)
    idx_all = jnp.concatenate(idxs, axis=0)
    rank_all = jnp.concatenate(ranks, axis=0)
    n = h_all.shape[0]
    cnt = counts[0].astype(I32)
    padded = ((cnt + EXPERT_TILE - 1) // EXPERT_TILE) * EXPERT_TILE
    ends = jnp.cumsum(padded)
    offsets = ends - padded
    n_tiles = (n * TOP_K + N_EXPERTS * (EXPERT_TILE - 1)) // EXPERT_TILE
    n_rows = n_tiles * EXPERT_TILE
    pos = (jnp.take(offsets, idx_all) + rank_all).astype(I32)
    pos_flat = pos.reshape(n * TOP_K)
    n_valid = (ends[-1] // EXPERT_TILE).astype(I32)
    tile_start = jnp.arange(n_tiles, dtype=I32) * EXPERT_TILE
    tile_start = jnp.minimum(tile_start, ends[-1] - EXPERT_TILE)
    tile_expert = jnp.sum(tile_start[:, None] >= ends[None, :], axis=1).astype(I32)
    n_valid = n_valid.reshape(1)
    xs = _dispatch_call(cnt, offsets.astype(I32), n_valid, pos_flat, h_all, n_rows)
    ys = _expert_call(layer, tile_expert, n_valid, xs, w_gu, b_gu, w_down, b_down)
    outs = []
    start = 0
    for grp, x, gate, (_, _, gt2) in zip(groups, xs_in, gates, mods):
        cgrp = _Group(grp.b, grp.t, TOKEN_BLOCK)
        p = lax.dynamic_slice_in_dim(pos_flat, start * TOP_K, grp.n * TOP_K)
        outs.append(_combine_call(cgrp, p, x, gate, gt2, ys, final_g))
        start += grp.n
    return outs


def _rwkv_proj_kernel(t_len, x_ref, xp_ref, g_ref, sh_ref, sc_ref, s0_ref, mu_ref, wrkv_ref, w0_ref,
                      w1_ref, w2_ref, a0_ref, a1_ref, a2_ref, g1_ref, g2_ref,
                      r_ref, w_ref, k_ref, v_ref, a_ref, gg_ref, scr):
    tm = x_ref.shape[0]
    g, sh, sc = g_ref[...], sh_ref[0], sc_ref[0]
    h = _modulate(x_ref[...], g, sh, sc)
    hp = _modulate(xp_ref[...], g, sh[0:8] if sh.shape[0] > 1 else sh, sc[0:8] if sc.shape[0] > 1 else sc)
    _stage_rows(scr, h, hp)
    prev = jnp.where(_tpos(tm, t_len) == 0, s0_ref[0], _shifted_rows(scr, 1))
    dx = prev - h
    mu = mu_ref[...]
    xr, xw, xk, xv, xa, xg = [h + dx * mu[n:n + 1] for n in range(6)]
    r_ref[...] = _bdot(xr, wrkv_ref[0]).astype(BF16)
    k_ref[...] = _bdot(xk, wrkv_ref[1]).astype(BF16)
    v_ref[...] = _bdot(xv, wrkv_ref[2]).astype(BF16)
    w_log = -_softplus(-(w0_ref[...] + _bdot(jnp.tanh(_bdot(xw, w1_ref[...])), w2_ref[...]))) - 0.5
    w_ref[...] = jnp.exp(-jnp.exp(w_log))
    a_ref[...] = _sigmoid(a0_ref[...] + _bdot(_bdot(xa, a1_ref[...]), a2_ref[...])).astype(BF16)
    gg_ref[...] = _bdot(_sigmoid(_bdot(xg, g1_ref[...])), g2_ref[...]).astype(BF16)


def _rwkv_core_kernel(r_ref, w_ref, k_ref, v_ref, a_ref, kk_p, ka_p, rk_p, lnw_p, lnb_p, s0_ref,
                      y_ref, st_ref, state, kk_s, b_s, km_s, r_s):
    j = pl.program_id(1)
    tc = r_ref.shape[0]
    n = RW_N

    @pl.when(j == 0)
    def _():
        state[...] = s0_ref[...]

    def step(t, c):
        kt, at, vt, rt = [z[t].astype(F32) for z in (k_ref, a_ref, v_ref, r_ref)]
        r_s[...] = rt
        kk = kt * kk_p[...]
        kk = kk * lax.rsqrt(jnp.sum(kk * kk, axis=0, keepdims=True) + 1e-6)
        km = kt * (1.0 + (at - 1.0) * ka_p[...])
        kk_s[...] = kk
        b_s[...] = kk * at
        km_s[...] = km
        sa = jnp.zeros((n, LANES), F32)
        for kx in range(n):
            sa = sa + state[kx] * kk_s[pl.ds(kx, 1), :]
        y = jnp.zeros((n, LANES), F32)
        for kx in range(n):
            s_new = (state[kx] * w_ref[t, pl.ds(kx, 1), :] - sa * b_s[pl.ds(kx, 1), :]
                     + vt * km_s[pl.ds(kx, 1), :])
            state[kx] = s_new
            y = y + s_new * r_s[pl.ds(kx, 1), :]
        mean = jnp.mean(y, axis=0, keepdims=True)
        yc = y - mean
        var = jnp.mean(yc * yc, axis=0, keepdims=True)
        bonus = jnp.sum(rt * km * rk_p[...], axis=0, keepdims=True) * vt
        y_ref[t] = (yc * lax.rsqrt(var + RW_GN_EPS) * lnw_p[...] + lnb_p[...] + bonus).astype(BF16)
        return c

    lax.fori_loop(0, tc, step, 0)

    @pl.when(j == pl.num_programs(1) - 1)
    def _():
        st_ref[...] = state[...]


def _rwkv_layer(grp, x, norm_g, mods, shift0, wkv0, mu, w_rkv, w0, w1, w2, a0, a1, a2, g1, g2,
                k_k, k_a, r_k, ln_w, ln_b, w_o, route=None):
    b, t = grp.b, grp.t
    shift, scale, gate = mods
    sh, sh_spec = grp.rowmod(shift)
    sc, sc_spec = grp.rowmod(scale)
    if shift0 is None:
        s0 = jnp.zeros((b, 1, D), F32)
    else:
        s0 = jnp.concatenate([shift0[:, None, :], jnp.zeros((b, t - 1, D), F32)], axis=1)
    s0, s0_spec = grp.rowseq(s0)
    bf = lambda z: z.astype(BF16)
    row = lambda z: z.reshape(1, -1)
    weights = [mu, bf(w_rkv), row(w0), bf(w1), bf(w2), row(a0), bf(a1), bf(a2), bf(g1), bf(g2)]
    outs = pl.pallas_call(
        functools.partial(_rwkv_proj_kernel, t),
        out_shape=tuple(jax.ShapeDtypeStruct((grp.n, D), dt) for dt in (BF16, F32, BF16, BF16, BF16, BF16)),
        grid=(grp.tiles,),
        in_specs=[grp.rows(D), grp.prev8(D), _full((1, D)), sh_spec, sc_spec, s0_spec]
        + [_full(z.shape) for z in weights],
        out_specs=tuple(grp.rows(D) for _ in range(6)),
        scratch_shapes=[pltpu.VMEM((grp.tm + 8, D), F32)],
        compiler_params=_cparams(("arbitrary",)),
        name="rwkv_proj",
    )(x, x, row(norm_g), sh, sc, s0, *weights)
    r, w, k, v, a, gg = outs
    bh = b * RW_H

    def to_core(z):
        return z.reshape(b, t, RW_H, RW_N).transpose(1, 3, 0, 2).reshape(t, RW_N, bh)

    def ptile(p):
        return jnp.tile(p.reshape(RW_H, RW_N).T, (1, b))

    if wkv0 is None:
        st0 = jnp.zeros((RW_N, RW_N, bh), F32)
    else:
        st0 = wkv0.transpose(3, 2, 0, 1).reshape(RW_N, RW_N, bh)
    tc = min(t, 16)
    seq = pl.BlockSpec((tc, RW_N, LANES), lambda q, j: (j, 0, q))
    par = pl.BlockSpec((RW_N, LANES), lambda q, j: (0, q))
    stt = pl.BlockSpec((RW_N, RW_N, LANES), lambda q, j: (0, 0, q))
    y, st = pl.pallas_call(
        _rwkv_core_kernel,
        out_shape=(jax.ShapeDtypeStruct((t, RW_N, bh), BF16),
                   jax.ShapeDtypeStruct((RW_N, RW_N, bh), F32)),
        grid=(bh // LANES, t // tc),
        in_specs=[seq] * 5 + [par] * 5 + [stt],
        out_specs=(seq, stt),
        scratch_shapes=[pltpu.VMEM((RW_N, RW_N, LANES), F32)] + [pltpu.VMEM((RW_N, LANES), F32)] * 4,
        compiler_params=_cparams(("arbitrary", "arbitrary")),
        name="rwkv_core",
    )(to_core(r), to_core(w), to_core(k), to_core(v), to_core(a),
      ptile(k_k), ptile(k_a), ptile(r_k.reshape(-1)), ptile(ln_w), ptile(ln_b), st0)
    y_rows = y.reshape(t, RW_N, b, RW_H).transpose(2, 0, 3, 1).reshape(grp.n, D)
    x_new = _finish(grp, route, x, y_rows, bf(w_o), gate, mul=gg)
    new_wkv = st.reshape(RW_N, RW_N, b, RW_H).transpose(2, 3, 1, 0)
    x_last = x.reshape(b, t, D)[:, -1]
    new_shift = _modrows_call(x_last, norm_g, shift, scale)
    return x_new, new_shift, new_wkv


def _pad_time(z, b, t, tp):
    if tp == t:
        return z
    w = z.shape[-1]
    return jnp.pad(z.reshape(b, t, w), ((0, 0), (0, tp - t), (0, 0))).reshape(b * tp, w)


def _unpad_time(z, b, t, tp):
    if tp == t:
        return z
    w = z.shape[-1]
    return z.reshape(b, tp, w)[:, :t].reshape(b * t, w)


def _gdn_proj_kernel(t_len, x_ref, xp_ref, g_ref, sh_ref, sc_ref, c1_ref, c2_ref, c3_ref, wqkv_ref,
                     wz_ref, wb_ref, wa_ref, cw_ref, alog_ref, dtb_ref,
                     qkv_ref, z_ref, beta_ref, gdec_ref, scr):
    tm = x_ref.shape[0]
    g, sh, sc = g_ref[...], sh_ref[0], sc_ref[0]
    h = _modulate(x_ref[...], g, sh, sc)
    hp = _modulate(xp_ref[...], g, sh[0:8] if sh.shape[0] > 1 else sh, sc[0:8] if sc.shape[0] > 1 else sc)
    hb = h.astype(BF16)
    pre = jnp.dot(hb, wqkv_ref[...], preferred_element_type=F32)
    pre8 = _bdot(hp, wqkv_ref[...])
    tpos = _tpos(tm, t_len)
    cw = cw_ref[...]
    conv = pre * cw[3:4]
    _stage_rows(scr, pre, pre8)
    for d, cref in ((1, c1_ref), (2, c2_ref), (3, c3_ref)):
        past = jnp.where(tpos >= d, _shifted_rows(scr, d), cref[0])
        conv = conv + past * cw[3 - d:4 - d]
    act = _silu(conv)
    nh = GD_H
    for hh in range(2 * nh):
        sl = slice(hh * GD_DK, (hh + 1) * GD_DK)
        seg = act[:, sl]
        seg = seg * lax.rsqrt(jnp.sum(seg * seg, axis=-1, keepdims=True) + 1e-6)
        if hh < nh:
            seg = seg * (GD_DK ** -0.5)
        qkv_ref[:, sl] = seg.astype(BF16)
    qkv_ref[:, 2 * nh * GD_DK:] = act[:, 2 * nh * GD_DK:].astype(BF16)
    z_ref[...] = jnp.dot(hb, wz_ref[...], preferred_element_type=F32).astype(BF16)
    beta_ref[...] = _sigmoid(jnp.dot(hb, wb_ref[...], preferred_element_type=F32))
    a_logit = jnp.dot(hb, wa_ref[...], preferred_element_type=F32)
    gdec_ref[...] = -jnp.exp(alog_ref[...]) * _softplus(a_logit + dtb_ref[...])


def _unit_lower_inverse(a, eye, masks):
    blk8, offs = masks
    n = range(len(a))
    a8 = [jnp.where(blk8, a[i], 0.0) for i in n]
    x = [eye - a8[i] for i in n]
    y = [_bdot(a8[i], a8[i]) for i in n]
    x = [x[i] + _bdot(x[i], y[i]) for i in n]
    y = [_bdot(y[i], y[i]) for i in n]
    x = [x[i] + _bdot(x[i], y[i]) for i in n]
    for off in offs:
        t = [_bdot(jnp.where(off, a[i], 0.0), x[i]) for i in n]
        x = [x[i] - _bdot(x[i], t[i]) for i in n]
    return x


def _inverse_masks(c):
    ri = lax.broadcasted_iota(I32, (c, c), 0)
    ci = lax.broadcasted_iota(I32, (c, c), 1)
    sr = lambda z, s: lax.shift_right_logical(z, jnp.full(z.shape, s, I32))
    blk8 = sr(ri, 3) == sr(ci, 3)
    offs = []
    m, lg = 8, 3
    while m < c:
        same = sr(ri, lg + 1) == sr(ci, lg + 1)
        lower = jnp.logical_and((sr(ri, lg) & 1) == 1, (sr(ci, lg) & 1) == 0)
        offs.append(jnp.logical_and(same, lower))
        m, lg = m * 2, lg + 1
    return ri, ci, (blk8, offs)


def _gdn_core_kernel(q_ref, k_ref, v_ref, z_ref, beta_ref, g_ref, s0_ref, nw_ref, y_ref, st_ref, state):
    cidx = pl.program_id(1)
    nb, c = q_ref.shape[0], q_ref.shape[1]

    @pl.when(cidx == 0)
    def _():
        state[...] = s0_ref[...]

    ri, ci, masks = _inverse_masks(c)
    incl = ri >= ci
    strict = ri > ci
    eye = (ri == ci).astype(F32)
    incl_f, incl_t = incl.astype(F32), (ci >= ri).astype(F32)
    nw = nw_ref[...]
    sls = [slice(h * GD_DK, (h + 1) * GD_DK) for h in range(GD_H)]
    pairs = [(s, h) for s in range(nb) for h in range(GD_H)]
    n = range(len(pairs))
    g = [g_ref[s] for s in range(nb)]
    cum = [_fdot(incl_f, g[s]) for s in range(nb)]
    cum_t = [lax.dot_general(g[s], incl_t, (((0,), (0,)), ((), ())), precision=HIGHEST,
                             preferred_element_type=F32) for s in range(nb)]
    beta = [beta_ref[s] for s in range(nb)]
    q = [q_ref[s, :, sls[h]] for s, h in pairs]
    k = [k_ref[s, :, sls[h]] for s, h in pairs]
    kf = [z.astype(F32) for z in k]
    v = [v_ref[s, :, sls[h]].astype(F32) for s, h in pairs]
    st = [state[s, h] for s, h in pairs]
    cum_c = [cum[s][:, h:h + 1] for s, h in pairs]
    dec = [jnp.where(incl, jnp.exp(jnp.where(incl, cum_c[i] - cum_t[s][h:h + 1, :], 0.0)), 0.0)
           for i, (s, h) in enumerate(pairs)]
    bcol = [beta[s][:, h:h + 1] for s, h in pairs]
    kb = [kf[i] * bcol[i] for i in n]
    a = [jnp.where(strict, _bdot_nt(kb[i], k[i]) * dec[i], 0.0) for i in n]
    attn = [_bdot_nt(q[i], k[i]) * dec[i] for i in n]
    x = _unit_lower_inverse(a, eye, masks)
    ecum = [jnp.exp(cum_c[i]) for i in n]
    sol = [_bdot(x[i], jnp.concatenate([v[i] * bcol[i], kb[i] * ecum[i]], axis=1)) for i in n]
    u = [sol[i][:, :GD_DV] - _bdot(sol[i][:, GD_DV:], st[i]) for i in n]
    o = [_bdot(q[i].astype(F32) * ecum[i], st[i]) + _bdot(attn[i], u[i]) for i in n]
    last = [cum[s][c - 1:c, h:h + 1] for s, h in pairs]
    s_new = [st[i] * jnp.exp(last[i]) + _bdot_tn(kf[i] * jnp.exp(last[i] - cum_c[i]), u[i]) for i in n]
    for i, (s, h) in enumerate(pairs):
        state[s, h] = s_new[i]
        on = o[i] * lax.rsqrt(jnp.mean(o[i] * o[i], axis=-1, keepdims=True) + NORM_EPS) * nw
        y_ref[s, :, sls[h]] = (on * _silu(z_ref[s, :, sls[h]].astype(F32))).astype(BF16)

    @pl.when(cidx == pl.num_programs(1) - 1)
    def _():
        st_ref[...] = state[...]


def _gdn_layer(grp, x, norm_g, mods, conv0, ssm0, w_in, conv_w, a_log, dt_bias, norm_w, w_o, chunk,
               seqs_per_step=1, route=None):
    b, t = grp.b, grp.t
    shift, scale, gate = mods
    sh, sh_spec = grp.rowmod(shift)
    sc, sc_spec = grp.rowmod(scale)
    kd = GD_H * GD_DK
    cstates, cspecs = [], []
    for d in (1, 2, 3):
        if conv0 is None:
            cs = jnp.zeros((b, 1, GD_C), F32)
        else:
            cs = jnp.concatenate([conv0[:, 3 - d:, :], jnp.zeros((b, t - d, GD_C), F32)], axis=1)
        cs, spec = grp.rowseq(cs)
        cstates.append(cs)
        cspecs.append(spec)
    bf = lambda z: z.astype(BF16)
    pad128 = lambda z: jnp.pad(z, ((0, 0), (0, LANES - z.shape[1])))
    w_qkv = bf(w_in[:, :GD_C])
    w_z = bf(w_in[:, GD_C:GD_C + kd])
    w_b = bf(pad128(w_in[:, GD_C + kd:GD_C + kd + GD_H]))
    w_a = bf(pad128(w_in[:, GD_C + kd + GD_H:]))
    weights = [w_qkv, w_z, w_b, w_a, conv_w, pad128(a_log.reshape(1, GD_H)), pad128(dt_bias.reshape(1, GD_H))]
    qkv, z, beta, gdec = pl.pallas_call(
        functools.partial(_gdn_proj_kernel, t),
        out_shape=(jax.ShapeDtypeStruct((grp.n, GD_C), BF16), jax.ShapeDtypeStruct((grp.n, kd), BF16),
                   jax.ShapeDtypeStruct((grp.n, LANES), F32), jax.ShapeDtypeStruct((grp.n, LANES), F32)),
        grid=(grp.tiles,),
        in_specs=[grp.rows(D), grp.prev8(D), _full((1, D)), sh_spec, sc_spec] + cspecs
        + [_full(z_.shape) for z_ in weights],
        out_specs=(grp.rows(GD_C), grp.rows(kd), grp.rows(LANES), grp.rows(LANES)),
        scratch_shapes=[pltpu.VMEM((grp.tm + 8, GD_C), F32)],
        compiler_params=_cparams(("arbitrary",)),
        name="gdn_proj",
    )(x, x, norm_g.reshape(1, D), sh, sc, *cstates, *weights)
    tp = ((t + chunk - 1) // chunk) * chunk
    nc = tp // chunk
    qkv_p, z_p = _pad_time(qkv, b, t, tp), _pad_time(z, b, t, tp)
    beta_p, g_p = _pad_time(beta, b, t, tp), _pad_time(gdec, b, t, tp)
    if ssm0 is None:
        ssm0 = jnp.zeros((b, GD_H, GD_DK, GD_DV), F32)
    nb = seqs_per_step
    assert b % nb == 0
    seq3 = lambda z_: z_.reshape(b, tp, z_.shape[-1])
    col = lambda j: pl.BlockSpec((nb, chunk, kd), lambda bi, c: (bi, c, j))
    lan = pl.BlockSpec((nb, chunk, LANES), lambda bi, c: (bi, c, 0))
    stt = pl.BlockSpec((nb, GD_H, GD_DK, GD_DV), lambda bi, c: (bi, 0, 0, 0))
    qkv3 = seq3(qkv_p)
    y, st = pl.pallas_call(
        _gdn_core_kernel,
        out_shape=(jax.ShapeDtypeStruct((b, tp, kd), BF16),
                   jax.ShapeDtypeStruct((b, GD_H, GD_DK, GD_DV), F32)),
        grid=(b // nb, nc),
        in_specs=[col(0), col(1), col(2), col(0), lan, lan, stt, _full((1, GD_DV))],
        out_specs=(col(0), stt),
        scratch_shapes=[pltpu.VMEM((nb, GD_H, GD_DK, GD_DV), F32)],
        compiler_params=_cparams(("arbitrary", "arbitrary")),
        name="gdn_core",
    )(qkv3, qkv3, qkv3, seq3(z_p), seq3(beta_p), seq3(g_p), ssm0, norm_w.reshape(1, GD_DV))
    y = y.reshape(b * tp, kd)
    x_new = _finish(grp, route, x, _unpad_time(y, b, t, tp), bf(w_o), gate)
    nl = min(t, GD_CONV - 1)
    x_last = x.reshape(b, t, D)[:, t - nl:].reshape(b * nl, D)
    rep = lambda m: jnp.repeat(m, nl, axis=0)
    pre_last = _modrows_call(x_last, norm_g, rep(shift), rep(scale), w_qkv).reshape(b, nl, GD_C)
    if nl < GD_CONV - 1:
        pre_last = jnp.concatenate([conv0[:, nl:], pre_last], axis=1)
    return x_new, pre_last, st


def _ret_proj_kernel(x_ref, g_ref, sh_ref, sc_ref, cos_ref, sin_ref, w_ref, q_ref, k_ref, v_ref, gate_ref):
    h = _modulate(x_ref[...], g_ref[...], sh_ref[0], sc_ref[0]).astype(BF16)
    kd = RT_H * RT_DK
    vd = RT_H * RT_DV
    cos, sin = cos_ref[0], sin_ref[0]
    even = (lax.broadcasted_iota(I32, (1, kd), 1) & 1) == 0

    def rotary(z):
        swapped = jnp.where(even, pltpu.roll(z, kd - 1, 1), pltpu.roll(z, 1, 1))
        return z * cos + swapped * sin

    q_ref[...] = rotary(jnp.dot(h, w_ref[:, 0:kd], preferred_element_type=F32)).astype(BF16)
    k = rotary(jnp.dot(h, w_ref[:, kd:2 * kd], preferred_element_type=F32))
    k_ref[...] = (k * (RT_DK ** -0.5)).astype(BF16)
    v_ref[...] = jnp.dot(h, w_ref[:, 2 * kd:2 * kd + vd], preferred_element_type=F32).astype(BF16)
    gate_ref[...] = jnp.dot(h, w_ref[:, 2 * kd + vd:], preferred_element_type=F32).astype(BF16)


def _ret_core_kernel(q_ref, k_ref, v_ref, gate_ref, dm_ref, qd_ref, kd_ref, cd_ref, s0_ref, nw_ref,
                     y_ref, st_ref, state):
    cidx = pl.program_id(1)

    @pl.when(cidx == 0)
    def _():
        state[...] = s0_ref[0]

    heads = range(RT_H)
    kss = [slice(h * RT_DK, (h + 1) * RT_DK) for h in heads]
    vss = [slice(h * RT_DV, (h + 1) * RT_DV) for h in heads]
    q = [q_ref[:, ks] for ks in kss]
    k = [k_ref[:, ks] for ks in kss]
    v = [v_ref[:, vs] for vs in vss]
    s = [state[h] for h in heads]
    inner = [_bdot_nt(q[h], k[h]) * dm_ref[h] for h in heads]
    cross = [_bdot(q[h], s[h]) * qd_ref[h] for h in heads]
    o = [_bdot(inner[h], v[h]) + cross[h] for h in heads]
    s_new = [s[h] * cd_ref[h] + _bdot_tn(k[h].astype(F32) * kd_ref[h], v[h]) for h in heads]
    for h in heads:
        state[h] = s_new[h]
        on = o[h] * lax.rsqrt(jnp.mean(o[h] * o[h], axis=-1, keepdims=True) + NORM_EPS) * nw_ref[:, vss[h]]
        y_ref[:, vss[h]] = (on * _silu(gate_ref[:, vss[h]].astype(F32))).astype(BF16)

    @pl.when(cidx == pl.num_programs(1) - 1)
    def _():
        st_ref[0] = state[...]


def _ret_layer(grp, x, norm_g, mods, s0, pos0, w_in, norm_w, w_o, chunk, route=None):
    b, t = grp.b, grp.t
    shift, scale, gate = mods
    sh, sh_spec = grp.rowmod(shift)
    sc, sc_spec = grp.rowmod(scale)
    kd, vd = RT_H * RT_DK, RT_H * RT_DV
    half = RT_DK // 2
    inv = 1.0 / (10000.0 ** jnp.linspace(0.0, 1.0, half, dtype=F32))
    pos = jnp.arange(t, dtype=F32) + float(pos0)
    ang = pos[:, None] * inv[None, :]
    cos = jnp.repeat(jnp.cos(ang), 2, axis=1)
    sin = jnp.stack([-jnp.sin(ang), jnp.sin(ang)], axis=-1).reshape(t, RT_DK)
    cos4, cos_spec = grp.postab(jnp.tile(cos, (1, RT_H)))
    sin4, sin_spec = grp.postab(jnp.tile(sin, (1, RT_H)))
    wb = w_in.astype(BF16)
    q, k, v, gt = pl.pallas_call(
        _ret_proj_kernel,
        out_shape=(jax.ShapeDtypeStruct((grp.n, kd), BF16), jax.ShapeDtypeStruct((grp.n, kd), BF16),
                   jax.ShapeDtypeStruct((grp.n, vd), BF16), jax.ShapeDtypeStruct((grp.n, vd), BF16)),
        grid=(grp.tiles,),
        in_specs=[grp.rows(D), _full((1, D)), sh_spec, sc_spec, cos_spec, sin_spec, _full(wb.shape)],
        out_specs=(grp.rows(kd), grp.rows(kd), grp.rows(vd), grp.rows(vd)),
        compiler_params=_cparams(("arbitrary",)),
        name="ret_proj",
    )(x, norm_g.reshape(1, D), sh, sc, cos4, sin4, wb)
    tp = ((t + chunk - 1) // chunk) * chunk
    nc = tp // chunk
    nv = min(t, chunk)
    assert tp == t or nc == 1
    log_gamma = jnp.log1p(-jnp.exp2(-5.0 - jnp.arange(RT_H, dtype=F32)))
    idx = jnp.arange(chunk, dtype=F32)
    diff = idx[:, None] - idx[None, :]
    dmask = jnp.where(diff >= 0, jnp.exp(log_gamma[:, None, None] * jnp.maximum(diff, 0.0)), 0.0)
    q_dec = jnp.exp(log_gamma[:, None] * (idx + 1.0))[:, :, None]
    k_dec = jnp.exp(log_gamma[:, None] * jnp.maximum(nv - 1.0 - idx, 0.0))[:, :, None]
    c_dec = jnp.exp(log_gamma * nv)[:, None, None]
    if s0 is None:
        s0 = jnp.zeros((b, RT_H, RT_DK, RT_DV), F32)
    rowk = pl.BlockSpec((chunk, kd), lambda bi, c: (bi * nc + c, 0))
    rowv = pl.BlockSpec((chunk, vd), lambda bi, c: (bi * nc + c, 0))
    stt = pl.BlockSpec((1, RT_H, RT_DK, RT_DV), lambda bi, c: (bi, 0, 0, 0))
    y, st = pl.pallas_call(
        _ret_core_kernel,
        out_shape=(jax.ShapeDtypeStruct((b * tp, vd), BF16),
                   jax.ShapeDtypeStruct((b, RT_H, RT_DK, RT_DV), F32)),
        grid=(b, nc),
        in_specs=[rowk, rowk, rowv, rowv, _full(dmask.shape), _full(q_dec.shape), _full(k_dec.shape),
                  _full(c_dec.shape), stt, _full((1, vd))],
        out_specs=(rowv, stt),
        scratch_shapes=[pltpu.VMEM((RT_H, RT_DK, RT_DV), F32)],
        compiler_params=_cparams(("arbitrary", "arbitrary")),
        name="ret_core",
    )(_pad_time(q, b, t, tp), _pad_time(k, b, t, tp), _pad_time(v, b, t, tp), _pad_time(gt, b, t, tp),
      dmask, q_dec, k_dec, c_dec, s0, norm_w.reshape(1, vd))
    x_new = _finish(grp, route, x, _unpad_time(y, b, t, tp), w_o.astype(BF16), gate)
    return x_new, st


def _hgrn_proj_kernel(layer, x_ref, g_ref, sh_ref, sc_ref, lbl_ref, w_ref,
                      q_ref, k_ref, lf_ref, v_ref, gate_ref):
    h = _modulate(x_ref[...], g_ref[...], sh_ref[0], sc_ref[0]).astype(BF16)
    ed = HG_H * HG_E
    logits = lbl_ref[...]
    e = jnp.exp(logits - jnp.max(logits, axis=0, keepdims=True))
    lrow = lax.broadcasted_iota(I32, logits.shape, 0)
    part = jnp.where(jnp.logical_and(lrow >= 1, lrow <= layer), e, 0.0)
    lb = jnp.sum(part, axis=0, keepdims=True) / jnp.sum(e, axis=0, keepdims=True)
    q_ref[...] = jnp.dot(h, w_ref[:, 0:ed], preferred_element_type=F32)
    f = lb + (1.0 - lb) * _sigmoid(jnp.dot(h, w_ref[:, ed:2 * ed], preferred_element_type=F32))
    k_ref[...] = 1.0 - f
    lf_ref[...] = jnp.log(f)
    v_ref[...] = jnp.dot(h, w_ref[:, 2 * ed:3 * ed], preferred_element_type=F32).astype(BF16)
    gate_ref[...] = jnp.dot(h, w_ref[:, 3 * ed:], preferred_element_type=F32).astype(BF16)


def _hgrn_core_kernel(q_ref, k_ref, lf_ref, v_ref, gate_ref, s0_ref, nw_ref, y_ref, st_ref, state):
    cidx = pl.program_id(1)
    nb, c = q_ref.shape[0], q_ref.shape[1]

    @pl.when(cidx == 0)
    def _():
        state[...] = s0_ref[...]

    ri = lax.broadcasted_iota(I32, (c, c), 0)
    ci = lax.broadcasted_iota(I32, (c, c), 1)
    ltri = (ri >= ci).astype(F32)
    row8 = lax.broadcasted_iota(I32, (8, 1), 0)
    hsl = [slice(h * HG_E, (h + 1) * HG_E) for h in range(HG_H)]
    pairs = [(s, h) for s in range(nb) for h in range(HG_H)]
    heads = range(len(pairs))
    sls = [hsl[h] for _, h in pairs]
    q = [q_ref[s, :, hsl[h]] for s, h in pairs]
    k = [k_ref[s, :, hsl[h]] for s, h in pairs]
    v = [v_ref[s, :, hsl[h]].astype(F32) for s, h in pairs]
    st = [state[s, h] for s, h in pairs]
    cum = [_fdot(ltri, lf_ref[s, :, hsl[h]]) for s, h in pairs]
    inter = [_bdot_nt(q[h] * jnp.exp(cum[h]), st[h]) for h in heads]
    last = [cum[h][c - 1:c, :] for h in heads]
    s_new = [st[h] * jnp.exp(last[h]) + _bdot_tn(v[h], k[h] * jnp.exp(last[h] - cum[h])) for h in heads]
    for h in heads:
        seq = pairs[h][0]
        state[seq, pairs[h][1]] = s_new[h]
        parts = []
        for g0 in range(0, c, 8):
            qg, cg = q[h][g0:g0 + 8], cum[h][g0:g0 + 8]
            acc = inter[h][g0:g0 + 8]
            for j in range(g0 + 8):
                diff = cg - cum[h][j:j + 1, :]
                if j >= g0:
                    causal = row8 >= (j - g0)
                    diff = jnp.where(causal, diff, 0.0)
                col = jnp.sum(qg * k[h][j:j + 1, :] * jnp.exp(diff), axis=-1, keepdims=True)
                if j >= g0:
                    col = jnp.where(causal, col, 0.0)
                acc = acc + col * v[h][j:j + 1, :]
            parts.append(acc)
        o = parts[0] if len(parts) == 1 else jnp.concatenate(parts, axis=0)
        on = o * lax.rsqrt(jnp.mean(o * o, axis=-1, keepdims=True) + NORM_EPS) * nw_ref[:, sls[h]]
        y_ref[seq, :, sls[h]] = (on * _silu(gate_ref[seq, :, sls[h]].astype(F32))).astype(BF16)

    @pl.when(cidx == pl.num_programs(1) - 1)
    def _():
        st_ref[...] = state[...]


def _hgrn_layer(grp, x, norm_g, mods, s0, layer, lb_logits, w_in, norm_w, w_o, chunk, seqs_per_step=1,
                route=None):
    b, t = grp.b, grp.t
    shift, scale, gate = mods
    sh, sh_spec = grp.rowmod(shift)
    sc, sc_spec = grp.rowmod(scale)
    ed, vd = HG_H * HG_E, HG_H * HG_DV
    wb = w_in.astype(BF16)
    q, k, lf, v, gt = pl.pallas_call(
        functools.partial(_hgrn_proj_kernel, layer),
        out_shape=(jax.ShapeDtypeStruct((grp.n, ed), F32), jax.ShapeDtypeStruct((grp.n, ed), F32),
                   jax.ShapeDtypeStruct((grp.n, ed), F32), jax.ShapeDtypeStruct((grp.n, vd), BF16),
                   jax.ShapeDtypeStruct((grp.n, vd), BF16)),
        grid=(grp.tiles,),
        in_specs=[grp.rows(D), _full((1, D)), sh_spec, sc_spec, _full(lb_logits.shape), _full(wb.shape)],
        out_specs=(grp.rows(ed), grp.rows(ed), grp.rows(ed), grp.rows(vd), grp.rows(vd)),
        compiler_params=_cparams(("arbitrary",)),
        name="hgrn_proj",
    )(x, norm_g.reshape(1, D), sh, sc, lb_logits, wb)
    tp = ((t + chunk - 1) // chunk) * chunk
    nc = tp // chunk
    if s0 is None:
        s0 = jnp.zeros((b, HG_H, HG_E, HG_DV), F32)
    nb = seqs_per_step
    assert b % nb == 0
    seq3 = lambda z_: _pad_time(z_, b, t, tp).reshape(b, tp, z_.shape[-1])
    row = pl.BlockSpec((nb, chunk, ed), lambda bi, c: (bi, c, 0))
    stt = pl.BlockSpec((nb, HG_H, HG_E, HG_DV), lambda bi, c: (bi, 0, 0, 0))
    y, st = pl.pallas_call(
        _hgrn_core_kernel,
        out_shape=(jax.ShapeDtypeStruct((b, tp, vd), BF16),
                   jax.ShapeDtypeStruct((b, HG_H, HG_E, HG_DV), F32)),
        grid=(b // nb, nc),
        in_specs=[row, row, row, row, row, stt, _full((1, vd))],
        out_specs=(row, stt),
        scratch_shapes=[pltpu.VMEM((nb, HG_H, HG_E, HG_DV), F32)],
        compiler_params=_cparams(("arbitrary", "arbitrary")),
        name="hgrn_core",
    )(seq3(q), seq3(k), seq3(lf), seq3(v), seq3(gt), jnp.swapaxes(s0, 2, 3), norm_w.reshape(1, vd))
    y = y.reshape(b * tp, vd)
    x_new = _finish(grp, route, x, _unpad_time(y, b, t, tp), w_o.astype(BF16), gate)
    return x_new, jnp.swapaxes(st, 2, 3)


ROW_TILE = 256
GDN_CHUNK, RET_CHUNK, HGRN_CHUNK = 64, 128, 16
SAMPLE_CHUNK = 16
GDN_SEQS, HGRN_SEQS = (4, 8), (4, 4)
PAST_LEN = 16384


def kernel(x_prompt, x_sample, c_prompt, c_sample, state_rwkv_wkv, state_rwkv_shift, state_gdn_ssm, state_gdn_conv, state_ret, state_hgrn, ada_w, ada_b, norm_mix, norm_ffn, norm_final, rwkv_mu, rwkv_w_rkv, rwkv_w0, rwkv_w1, rwkv_w2, rwkv_a0, rwkv_a1, rwkv_a2, rwkv_g1, rwkv_g2, rwkv_k_k, rwkv_k_a, rwkv_r_k, rwkv_ln_w, rwkv_ln_b, rwkv_w_o, gdn_w_in, gdn_conv_w, gdn_a_log, gdn_dt_bias, gdn_norm_w, gdn_w_o, ret_w_in, ret_norm_w, ret_w_o, hgrn_w_in, hgrn_lb_logits, hgrn_norm_w, hgrn_w_o, moe_w_router, moe_b_router, moe_w_gu, moe_b_gu, moe_w_down, moe_b_down):
    bp, tp, _ = x_prompt.shape
    bs, ts, _ = x_sample.shape
    gp, gs = _Group(bp, tp, ROW_TILE), _Group(bs, ts, ROW_TILE)
    ada = _ada_call(jnp.concatenate([c_prompt, c_sample], axis=0), ada_w, ada_b)
    xp = x_prompt.reshape(bp * tp, D)
    xs = x_sample.reshape(bs * ts, D)
    outs_p = {k: [] for k in ("wkv", "shift", "ssm", "conv", "ret", "hgrn")}
    outs_s = {k: [] for k in ("wkv", "shift", "ssm", "conv", "ret", "hgrn")}
    for i in range(DEPTH):
        kind, j = i % 4, i // 4
        m = ada[i].reshape(bp + bs, 6, D)
        mod_p = [m[:bp, n] for n in range(6)]
        mod_s = [m[bp:, n] for n in range(6)]
        g = norm_mix[i]
        rt_p = (norm_ffn[i], mod_p[3], mod_p[4], moe_w_router[i], moe_b_router[i])
        rt_s = (norm_ffn[i], mod_s[3], mod_s[4], moe_w_router[i], moe_b_router[i])
        route_p = (jnp.zeros((1, N_EXPERTS), F32),) + rt_p
        if kind == 0:
            prm = (rwkv_mu[j], rwkv_w_rkv[j], rwkv_w0[j], rwkv_w1[j], rwkv_w2[j], rwkv_a0[j], rwkv_a1[j],
                   rwkv_a2[j], rwkv_g1[j], rwkv_g2[j], rwkv_k_k[j], rwkv_k_a[j], rwkv_r_k[j],
                   rwkv_ln_w[j], rwkv_ln_b[j], rwkv_w_o[j])
            xp, sh_p, wkv_p = _rwkv_layer(gp, xp, g, mod_p[:3], None, None, *prm, route=route_p)
            xs, sh_s, wkv_s = _rwkv_layer(gs, xs, g, mod_s[:3], state_rwkv_shift[j], state_rwkv_wkv[j], *prm,
                                          route=(xp[5],) + rt_s)
            outs_p["wkv"].append(wkv_p); outs_p["shift"].append(sh_p)
            outs_s["wkv"].append(wkv_s); outs_s["shift"].append(sh_s)
        elif kind == 1:
            prm = (gdn_w_in[j], gdn_conv_w[j], gdn_a_log[j], gdn_dt_bias[j], gdn_norm_w[j], gdn_w_o[j])
            xp, cv_p, ss_p = _gdn_layer(gp, xp, g, mod_p[:3], None, None, *prm, GDN_CHUNK, GDN_SEQS[0],
                                        route=route_p)
            xs, cv_s, ss_s = _gdn_layer(gs, xs, g, mod_s[:3], state_gdn_conv[j], state_gdn_ssm[j], *prm,
                                        SAMPLE_CHUNK, GDN_SEQS[1], route=(xp[5],) + rt_s)
            outs_p["ssm"].append(ss_p); outs_p["conv"].append(cv_p)
            outs_s["ssm"].append(ss_s); outs_s["conv"].append(cv_s)
        elif kind == 2:
            prm = (ret_w_in[j], ret_norm_w[j], ret_w_o[j])
            xp, r_p = _ret_layer(gp, xp, g, mod_p[:3], None, 0, *prm, RET_CHUNK, route=route_p)
            xs, r_s = _ret_layer(gs, xs, g, mod_s[:3], state_ret[j], PAST_LEN, *prm, SAMPLE_CHUNK,
                                  route=(xp[5],) + rt_s)
            outs_p["ret"].append(r_p); outs_s["ret"].append(r_s)
        else:
            prm = (i, hgrn_lb_logits, hgrn_w_in[j], hgrn_norm_w[j], hgrn_w_o[j])
            xp, h_p = _hgrn_layer(gp, xp, g, mod_p[:3], None, *prm, HGRN_CHUNK, HGRN_SEQS[0], route=route_p)
            xs, h_s = _hgrn_layer(gs, xs, g, mod_s[:3], state_hgrn[j], *prm, SAMPLE_CHUNK, HGRN_SEQS[1],
                                   route=(xp[5],) + rt_s)
            outs_p["hgrn"].append(h_p); outs_s["hgrn"].append(h_s)
        last = i == DEPTH - 1
        xp, xs = _moe(i, [gp, gs], [xp, xs], [mod_p[5], mod_s[5]], moe_w_gu, moe_b_gu, moe_w_down, moe_b_down,
                      norm_final if last else None)
    y_prompt = xp[1].reshape(bp, tp, D)
    y_sample = xs[1].reshape(bs, ts, D)
    order = ("wkv", "shift", "ssm", "conv", "ret", "hgrn")
    return ((y_prompt, y_sample) + tuple(jnp.stack(outs_p[k]) for k in order)
            + tuple(jnp.stack(outs_s[k]) for k in order))
```

```python
import functools
import math

import jax
import jax.numpy as jnp
from jax import lax
from jax.experimental import pallas as pl
from jax.experimental.pallas import tpu as pltpu

F32 = jnp.float32
BF16 = jnp.bfloat16
I32 = jnp.int32
HIGHEST = lax.Precision.HIGHEST

D = 1024
DEPTH = 4
NORM_EPS = 1e-6
RW_H, RW_N = 16, 64
RW_GN_EPS = 64e-5
GD_H, GD_DK, GD_DV, GD_CONV = 8, 128, 128, 4
GD_C = 3 * GD_H * GD_DK
RT_H, RT_DK, RT_DV = 4, 256, 512
HG_H, HG_E, HG_DV = 8, 128, 128
N_EXPERTS, TOP_K, D_FF = 32, 4, 1024
SWIGLU_LIMIT, SWIGLU_ALPHA = 7.0, 1.702

LANES = 128
EXPERT_TILE = 512
ROUTER_TILE = 512
TOKEN_BLOCK = 256
VMEM_LIMIT = 56 * 1024 * 1024


def _cparams(sem, vmem=VMEM_LIMIT):
    return pltpu.CompilerParams(dimension_semantics=sem, vmem_limit_bytes=vmem)


def _sigmoid(x):
    return 1.0 / (1.0 + jnp.exp(-x))


def _silu(x):
    return x * _sigmoid(x)


def _softplus(x):
    return jnp.maximum(x, 0.0) + jnp.log(1.0 + jnp.exp(-jnp.abs(x)))


def _modulate(x, g, shift, scale):
    ms = jnp.mean(x * x, axis=-1, keepdims=True)
    return (x * lax.rsqrt(ms + NORM_EPS) * g) * (1.0 + scale) + shift


def _bdot(a, b):
    return jnp.dot(a.astype(BF16), b.astype(BF16), preferred_element_type=F32)


def _bdot_nt(a, b):
    return lax.dot_general(a.astype(BF16), b.astype(BF16), (((1,), (1,)), ((), ())),
                           preferred_element_type=F32)


def _bdot_tn(a, b):
    return lax.dot_general(a.astype(BF16), b.astype(BF16), (((0,), (0,)), ((), ())),
                           preferred_element_type=F32)


def _fdot(a, b):
    return jnp.dot(a, b, precision=HIGHEST, preferred_element_type=F32)


class _Group:
    def __init__(self, b, t, tm):
        self.b, self.t, self.n = b, t, b * t
        self.tm = min(tm, self.n)
        assert self.n % self.tm == 0
        assert (self.t % self.tm == 0) or (self.tm % self.t == 0)
        self.per_batch = self.t % self.tm == 0
        self.tiles = self.n // self.tm

    def rows(self, width):
        return pl.BlockSpec((self.tm, width), lambda i: (i, 0))

    def rowmod(self, arr):
        w = arr.shape[-1]
        if self.per_batch:
            k = self.t // self.tm
            return arr.reshape(self.b, 1, w), pl.BlockSpec((1, 1, w), lambda i: (i // k, 0, 0))
        rep = jnp.repeat(arr, self.t, axis=0).reshape(self.tiles, self.tm, w)
        return rep, pl.BlockSpec((1, self.tm, w), lambda i: (i, 0, 0))

    def rowseq(self, arr):
        w = arr.shape[-1]
        if self.per_batch:
            assert arr.shape[1] == 1
            k = self.t // self.tm
            return arr, pl.BlockSpec((1, 1, w), lambda i: (i // k, 0, 0))
        return arr.reshape(self.tiles, self.tm, w), pl.BlockSpec((1, self.tm, w), lambda i: (i, 0, 0))

    def postab(self, tab):
        w = tab.shape[-1]
        if self.per_batch:
            k = self.t // self.tm
            return tab.reshape(k, self.tm, w), pl.BlockSpec((1, self.tm, w), lambda i: (i % k, 0, 0))
        rep = jnp.tile(tab, (self.tm // self.t, 1)).reshape(1, self.tm, w)
        return rep, pl.BlockSpec((1, self.tm, w), lambda i: (0, 0, 0))

    def prev8(self, width):
        k = self.tm // 8
        return pl.BlockSpec((8, width), lambda i: (jnp.maximum(i * k - 1, 0), 0))


def _full(shape):
    nd = len(shape)
    return pl.BlockSpec(shape, lambda *a: (0,) * nd)


def _tpos(tm, t):
    row = pl.program_id(0) * tm + lax.broadcasted_iota(I32, (tm, 1), 0)
    return row % t


def _stage_rows(scr, cur, prev8):
    scr[0:8, :] = prev8
    scr[8:, :] = cur


def _shifted_rows(scr, d):
    return scr[pl.ds(8 - d, scr.shape[0] - 8), :]


def _ada_kernel(c_ref, w_ref, b_ref, o_ref):
    o_ref[0] = _bdot(_silu(c_ref[...]), w_ref[0]) + b_ref[0]


def _ada_call(c_all, ada_w, ada_b):
    nb = c_all.shape[0]
    tn = 1536
    return pl.pallas_call(
        _ada_kernel,
        out_shape=jax.ShapeDtypeStruct((DEPTH, nb, 6 * D), F32),
        grid=(DEPTH, 6 * D // tn),
        in_specs=[pl.BlockSpec((nb, D), lambda l, j: (0, 0)),
                  pl.BlockSpec((1, D, tn), lambda l, j: (l, 0, j)),
                  pl.BlockSpec((1, 1, tn), lambda l, j: (l, 0, j))],
        out_specs=pl.BlockSpec((1, nb, tn), lambda l, j: (l, 0, j)),
        compiler_params=_cparams(("arbitrary", "arbitrary")),
        name="adaln",
    )(c_all, ada_w, ada_b.reshape(DEPTH, 1, 6 * D))


def _modrows_kernel(x_ref, g_ref, sh_ref, sc_ref, *rest):
    h = _modulate(x_ref[...], g_ref[...], sh_ref[...], sc_ref[...])
    if len(rest) == 2:
        w_ref, o_ref = rest
        o_ref[...] = _bdot(h, w_ref[...])
    else:
        rest[0][...] = h


def _modrows_call(x, g, shift, scale, w=None):
    n = x.shape[0]
    args = [x, g.reshape(1, D), shift, scale]
    specs = [_full((n, D)), _full((1, D)), _full((n, D)), _full((n, D))]
    width = D
    if w is not None:
        args.append(w)
        specs.append(_full(w.shape))
        width = w.shape[1]
    return pl.pallas_call(
        _modrows_kernel,
        out_shape=jax.ShapeDtypeStruct((n, width), F32),
        grid=(1,),
        in_specs=specs,
        out_specs=_full((n, width)),
        compiler_params=_cparams(("arbitrary",)),
        name="modrows",
    )(*args)


def _outproj_kernel(has_mul, x_ref, y_ref, *rest):
    if has_mul:
        m_ref, w_ref, gt_ref, o_ref = rest
        y = y_ref[...].astype(F32) * m_ref[...].astype(F32)
    else:
        w_ref, gt_ref, o_ref = rest
        y = y_ref[...]
    o_ref[...] = x_ref[...] + gt_ref[0] * _bdot(y, w_ref[...])


def _outproj_call(grp, x, y, w_o, gate, mul=None):
    dy = y.shape[1]
    gt, gt_spec = grp.rowmod(gate)
    args = [x, y]
    specs = [grp.rows(D), grp.rows(dy)]
    if mul is not None:
        args.append(mul)
        specs.append(grp.rows(dy))
    args += [w_o, gt]
    specs += [_full(w_o.shape), gt_spec]
    return pl.pallas_call(
        functools.partial(_outproj_kernel, mul is not None),
        out_shape=jax.ShapeDtypeStruct((grp.n, D), F32),
        grid=(grp.tiles,),
        in_specs=specs,
        out_specs=grp.rows(D),
        compiler_params=_cparams(("arbitrary",)),
        name="outproj",
    )(*args)


def _route_rows(x, cin_ref, g_ref, sh_ref, sc_ref, wr_ref, br_ref,
                h_ref, idx_ref, gate_ref, rank_ref, cnt_ref, carry):
    i = pl.program_id(0)

    @pl.when(i == 0)
    def _():
        carry[...] = cin_ref[...]

    tm = x.shape[0]
    h = _modulate(x, g_ref[...], sh_ref[0], sc_ref[0])
    h_ref[...] = h
    logits = _fdot(h, wr_ref[...]) + br_ref[...]
    lane = lax.broadcasted_iota(I32, logits.shape, 1)
    work = logits
    sel = jnp.zeros(logits.shape, jnp.bool_)
    picks, vals = [], []
    for _ in range(TOP_K):
        m = jnp.max(work, axis=-1, keepdims=True)
        idx = jnp.min(jnp.where(work == m, lane, N_EXPERTS), axis=-1, keepdims=True)
        pick = lane == idx
        picks.append((idx, pick))
        vals.append(m)
        sel = jnp.logical_or(sel, pick)
        work = jnp.where(pick, -jnp.inf, work)
    es = [jnp.exp(v - vals[0]) for v in vals]
    denom = es[0] + es[1] + es[2] + es[3]
    self_f = sel.astype(F32)
    tri = (lax.broadcasted_iota(I32, (tm, tm), 0) > lax.broadcasted_iota(I32, (tm, tm), 1))
    local = jnp.dot(tri.astype(BF16), self_f.astype(BF16), preferred_element_type=F32)
    rank = local + carry[...]
    carry[...] = carry[...] + jnp.sum(self_f, axis=0, keepdims=True)
    cnt_ref[...] = carry[...]
    lane_o = lax.broadcasted_iota(I32, (tm, LANES), 1)
    idx_o = jnp.zeros((tm, LANES), I32)
    gate_o = jnp.zeros((tm, LANES), F32)
    rank_o = jnp.zeros((tm, LANES), I32)
    for k in range(TOP_K):
        idx, pick = picks[k]
        rk = jnp.sum(jnp.where(pick, rank, 0.0), axis=-1, keepdims=True)
        idx_o = jnp.where(lane_o == k, idx, idx_o)
        gate_o = jnp.where(lane_o == k, es[k] / denom, gate_o)
        rank_o = jnp.where(lane_o == k, rk.astype(I32), rank_o)
    idx_ref[...] = idx_o
    gate_ref[...] = gate_o
    rank_ref[...] = rank_o


def _router_kernel(cin_ref, x_ref, g_ref, sh_ref, sc_ref, wr_ref, br_ref,
                   h_ref, idx_ref, gate_ref, rank_ref, cnt_ref, carry):
    _route_rows(x_ref[...], cin_ref, g_ref, sh_ref, sc_ref, wr_ref, br_ref,
                h_ref, idx_ref, gate_ref, rank_ref, cnt_ref, carry)


def _router_call(grp, route, x):
    counts_in, g2, shift2, scale2, w_router, b_router = route
    sh, sh_spec = grp.rowmod(shift2)
    sc, sc_spec = grp.rowmod(scale2)
    pad = pl.BlockSpec((grp.tm, LANES), lambda i: (i, 0))
    outs = pl.pallas_call(
        _router_kernel,
        out_shape=(jax.ShapeDtypeStruct((grp.n, D), F32),
                   jax.ShapeDtypeStruct((grp.n, LANES), I32),
                   jax.ShapeDtypeStruct((grp.n, LANES), F32),
                   jax.ShapeDtypeStruct((grp.n, LANES), I32),
                   jax.ShapeDtypeStruct((1, N_EXPERTS), F32)),
        grid=(grp.tiles,),
        in_specs=[_full((1, N_EXPERTS)), grp.rows(D), _full((1, D)), sh_spec, sc_spec,
                  _full((D, N_EXPERTS)), _full((1, N_EXPERTS))],
        out_specs=(grp.rows(D), pad, pad, pad, _full((1, N_EXPERTS))),
        scratch_shapes=[pltpu.VMEM((1, N_EXPERTS), F32)],
        compiler_params=_cparams(("arbitrary",)),
        name="moe_router",
    )(counts_in, x, g2.reshape(1, D), sh, sc, w_router, b_router.reshape(1, N_EXPERTS))
    return (x,) + tuple(outs)


def _dispatch_kernel(cnt_ref, off_ref, nv_ref, pos_ref, h_ref, xs_ref, zbuf, sem, sem_z):
    i = pl.program_id(0)

    for t in range(TOKEN_BLOCK):
        for k in range(TOP_K):
            pltpu.make_async_copy(h_ref.at[pl.ds(t, 1)], xs_ref.at[pl.ds(pos_ref[t * TOP_K + k], 1)],
                                  sem).start(priority=k % 2)
    for k in range(TOP_K):
        pltpu.make_async_copy(h_ref, xs_ref.at[pl.ds(0, TOKEN_BLOCK)], sem).wait()

    @pl.when(i == pl.num_programs(0) - 1)
    def _():
        zbuf[...] = jnp.zeros(zbuf.shape, F32)
        bits = [1 << s for s in range(EXPERT_TILE.bit_length() - 2, 2, -1)]

        def pad_copies(e, wait):
            n = cnt_ref[e]
            start = off_ref[e] + n
            end = off_ref[e] + ((n + EXPERT_TILE - 1) // EXPERT_TILE) * EXPERT_TILE
            head = (-start) & 7

            def one(r, c):
                cp = pltpu.make_async_copy(zbuf.at[pl.ds(0, 1)], xs_ref.at[pl.ds(start + r, 1)], sem_z)
                if wait:
                    cp.wait()
                else:
                    cp.start()
                return c

            lax.fori_loop(0, head, one, 0)
            start8 = start + head
            rem = end - start8
            for bit in bits:
                @pl.when((rem & bit) != 0)
                def _():
                    s = pl.multiple_of(start8 + (rem & ~(2 * bit - 1)), 8)
                    cp = pltpu.make_async_copy(zbuf.at[pl.ds(0, bit)], xs_ref.at[pl.ds(s, bit)], sem_z)
                    if wait:
                        cp.wait()
                    else:
                        cp.start()

        def tail_copy(j):
            return pltpu.make_async_copy(zbuf, xs_ref.at[pl.ds(j * EXPERT_TILE, EXPERT_TILE)], sem_z)

        n_tiles = xs_ref.shape[0] // EXPERT_TILE
        for wait in (False, True):
            def per_expert(e, c):
                pad_copies(e, wait)
                return c

            def per_tail(j, c):
                if wait:
                    tail_copy(j).wait()
                else:
                    tail_copy(j).start()
                return c

            lax.fori_loop(0, N_EXPERTS, per_expert, 0)
            lax.fori_loop(nv_ref[0], n_tiles, per_tail, 0)


def _dispatch_call(counts, offsets, n_valid, pos_flat, h, n_rows):
    n = h.shape[0]
    return pl.pallas_call(
        _dispatch_kernel,
        out_shape=jax.ShapeDtypeStruct((n_rows, D), F32),
        grid_spec=pltpu.PrefetchScalarGridSpec(
            num_scalar_prefetch=3,
            grid=(n // TOKEN_BLOCK,),
            in_specs=[pl.BlockSpec((TOKEN_BLOCK * TOP_K,), lambda i, c, o, v: (i,), memory_space=pltpu.SMEM),
                      pl.BlockSpec((TOKEN_BLOCK, D), lambda i, c, o, v: (i, 0))],
            out_specs=pl.BlockSpec(memory_space=pl.ANY),
            scratch_shapes=[pltpu.VMEM((EXPERT_TILE, D), F32), pltpu.SemaphoreType.DMA(()),
                            pltpu.SemaphoreType.DMA(())]),
        compiler_params=_cparams(("arbitrary",)),
        name="moe_dispatch",
    )(counts, offsets, n_valid, pos_flat, h)


def _expert_kernel(te_ref, nv_ref, x_ref, wgu_ref, bgu_ref, wd_ref, bd_ref, o_ref, wgu_s, wd_s):
    j = pl.program_id(0)
    fresh = jnp.logical_or(j == 0, te_ref[j] != te_ref[jnp.maximum(j - 1, 0)])

    @pl.when(jnp.logical_and(j < nv_ref[0], fresh))
    def _():
        wgu_s[...] = wgu_ref[0, 0].astype(BF16)
        wd_s[...] = wd_ref[0, 0].astype(BF16)

    @pl.when(j < nv_ref[0])
    def _():
        x = x_ref[...].astype(BF16)
        gu = jnp.dot(x, wgu_s[...], preferred_element_type=F32) + bgu_ref[0, 0]
        gl = jnp.minimum(gu[:, :D_FF], SWIGLU_LIMIT)
        up = jnp.clip(gu[:, D_FF:], -SWIGLU_LIMIT, SWIGLU_LIMIT)
        act = (up + 1.0) * gl * _sigmoid(SWIGLU_ALPHA * gl)
        o_ref[...] = jnp.dot(act.astype(BF16), wd_s[...], preferred_element_type=F32) + bd_ref[0, 0]

    @pl.when(j >= nv_ref[0])
    def _():
        o_ref[...] = jnp.zeros(o_ref.shape, F32)


def _expert_call(layer, tile_expert, n_valid, xs, w_gu, b_gu, w_down, b_down):
    n_rows = xs.shape[0]
    g = n_rows // EXPERT_TILE
    return pl.pallas_call(
        _expert_kernel,
        out_shape=jax.ShapeDtypeStruct((n_rows, D), F32),
        grid_spec=pltpu.PrefetchScalarGridSpec(
            num_scalar_prefetch=2,
            grid=(g,),
            in_specs=[pl.BlockSpec((EXPERT_TILE, D), lambda j, te, nv: (jnp.minimum(j, nv[0] - 1), 0)),
                      pl.BlockSpec((1, 1, D, 2 * D_FF), lambda j, te, nv: (layer, te[j], 0, 0)),
                      pl.BlockSpec((1, 1, 1, 2 * D_FF), lambda j, te, nv: (layer, te[j], 0, 0)),
                      pl.BlockSpec((1, 1, D_FF, D), lambda j, te, nv: (layer, te[j], 0, 0)),
                      pl.BlockSpec((1, 1, 1, D), lambda j, te, nv: (layer, te[j], 0, 0))],
            out_specs=pl.BlockSpec((EXPERT_TILE, D), lambda j, te, nv: (j, 0)),
            scratch_shapes=[pltpu.VMEM((D, 2 * D_FF), BF16), pltpu.VMEM((D_FF, D), BF16)]),
        compiler_params=_cparams(("arbitrary",)),
        name="moe_experts",
    )(tile_expert, n_valid, xs, w_gu, b_gu.reshape(-1, N_EXPERTS, 1, 2 * D_FF), w_down,
      b_down.reshape(-1, N_EXPERTS, 1, D))


def _combine_kernel(with_norm, pos_ref, x_ref, gate_ref, gt_ref, ys_ref, *rest):
    if with_norm:
        ng_ref, o_ref, y_ref, buf, sems = rest
    else:
        o_ref, buf, sems = rest
    half = TOKEN_BLOCK // 2
    for hf in range(2):
        for t in range(hf * half, (hf + 1) * half):
            for k in range(TOP_K):
                pltpu.make_async_copy(ys_ref.at[pl.ds(pos_ref[t * TOP_K + k], 1)], buf.at[k, pl.ds(t, 1)],
                                      sems.at[hf]).start(priority=k % 2)
    for hf in range(2):
        rows = pl.ds(hf * half, half)
        for k in range(TOP_K):
            pltpu.make_async_copy(ys_ref.at[pl.ds(0, half)], buf.at[k, rows], sems.at[hf]).wait()
        gate = gate_ref[rows, :]
        f = gate[:, 0:1] * buf[0, rows]
        for k in range(1, TOP_K):
            f = f + gate[:, k:k + 1] * buf[k, rows]
        gt = gt_ref[0]
        x_new = x_ref[rows, :] + (gt if gt.shape[0] == 1 else gt[hf * half:(hf + 1) * half]) * f
        o_ref[rows, :] = x_new
        if with_norm:
            ms = jnp.mean(x_new * x_new, axis=-1, keepdims=True)
            y_ref[rows, :] = x_new * lax.rsqrt(ms + NORM_EPS) * ng_ref[...]


def _combine_call(grp, pos_flat, x, gate_pad, gate2, ys, final_g=None):
    gt, gt_spec = grp.rowmod(gate2)
    assert grp.tm == TOKEN_BLOCK
    with_norm = final_g is not None
    args = [pos_flat, x, gate_pad, gt, ys]
    specs = [pl.BlockSpec((TOKEN_BLOCK * TOP_K,), lambda i: (i,), memory_space=pltpu.SMEM),
             grp.rows(D), grp.rows(LANES), gt_spec, pl.BlockSpec(memory_space=pl.ANY)]
    shape = jax.ShapeDtypeStruct((grp.n, D), F32)
    if with_norm:
        args.append(final_g.reshape(1, D))
        specs.append(_full((1, D)))
    return pl.pallas_call(
        functools.partial(_combine_kernel, with_norm),
        out_shape=(shape, shape) if with_norm else shape,
        grid=(grp.tiles,),
        in_specs=specs,
        out_specs=(grp.rows(D), grp.rows(D)) if with_norm else grp.rows(D),
        scratch_shapes=[pltpu.VMEM((TOP_K, TOKEN_BLOCK, D), F32), pltpu.SemaphoreType.DMA((2,))],
        compiler_params=_cparams(("arbitrary",)),
        name="moe_combine",
    )(*args)


def _finish(grp, route, x, y, w_o, gate, mul=None):
    x_new = _outproj_call(grp, x, y, w_o, gate, mul)
    if route is None:
        return x_new
    return _router_call(_Group(grp.b, grp.t, ROUTER_TILE), route, x_new)


def _moe(layer, groups, routed, gates2, w_gu, b_gu, w_down, b_down, final_g=None):
    counts = routed[-1][5]
    xs_in = [r[0] for r in routed]
    hs = [r[1] for r in routed]
    idxs = [r[2][:, :TOP_K] for r in routed]
    gates = [r[3] for r in routed]
    ranks = [r[4][:, :TOP_K] for r in routed]
    mods = [(None, None, g2) for g2 in gates2]
    h_all = jnp.concatenate(hs, axis=0)
    idx_all = jnp.concatenate(idxs, axis=0)
    rank_all = jnp.concatenate(ranks, axis=0)
    n = h_all.shape[0]
    cnt = counts[0].astype(I32)
    padded = ((cnt + EXPERT_TILE - 1) // EXPERT_TILE) * EXPERT_TILE
    ends = jnp.cumsum(padded)
    offsets = ends - padded
    n_tiles = (n * TOP_K + N_EXPERTS * (EXPERT_TILE - 1)) // EXPERT_TILE
    n_rows = n_tiles * EXPERT_TILE
    pos = (jnp.take(offsets, idx_all) + rank_all).astype(I32)
    pos_flat = pos.reshape(n * TOP_K)
    n_valid = (ends[-1] // EXPERT_TILE).astype(I32)
    tile_start = jnp.arange(n_tiles, dtype=I32) * EXPERT_TILE
    tile_start = jnp.minimum(tile_start, ends[-1] - EXPERT_TILE)
    tile_expert = jnp.sum(tile_start[:, None] >= ends[None, :], axis=1).astype(I32)
    n_valid = n_valid.reshape(1)
    xs = _dispatch_call(cnt, offsets.astype(I32), n_valid, pos_flat, h_all, n_rows)
    ys = _expert_call(layer, tile_expert, n_valid, xs, w_gu, b_gu, w_down, b_down)
    outs = []
    start = 0
    for grp, x, gate, (_, _, gt2) in zip(groups, xs_in, gates, mods):
        cgrp = _Group(grp.b, grp.t, TOKEN_BLOCK)
        p = lax.dynamic_slice_in_dim(pos_flat, start * TOP_K, grp.n * TOP_K)
        outs.append(_combine_call(cgrp, p, x, gate, gt2, ys, final_g))
        start += grp.n
    return outs


def _rwkv_proj_kernel(t_len, x_ref, xp_ref, g_ref, sh_ref, sc_ref, s0_ref, mu_ref, wrkv_ref, w0_ref,
                      w1_ref, w2_ref, a0_ref, a1_ref, a2_ref, g1_ref, g2_ref,
                      r_ref, w_ref, k_ref, v_ref, a_ref, gg_ref, scr):
    tm = x_ref.shape[0]
    g, sh, sc = g_ref[...], sh_ref[0], sc_ref[0]
    h = _modulate(x_ref[...], g, sh, sc)
    hp = _modulate(xp_ref[...], g, sh[0:8] if sh.shape[0] > 1 else sh, sc[0:8] if sc.shape[0] > 1 else sc)
    _stage_rows(scr, h, hp)
    prev = jnp.where(_tpos(tm, t_len) == 0, s0_ref[0], _shifted_rows(scr, 1))
    dx = prev - h
    mu = mu_ref[...]
    xr, xw, xk, xv, xa, xg = [h + dx * mu[n:n + 1] for n in range(6)]
    r_ref[...] = _bdot(xr, wrkv_ref[0]).astype(BF16)
    k_ref[...] = _bdot(xk, wrkv_ref[1]).astype(BF16)
    v_ref[...] = _bdot(xv, wrkv_ref[2]).astype(BF16)
    w_log = -_softplus(-(w0_ref[...] + _bdot(jnp.tanh(_bdot(xw, w1_ref[...])), w2_ref[...]))) - 0.5
    w_ref[...] = jnp.exp(-jnp.exp(w_log))
    a_ref[...] = _sigmoid(a0_ref[...] + _bdot(_bdot(xa, a1_ref[...]), a2_ref[...])).astype(BF16)
    gg_ref[...] = _bdot(_sigmoid(_bdot(xg, g1_ref[...])), g2_ref[...]).astype(BF16)


def _rwkv_core_kernel(r_ref, w_ref, k_ref, v_ref, a_ref, kk_p, ka_p, rk_p, lnw_p, lnb_p, s0_ref,
                      y_ref, st_ref, state, kk_s, b_s, km_s, r_s):
    j = pl.program_id(1)
    tc = r_ref.shape[0]
    n = RW_N

    @pl.when(j == 0)
    def _():
        state[...] = s0_ref[...]

    def step(t, c):
        kt, at, vt, rt = [z[t].astype(F32) for z in (k_ref, a_ref, v_ref, r_ref)]
        r_s[...] = rt
        kk = kt * kk_p[...]
        kk = kk * lax.rsqrt(jnp.sum(kk * kk, axis=0, keepdims=True) + 1e-6)
        km = kt * (1.0 + (at - 1.0) * ka_p[...])
        kk_s[...] = kk
        b_s[...] = kk * at
        km_s[...] = km
        sa2 = [jnp.zeros((n, LANES), F32), jnp.zeros((n, LANES), F32)]
        for kx in range(n):
            sa2[kx % 2] = sa2[kx % 2] + state[kx] * kk_s[pl.ds(kx, 1), :]
        sa = sa2[0] + sa2[1]
        y2 = [jnp.zeros((n, LANES), F32), jnp.zeros((n, LANES), F32)]
        for kx in range(n):
            s_new = (state[kx] * w_ref[t, pl.ds(kx, 1), :] - sa * b_s[pl.ds(kx, 1), :]
                     + vt * km_s[pl.ds(kx, 1), :])
            state[kx] = s_new
            y2[kx % 2] = y2[kx % 2] + s_new * r_s[pl.ds(kx, 1), :]
        y = y2[0] + y2[1]
        mean = jnp.mean(y, axis=0, keepdims=True)
        yc = y - mean
        var = jnp.mean(yc * yc, axis=0, keepdims=True)
        bonus = jnp.sum(rt * km * rk_p[...], axis=0, keepdims=True) * vt
        y_ref[t] = (yc * lax.rsqrt(var + RW_GN_EPS) * lnw_p[...] + lnb_p[...] + bonus).astype(BF16)
        return c

    lax.fori_loop(0, tc, step, 0)

    @pl.when(j == pl.num_programs(1) - 1)
    def _():
        st_ref[...] = state[...]


def _rwkv_layer(grp, x, norm_g, mods, shift0, wkv0, mu, w_rkv, w0, w1, w2, a0, a1, a2, g1, g2,
                k_k, k_a, r_k, ln_w, ln_b, w_o, route=None):
    b, t = grp.b, grp.t
    shift, scale, gate = mods
    sh, sh_spec = grp.rowmod(shift)
    sc, sc_spec = grp.rowmod(scale)
    if shift0 is None:
        s0 = jnp.zeros((b, 1, D), F32)
    else:
        s0 = jnp.concatenate([shift0[:, None, :], jnp.zeros((b, t - 1, D), F32)], axis=1)
    s0, s0_spec = grp.rowseq(s0)
    bf = lambda z: z.astype(BF16)
    row = lambda z: z.reshape(1, -1)
    weights = [mu, bf(w_rkv), row(w0), bf(w1), bf(w2), row(a0), bf(a1), bf(a2), bf(g1), bf(g2)]
    outs = pl.pallas_call(
        functools.partial(_rwkv_proj_kernel, t),
        out_shape=tuple(jax.ShapeDtypeStruct((grp.n, D), dt) for dt in (BF16, F32, BF16, BF16, BF16, BF16)),
        grid=(grp.tiles,),
        in_specs=[grp.rows(D), grp.prev8(D), _full((1, D)), sh_spec, sc_spec, s0_spec]
        + [_full(z.shape) for z in weights],
        out_specs=tuple(grp.rows(D) for _ in range(6)),
        scratch_shapes=[pltpu.VMEM((grp.tm + 8, D), F32)],
        compiler_params=_cparams(("arbitrary",)),
        name="rwkv_proj",
    )(x, x, row(norm_g), sh, sc, s0, *weights)
    r, w, k, v, a, gg = outs
    bh = b * RW_H

    def to_core(z):
        return z.reshape(b, t, RW_H, RW_N).transpose(1, 3, 0, 2).reshape(t, RW_N, bh)

    def ptile(p):
        return jnp.tile(p.reshape(RW_H, RW_N).T, (1, b))

    if wkv0 is None:
        st0 = jnp.zeros((RW_N, RW_N, bh), F32)
    else:
        st0 = wkv0.transpose(3, 2, 0, 1).reshape(RW_N, RW_N, bh)
    tc = min(t, 32)
    seq = pl.BlockSpec((tc, RW_N, LANES), lambda q, j: (j, 0, q))
    par = pl.BlockSpec((RW_N, LANES), lambda q, j: (0, q))
    stt = pl.BlockSpec((RW_N, RW_N, LANES), lambda q, j: (0, 0, q))
    y, st = pl.pallas_call(
        _rwkv_core_kernel,
        out_shape=(jax.ShapeDtypeStruct((t, RW_N, bh), BF16),
                   jax.ShapeDtypeStruct((RW_N, RW_N, bh), F32)),
        grid=(bh // LANES, t // tc),
        in_specs=[seq] * 5 + [par] * 5 + [stt],
        out_specs=(seq, stt),
        scratch_shapes=[pltpu.VMEM((RW_N, RW_N, LANES), F32)] + [pltpu.VMEM((RW_N, LANES), F32)] * 4,
        compiler_params=_cparams(("arbitrary", "arbitrary")),
        name="rwkv_core",
    )(to_core(r), to_core(w), to_core(k), to_core(v), to_core(a),
      ptile(k_k), ptile(k_a), ptile(r_k.reshape(-1)), ptile(ln_w), ptile(ln_b), st0)
    y_rows = y.reshape(t, RW_N, b, RW_H).transpose(2, 0, 3, 1).reshape(grp.n, D)
    x_new = _finish(grp, route, x, y_rows, bf(w_o), gate, mul=gg)
    new_wkv = st.reshape(RW_N, RW_N, b, RW_H).transpose(2, 3, 1, 0)
    x_last = x.reshape(b, t, D)[:, -1]
    new_shift = _modrows_call(x_last, norm_g, shift, scale)
    return x_new, new_shift, new_wkv


def _pad_time(z, b, t, tp):
    if tp == t:
        return z
    w = z.shape[-1]
    return jnp.pad(z.reshape(b, t, w), ((0, 0), (0, tp - t), (0, 0))).reshape(b * tp, w)


def _unpad_time(z, b, t, tp):
    if tp == t:
        return z
    w = z.shape[-1]
    return z.reshape(b, tp, w)[:, :t].reshape(b * t, w)


def _gdn_proj_kernel(t_len, x_ref, xp_ref, g_ref, sh_ref, sc_ref, c1_ref, c2_ref, c3_ref, wqkv_ref,
                     wz_ref, wb_ref, wa_ref, cw_ref, alog_ref, dtb_ref,
                     qkv_ref, z_ref, beta_ref, gdec_ref, scr):
    tm = x_ref.shape[0]
    g, sh, sc = g_ref[...], sh_ref[0], sc_ref[0]
    h = _modulate(x_ref[...], g, sh, sc)
    hp = _modulate(xp_ref[...], g, sh[0:8] if sh.shape[0] > 1 else sh, sc[0:8] if sc.shape[0] > 1 else sc)
    hb = h.astype(BF16)
    pre = jnp.dot(hb, wqkv_ref[...], preferred_element_type=F32)
    pre8 = _bdot(hp, wqkv_ref[...])
    tpos = _tpos(tm, t_len)
    cw = cw_ref[...]
    conv = pre * cw[3:4]
    _stage_rows(scr, pre, pre8)
    for d, cref in ((1, c1_ref), (2, c2_ref), (3, c3_ref)):
        past = jnp.where(tpos >= d, _shifted_rows(scr, d), cref[0])
        conv = conv + past * cw[3 - d:4 - d]
    act = _silu(conv)
    nh = GD_H
    for hh in range(2 * nh):
        sl = slice(hh * GD_DK, (hh + 1) * GD_DK)
        seg = act[:, sl]
        seg = seg * lax.rsqrt(jnp.sum(seg * seg, axis=-1, keepdims=True) + 1e-6)
        if hh < nh:
            seg = seg * (GD_DK ** -0.5)
        qkv_ref[:, sl] = seg.astype(BF16)
    qkv_ref[:, 2 * nh * GD_DK:] = act[:, 2 * nh * GD_DK:].astype(BF16)
    z_ref[...] = jnp.dot(hb, wz_ref[...], preferred_element_type=F32).astype(BF16)
    beta_ref[...] = _sigmoid(jnp.dot(hb, wb_ref[...], preferred_element_type=F32))
    a_logit = jnp.dot(hb, wa_ref[...], preferred_element_type=F32)
    gdec_ref[...] = -jnp.exp(alog_ref[...]) * _softplus(a_logit + dtb_ref[...])


def _unit_lower_inverse(a, eye, masks):
    blk8, offs = masks
    n = range(len(a))
    a8 = [jnp.where(blk8, a[i], 0.0) for i in n]
    x = [eye - a8[i] for i in n]
    y = [_bdot(a8[i], a8[i]) for i in n]
    x = [x[i] + _bdot(x[i], y[i]) for i in n]
    y = [_bdot(y[i], y[i]) for i in n]
    x = [x[i] + _bdot(x[i], y[i]) for i in n]
    for off in offs:
        t = [_bdot(jnp.where(off, a[i], 0.0), x[i]) for i in n]
        x = [x[i] - _bdot(x[i], t[i]) for i in n]
    return x


def _inverse_masks(c):
    ri = lax.broadcasted_iota(I32, (c, c), 0)
    ci = lax.broadcasted_iota(I32, (c, c), 1)
    sr = lambda z, s: lax.shift_right_logical(z, jnp.full(z.shape, s, I32))
    blk8 = sr(ri, 3) == sr(ci, 3)
    offs = []
    m, lg = 8, 3
    while m < c:
        same = sr(ri, lg + 1) == sr(ci, lg + 1)
        lower = jnp.logical_and((sr(ri, lg) & 1) == 1, (sr(ci, lg) & 1) == 0)
        offs.append(jnp.logical_and(same, lower))
        m, lg = m * 2, lg + 1
    return ri, ci, (blk8, offs)


def _gdn_core_kernel(q_ref, k_ref, v_ref, z_ref, beta_ref, g_ref, s0_ref, nw_ref, y_ref, st_ref, state):
    cidx = pl.program_id(1)
    nb, c = q_ref.shape[0], q_ref.shape[1]

    @pl.when(cidx == 0)
    def _():
        state[...] = s0_ref[...]

    ri, ci, masks = _inverse_masks(c)
    incl = ri >= ci
    strict = ri > ci
    eye = (ri == ci).astype(F32)
    incl_f, incl_t = incl.astype(F32), (ci >= ri).astype(F32)
    nw = nw_ref[...]
    sls = [slice(h * GD_DK, (h + 1) * GD_DK) for h in range(GD_H)]
    pairs = [(s, h) for s in range(nb) for h in range(GD_H)]
    n = range(len(pairs))
    g = [g_ref[s] for s in range(nb)]
    cum = [_fdot(incl_f, g[s]) for s in range(nb)]
    cum_t = [lax.dot_general(g[s], incl_t, (((0,), (0,)), ((), ())), precision=HIGHEST,
                             preferred_element_type=F32) for s in range(nb)]
    beta = [beta_ref[s] for s in range(nb)]
    q = [q_ref[s, :, sls[h]] for s, h in pairs]
    k = [k_ref[s, :, sls[h]] for s, h in pairs]
    kf = [z.astype(F32) for z in k]
    v = [v_ref[s, :, sls[h]].astype(F32) for s, h in pairs]
    st = [state[s, h] for s, h in pairs]
    cum_c = [cum[s][:, h:h + 1] for s, h in pairs]
    dec = [jnp.where(incl, jnp.exp(jnp.where(incl, cum_c[i] - cum_t[s][h:h + 1, :], 0.0)), 0.0)
           for i, (s, h) in enumerate(pairs)]
    bcol = [beta[s][:, h:h + 1] for s, h in pairs]
    kb = [kf[i] * bcol[i] for i in n]
    a = [jnp.where(strict, _bdot_nt(kb[i], k[i]) * dec[i], 0.0) for i in n]
    attn = [_bdot_nt(q[i], k[i]) * dec[i] for i in n]
    x = _unit_lower_inverse(a, eye, masks)
    ecum = [jnp.exp(cum_c[i]) for i in n]
    sol = [_bdot(x[i], jnp.concatenate([v[i] * bcol[i], kb[i] * ecum[i]], axis=1)) for i in n]
    u = [sol[i][:, :GD_DV] - _bdot(sol[i][:, GD_DV:], st[i]) for i in n]
    o = [_bdot(q[i].astype(F32) * ecum[i], st[i]) + _bdot(attn[i], u[i]) for i in n]
    last = [cum[s][c - 1:c, h:h + 1] for s, h in pairs]
    s_new = [st[i] * jnp.exp(last[i]) + _bdot_tn(kf[i] * jnp.exp(last[i] - cum_c[i]), u[i]) for i in n]
    for i, (s, h) in enumerate(pairs):
        state[s, h] = s_new[i]
        on = o[i] * lax.rsqrt(jnp.mean(o[i] * o[i], axis=-1, keepdims=True) + NORM_EPS) * nw
        y_ref[s, :, sls[h]] = (on * _silu(z_ref[s, :, sls[h]].astype(F32))).astype(BF16)

    @pl.when(cidx == pl.num_programs(1) - 1)
    def _():
        st_ref[...] = state[...]


def _gdn_layer(grp, x, norm_g, mods, conv0, ssm0, w_in, conv_w, a_log, dt_bias, norm_w, w_o, chunk,
               seqs_per_step=1, route=None):
    b, t = grp.b, grp.t
    shift, scale, gate = mods
    sh, sh_spec = grp.rowmod(shift)
    sc, sc_spec = grp.rowmod(scale)
    kd = GD_H * GD_DK
    cstates, cspecs = [], []
    for d in (1, 2, 3):
        if conv0 is None:
            cs = jnp.zeros((b, 1, GD_C), F32)
        else:
            cs = jnp.concatenate([conv0[:, 3 - d:, :], jnp.zeros((b, t - d, GD_C), F32)], axis=1)
        cs, spec = grp.rowseq(cs)
        cstates.append(cs)
        cspecs.append(spec)
    bf = lambda z: z.astype(BF16)
    pad128 = lambda z: jnp.pad(z, ((0, 0), (0, LANES - z.shape[1])))
    w_qkv = bf(w_in[:, :GD_C])
    w_z = bf(w_in[:, GD_C:GD_C + kd])
    w_b = bf(pad128(w_in[:, GD_C + kd:GD_C + kd + GD_H]))
    w_a = bf(pad128(w_in[:, GD_C + kd + GD_H:]))
    weights = [w_qkv, w_z, w_b, w_a, conv_w, pad128(a_log.reshape(1, GD_H)), pad128(dt_bias.reshape(1, GD_H))]
    qkv, z, beta, gdec = pl.pallas_call(
        functools.partial(_gdn_proj_kernel, t),
        out_shape=(jax.ShapeDtypeStruct((grp.n, GD_C), BF16), jax.ShapeDtypeStruct((grp.n, kd), BF16),
                   jax.ShapeDtypeStruct((grp.n, LANES), F32), jax.ShapeDtypeStruct((grp.n, LANES), F32)),
        grid=(grp.tiles,),
        in_specs=[grp.rows(D), grp.prev8(D), _full((1, D)), sh_spec, sc_spec] + cspecs
        + [_full(z_.shape) for z_ in weights],
        out_specs=(grp.rows(GD_C), grp.rows(kd), grp.rows(LANES), grp.rows(LANES)),
        scratch_shapes=[pltpu.VMEM((grp.tm + 8, GD_C), F32)],
        compiler_params=_cparams(("arbitrary",)),
        name="gdn_proj",
    )(x, x, norm_g.reshape(1, D), sh, sc, *cstates, *weights)
    tp = ((t + chunk - 1) // chunk) * chunk
    nc = tp // chunk
    qkv_p, z_p = _pad_time(qkv, b, t, tp), _pad_time(z, b, t, tp)
    beta_p, g_p = _pad_time(beta, b, t, tp), _pad_time(gdec, b, t, tp)
    if ssm0 is None:
        ssm0 = jnp.zeros((b, GD_H, GD_DK, GD_DV), F32)
    nb = seqs_per_step
    assert b % nb == 0
    seq3 = lambda z_: z_.reshape(b, tp, z_.shape[-1])
    col = lambda j: pl.BlockSpec((nb, chunk, kd), lambda bi, c: (bi, c, j))
    lan = pl.BlockSpec((nb, chunk, LANES), lambda bi, c: (bi, c, 0))
    stt = pl.BlockSpec((nb, GD_H, GD_DK, GD_DV), lambda bi, c: (bi, 0, 0, 0))
    qkv3 = seq3(qkv_p)
    y, st = pl.pallas_call(
        _gdn_core_kernel,
        out_shape=(jax.ShapeDtypeStruct((b, tp, kd), BF16),
                   jax.ShapeDtypeStruct((b, GD_H, GD_DK, GD_DV), F32)),
        grid=(b // nb, nc),
        in_specs=[col(0), col(1), col(2), col(0), lan, lan, stt, _full((1, GD_DV))],
        out_specs=(col(0), stt),
        scratch_shapes=[pltpu.VMEM((nb, GD_H, GD_DK, GD_DV), F32)],
        compiler_params=_cparams(("arbitrary", "arbitrary")),
        name="gdn_core",
    )(qkv3, qkv3, qkv3, seq3(z_p), seq3(beta_p), seq3(g_p), ssm0, norm_w.reshape(1, GD_DV))
    y = y.reshape(b * tp, kd)
    x_new = _finish(grp, route, x, _unpad_time(y, b, t, tp), bf(w_o), gate)
    nl = min(t, GD_CONV - 1)
    x_last = x.reshape(b, t, D)[:, t - nl:].reshape(b * nl, D)
    rep = lambda m: jnp.repeat(m, nl, axis=0)
    pre_last = _modrows_call(x_last, norm_g, rep(shift), rep(scale), w_qkv).reshape(b, nl, GD_C)
    if nl < GD_CONV - 1:
        pre_last = jnp.concatenate([conv0[:, nl:], pre_last], axis=1)
    return x_new, pre_last, st


def _ret_proj_kernel(x_ref, g_ref, sh_ref, sc_ref, cos_ref, sin_ref, w_ref, q_ref, k_ref, v_ref, gate_ref):
    h = _modulate(x_ref[...], g_ref[...], sh_ref[0], sc_ref[0]).astype(BF16)
    kd = RT_H * RT_DK
    vd = RT_H * RT_DV
    cos, sin = cos_ref[0], sin_ref[0]
    even = (lax.broadcasted_iota(I32, (1, kd), 1) & 1) == 0

    def rotary(z):
        swapped = jnp.where(even, pltpu.roll(z, kd - 1, 1), pltpu.roll(z, 1, 1))
        return z * cos + swapped * sin

    q_ref[...] = rotary(jnp.dot(h, w_ref[:, 0:kd], preferred_element_type=F32)).astype(BF16)
    k = rotary(jnp.dot(h, w_ref[:, kd:2 * kd], preferred_element_type=F32))
    k_ref[...] = (k * (RT_DK ** -0.5)).astype(BF16)
    v_ref[...] = jnp.dot(h, w_ref[:, 2 * kd:2 * kd + vd], preferred_element_type=F32).astype(BF16)
    gate_ref[...] = jnp.dot(h, w_ref[:, 2 * kd + vd:], preferred_element_type=F32).astype(BF16)


def _ret_core_kernel(q_ref, k_ref, v_ref, gate_ref, dm_ref, qd_ref, kd_ref, cd_ref, s0_ref, nw_ref,
                     y_ref, st_ref, state):
    cidx = pl.program_id(1)
    nb = q_ref.shape[0]

    @pl.when(cidx == 0)
    def _():
        state[...] = s0_ref[...]

    kss = [slice(h * RT_DK, (h + 1) * RT_DK) for h in range(RT_H)]
    vss = [slice(h * RT_DV, (h + 1) * RT_DV) for h in range(RT_H)]
    pairs = [(s, h) for s in range(nb) for h in range(RT_H)]
    n = range(len(pairs))
    q = [q_ref[s, :, kss[h]] for s, h in pairs]
    k = [k_ref[s, :, kss[h]] for s, h in pairs]
    v = [v_ref[s, :, vss[h]] for s, h in pairs]
    st = [state[s, h] for s, h in pairs]
    inner = [_bdot_nt(q[i], k[i]) * dm_ref[pairs[i][1]] for i in n]
    cross = [_bdot(q[i], st[i]) * qd_ref[pairs[i][1]] for i in n]
    o = [_bdot(inner[i], v[i]) + cross[i] for i in n]
    s_new = [st[i] * cd_ref[pairs[i][1]] + _bdot_tn(k[i].astype(F32) * kd_ref[pairs[i][1]], v[i]) for i in n]
    for i, (s, h) in enumerate(pairs):
        state[s, h] = s_new[i]
        on = o[i] * lax.rsqrt(jnp.mean(o[i] * o[i], axis=-1, keepdims=True) + NORM_EPS) * nw_ref[:, vss[h]]
        y_ref[s, :, vss[h]] = (on * _silu(gate_ref[s, :, vss[h]].astype(F32))).astype(BF16)

    @pl.when(cidx == pl.num_programs(1) - 1)
    def _():
        st_ref[...] = state[...]


def _ret_layer(grp, x, norm_g, mods, s0, pos0, w_in, norm_w, w_o, chunk, seqs_per_step=1, route=None):
    b, t = grp.b, grp.t
    shift, scale, gate = mods
    sh, sh_spec = grp.rowmod(shift)
    sc, sc_spec = grp.rowmod(scale)
    kd, vd = RT_H * RT_DK, RT_H * RT_DV
    half = RT_DK // 2
    inv = 1.0 / (10000.0 ** jnp.linspace(0.0, 1.0, half, dtype=F32))
    pos = jnp.arange(t, dtype=F32) + float(pos0)
    ang = pos[:, None] * inv[None, :]
    cos = jnp.repeat(jnp.cos(ang), 2, axis=1)
    sin = jnp.stack([-jnp.sin(ang), jnp.sin(ang)], axis=-1).reshape(t, RT_DK)
    cos4, cos_spec = grp.postab(jnp.tile(cos, (1, RT_H)))
    sin4, sin_spec = grp.postab(jnp.tile(sin, (1, RT_H)))
    wb = w_in.astype(BF16)
    q, k, v, gt = pl.pallas_call(
        _ret_proj_kernel,
        out_shape=(jax.ShapeDtypeStruct((grp.n, kd), BF16), jax.ShapeDtypeStruct((grp.n, kd), BF16),
                   jax.ShapeDtypeStruct((grp.n, vd), BF16), jax.ShapeDtypeStruct((grp.n, vd), BF16)),
        grid=(grp.tiles,),
        in_specs=[grp.rows(D), _full((1, D)), sh_spec, sc_spec, cos_spec, sin_spec, _full(wb.shape)],
        out_specs=(grp.rows(kd), grp.rows(kd), grp.rows(vd), grp.rows(vd)),
        compiler_params=_cparams(("arbitrary",)),
        name="ret_proj",
    )(x, norm_g.reshape(1, D), sh, sc, cos4, sin4, wb)
    tp = ((t + chunk - 1) // chunk) * chunk
    nc = tp // chunk
    nv = min(t, chunk)
    assert tp == t or nc == 1
    log_gamma = jnp.log1p(-jnp.exp2(-5.0 - jnp.arange(RT_H, dtype=F32)))
    idx = jnp.arange(chunk, dtype=F32)
    diff = idx[:, None] - idx[None, :]
    dmask = jnp.where(diff >= 0, jnp.exp(log_gamma[:, None, None] * jnp.maximum(diff, 0.0)), 0.0)
    q_dec = jnp.exp(log_gamma[:, None] * (idx + 1.0))[:, :, None]
    k_dec = jnp.exp(log_gamma[:, None] * jnp.maximum(nv - 1.0 - idx, 0.0))[:, :, None]
    c_dec = jnp.exp(log_gamma * nv)[:, None, None]
    if s0 is None:
        s0 = jnp.zeros((b, RT_H, RT_DK, RT_DV), F32)
    nb = seqs_per_step
    assert b % nb == 0
    seq3 = lambda z_: _pad_time(z_, b, t, tp).reshape(b, tp, z_.shape[-1])
    rowk = pl.BlockSpec((nb, chunk, kd), lambda bi, c: (bi, c, 0))
    rowv = pl.BlockSpec((nb, chunk, vd), lambda bi, c: (bi, c, 0))
    stt = pl.BlockSpec((nb, RT_H, RT_DK, RT_DV), lambda bi, c: (bi, 0, 0, 0))
    y, st = pl.pallas_call(
        _ret_core_kernel,
        out_shape=(jax.ShapeDtypeStruct((b, tp, vd), BF16),
                   jax.ShapeDtypeStruct((b, RT_H, RT_DK, RT_DV), F32)),
        grid=(b // nb, nc),
        in_specs=[rowk, rowk, rowv, rowv, _full(dmask.shape), _full(q_dec.shape), _full(k_dec.shape),
                  _full(c_dec.shape), stt, _full((1, vd))],
        out_specs=(rowv, stt),
        scratch_shapes=[pltpu.VMEM((nb, RT_H, RT_DK, RT_DV), F32)],
        compiler_params=_cparams(("arbitrary", "arbitrary")),
        name="ret_core",
    )(seq3(q), seq3(k), seq3(v), seq3(gt), dmask, q_dec, k_dec, c_dec, s0, norm_w.reshape(1, vd))
    y = y.reshape(b * tp, vd)
    x_new = _finish(grp, route, x, _unpad_time(y, b, t, tp), w_o.astype(BF16), gate)
    return x_new, st


def _hgrn_proj_kernel(layer, x_ref, g_ref, sh_ref, sc_ref, lbl_ref, w_ref,
                      q_ref, k_ref, lf_ref, v_ref, gate_ref):
    h = _modulate(x_ref[...], g_ref[...], sh_ref[0], sc_ref[0]).astype(BF16)
    ed = HG_H * HG_E
    logits = lbl_ref[...]
    e = jnp.exp(logits - jnp.max(logits, axis=0, keepdims=True))
    lrow = lax.broadcasted_iota(I32, logits.shape, 0)
    part = jnp.where(jnp.logical_and(lrow >= 1, lrow <= layer), e, 0.0)
    lb = jnp.sum(part, axis=0, keepdims=True) / jnp.sum(e, axis=0, keepdims=True)
    q_ref[...] = jnp.dot(h, w_ref[:, 0:ed], preferred_element_type=F32)
    f = lb + (1.0 - lb) * _sigmoid(jnp.dot(h, w_ref[:, ed:2 * ed], preferred_element_type=F32))
    k_ref[...] = 1.0 - f
    lf_ref[...] = jnp.log(f)
    v_ref[...] = jnp.dot(h, w_ref[:, 2 * ed:3 * ed], preferred_element_type=F32).astype(BF16)
    gate_ref[...] = jnp.dot(h, w_ref[:, 3 * ed:], preferred_element_type=F32).astype(BF16)


def _hgrn_core_kernel(q_ref, k_ref, lf_ref, v_ref, gate_ref, s0_ref, nw_ref, y_ref, st_ref, state):
    cidx = pl.program_id(1)
    nb, c = q_ref.shape[0], q_ref.shape[1]

    @pl.when(cidx == 0)
    def _():
        state[...] = s0_ref[...]

    ri = lax.broadcasted_iota(I32, (c, c), 0)
    ci = lax.broadcasted_iota(I32, (c, c), 1)
    ltri = (ri >= ci).astype(F32)
    row8 = lax.broadcasted_iota(I32, (8, 1), 0)
    hsl = [slice(h * HG_E, (h + 1) * HG_E) for h in range(HG_H)]
    pairs = [(s, h) for s in range(nb) for h in range(HG_H)]
    heads = range(len(pairs))
    sls = [hsl[h] for _, h in pairs]
    q = [q_ref[s, :, hsl[h]] for s, h in pairs]
    k = [k_ref[s, :, hsl[h]] for s, h in pairs]
    v = [v_ref[s, :, hsl[h]].astype(F32) for s, h in pairs]
    st = [state[s, h] for s, h in pairs]
    cum = [_fdot(ltri, lf_ref[s, :, hsl[h]]) for s, h in pairs]
    inter = [_bdot_nt(q[h] * jnp.exp(cum[h]), st[h]) for h in heads]
    last = [cum[h][c - 1:c, :] for h in heads]
    s_new = [st[h] * jnp.exp(last[h]) + _bdot_tn(v[h], k[h] * jnp.exp(last[h] - cum[h])) for h in heads]
    for h in heads:
        seq = pairs[h][0]
        state[seq, pairs[h][1]] = s_new[h]
        parts = []
        for g0 in range(0, c, 8):
            qg, cg = q[h][g0:g0 + 8], cum[h][g0:g0 + 8]
            acc = inter[h][g0:g0 + 8]
            for j in range(g0 + 8):
                diff = cg - cum[h][j:j + 1, :]
                if j >= g0:
                    causal = row8 >= (j - g0)
                    diff = jnp.where(causal, diff, 0.0)
                col = jnp.sum(qg * k[h][j:j + 1, :] * jnp.exp(diff), axis=-1, keepdims=True)
                if j >= g0:
                    col = jnp.where(causal, col, 0.0)
                acc = acc + col * v[h][j:j + 1, :]
            parts.append(acc)
        o = parts[0] if len(parts) == 1 else jnp.concatenate(parts, axis=0)
        on = o * lax.rsqrt(jnp.mean(o * o, axis=-1, keepdims=True) + NORM_EPS) * nw_ref[:, sls[h]]
        y_ref[seq, :, sls[h]] = (on * _silu(gate_ref[seq, :, sls[h]].astype(F32))).astype(BF16)

    @pl.when(cidx == pl.num_programs(1) - 1)
    def _():
        st_ref[...] = state[...]


def _hgrn_layer(grp, x, norm_g, mods, s0, layer, lb_logits, w_in, norm_w, w_o, chunk, seqs_per_step=1,
                route=None):
    b, t = grp.b, grp.t
    shift, scale, gate = mods
    sh, sh_spec = grp.rowmod(shift)
    sc, sc_spec = grp.rowmod(scale)
    ed, vd = HG_H * HG_E, HG_H * HG_DV
    wb = w_in.astype(BF16)
    q, k, lf, v, gt = pl.pallas_call(
        functools.partial(_hgrn_proj_kernel, layer),
        out_shape=(jax.ShapeDtypeStruct((grp.n, ed), F32), jax.ShapeDtypeStruct((grp.n, ed), F32),
                   jax.ShapeDtypeStruct((grp.n, ed), F32), jax.ShapeDtypeStruct((grp.n, vd), BF16),
                   jax.ShapeDtypeStruct((grp.n, vd), BF16)),
        grid=(grp.tiles,),
        in_specs=[grp.rows(D), _full((1, D)), sh_spec, sc_spec, _full(lb_logits.shape), _full(wb.shape)],
        out_specs=(grp.rows(ed), grp.rows(ed), grp.rows(ed), grp.rows(vd), grp.rows(vd)),
        compiler_params=_cparams(("arbitrary",)),
        name="hgrn_proj",
    )(x, norm_g.reshape(1, D), sh, sc, lb_logits, wb)
    tp = ((t + chunk - 1) // chunk) * chunk
    nc = tp // chunk
    if s0 is None:
        s0 = jnp.zeros((b, HG_H, HG_E, HG_DV), F32)
    nb = seqs_per_step
    assert b % nb == 0
    seq3 = lambda z_: _pad_time(z_, b, t, tp).reshape(b, tp, z_.shape[-1])
    row = pl.BlockSpec((nb, chunk, ed), lambda bi, c: (bi, c, 0))
    stt = pl.BlockSpec((nb, HG_H, HG_E, HG_DV), lambda bi, c: (bi, 0, 0, 0))
    y, st = pl.pallas_call(
        _hgrn_core_kernel,
        out_shape=(jax.ShapeDtypeStruct((b, tp, vd), BF16),
                   jax.ShapeDtypeStruct((b, HG_H, HG_E, HG_DV), F32)),
        grid=(b // nb, nc),
        in_specs=[row, row, row, row, row, stt, _full((1, vd))],
        out_specs=(row, stt),
        scratch_shapes=[pltpu.VMEM((nb, HG_H, HG_E, HG_DV), F32)],
        compiler_params=_cparams(("arbitrary", "arbitrary")),
        name="hgrn_core",
    )(seq3(q), seq3(k), seq3(lf), seq3(v), seq3(gt), jnp.swapaxes(s0, 2, 3), norm_w.reshape(1, vd))
    y = y.reshape(b * tp, vd)
    x_new = _finish(grp, route, x, _unpad_time(y, b, t, tp), w_o.astype(BF16), gate)
    return x_new, jnp.swapaxes(st, 2, 3)


ROW_TILE = 256
GDN_CHUNK, RET_CHUNK, HGRN_CHUNK = 64, 128, 16
SAMPLE_CHUNK = 16
GDN_SEQS, RET_SEQS, HGRN_SEQS = (4, 8), (2, 2), (4, 4)
PAST_LEN = 16384


def kernel(x_prompt, x_sample, c_prompt, c_sample, state_rwkv_wkv, state_rwkv_shift, state_gdn_ssm, state_gdn_conv, state_ret, state_hgrn, ada_w, ada_b, norm_mix, norm_ffn, norm_final, rwkv_mu, rwkv_w_rkv, rwkv_w0, rwkv_w1, rwkv_w2, rwkv_a0, rwkv_a1, rwkv_a2, rwkv_g1, rwkv_g2, rwkv_k_k, rwkv_k_a, rwkv_r_k, rwkv_ln_w, rwkv_ln_b, rwkv_w_o, gdn_w_in, gdn_conv_w, gdn_a_log, gdn_dt_bias, gdn_norm_w, gdn_w_o, ret_w_in, ret_norm_w, ret_w_o, hgrn_w_in, hgrn_lb_logits, hgrn_norm_w, hgrn_w_o, moe_w_router, moe_b_router, moe_w_gu, moe_b_gu, moe_w_down, moe_b_down):
    bp, tp, _ = x_prompt.shape
    bs, ts, _ = x_sample.shape
    gp, gs = _Group(bp, tp, ROW_TILE), _Group(bs, ts, ROW_TILE)
    ada = _ada_call(jnp.concatenate([c_prompt, c_sample], axis=0), ada_w, ada_b)
    xp = x_prompt.reshape(bp * tp, D)
    xs = x_sample.reshape(bs * ts, D)
    outs_p = {k: [] for k in ("wkv", "shift", "ssm", "conv", "ret", "hgrn")}
    outs_s = {k: [] for k in ("wkv", "shift", "ssm", "conv", "ret", "hgrn")}
    for i in range(DEPTH):
        kind, j = i % 4, i // 4
        m = ada[i].reshape(bp + bs, 6, D)
        mod_p = [m[:bp, n] for n in range(6)]
        mod_s = [m[bp:, n] for n in range(6)]
        g = norm_mix[i]
        rt_p = (norm_ffn[i], mod_p[3], mod_p[4], moe_w_router[i], moe_b_router[i])
        rt_s = (norm_ffn[i], mod_s[3], mod_s[4], moe_w_router[i], moe_b_router[i])
        route_p = (jnp.zeros((1, N_EXPERTS), F32),) + rt_p
        if kind == 0:
            prm = (rwkv_mu[j], rwkv_w_rkv[j], rwkv_w0[j], rwkv_w1[j], rwkv_w2[j], rwkv_a0[j], rwkv_a1[j],
                   rwkv_a2[j], rwkv_g1[j], rwkv_g2[j], rwkv_k_k[j], rwkv_k_a[j], rwkv_r_k[j],
                   rwkv_ln_w[j], rwkv_ln_b[j], rwkv_w_o[j])
            xp, sh_p, wkv_p = _rwkv_layer(gp, xp, g, mod_p[:3], None, None, *prm, route=route_p)
            xs, sh_s, wkv_s = _rwkv_layer(gs, xs, g, mod_s[:3], state_rwkv_shift[j], state_rwkv_wkv[j], *prm,
                                          route=(xp[5],) + rt_s)
            outs_p["wkv"].append(wkv_p); outs_p["shift"].append(sh_p)
            outs_s["wkv"].append(wkv_s); outs_s["shift"].append(sh_s)
        elif kind == 1:
            prm = (gdn_w_in[j], gdn_conv_w[j], gdn_a_log[j], gdn_dt_bias[j], gdn_norm_w[j], gdn_w_o[j])
            xp, cv_p, ss_p = _gdn_layer(gp, xp, g, mod_p[:3], None, None, *prm, GDN_CHUNK, GDN_SEQS[0],
                                        route=route_p)
            xs, cv_s, ss_s = _gdn_layer(gs, xs, g, mod_s[:3], state_gdn_conv[j], state_gdn_ssm[j], *prm,
                                        SAMPLE_CHUNK, GDN_SEQS[1], route=(xp[5],) + rt_s)
            outs_p["ssm"].append(ss_p); outs_p["conv"].append(cv_p)
            outs_s["ssm"].append(ss_s); outs_s["conv"].append(cv_s)
        elif kind == 2:
            prm = (ret_w_in[j], ret_norm_w[j], ret_w_o[j])
            xp, r_p = _ret_layer(gp, xp, g, mod_p[:3], None, 0, *prm, RET_CHUNK, RET_SEQS[0], route=route_p)
            xs, r_s = _ret_layer(gs, xs, g, mod_s[:3], state_ret[j], PAST_LEN, *prm, SAMPLE_CHUNK,
                                  RET_SEQS[1], route=(xp[5],) + rt_s)
            outs_p["ret"].append(r_p); outs_s["ret"].append(r_s)
        else:
            prm = (i, hgrn_lb_logits, hgrn_w_in[j], hgrn_norm_w[j], hgrn_w_o[j])
            xp, h_p = _hgrn_layer(gp, xp, g, mod_p[:3], None, *prm, HGRN_CHUNK, HGRN_SEQS[0], route=route_p)
            xs, h_s = _hgrn_layer(gs, xs, g, mod_s[:3], state_hgrn[j], *prm, SAMPLE_CHUNK, HGRN_SEQS[1],
                                   route=(xp[5],) + rt_s)
            outs_p["hgrn"].append(h_p); outs_s["hgrn"].append(h_s)
        last = i == DEPTH - 1
        xp, xs = _moe(i, [gp, gs], [xp, xs], [mod_p[5], mod_s[5]], moe_w_gu, moe_b_gu, moe_w_down, moe_b_down,
                      norm_final if last else None)
    y_prompt = xp[1].reshape(bp, tp, D)
    y_sample = xs[1].reshape(bs, ts, D)
    order = ("wkv", "shift", "ssm", "conv", "ret", "hgrn")
    return ((y_prompt, y_sample) + tuple(jnp.stack(outs_p[k]) for k in order)
            + tuple(jnp.stack(outs_s[k]) for k in order))
```

```python
import functools
import math

import jax
import jax.numpy as jnp
from jax import lax
from jax.experimental import pallas as pl
from jax.experimental.pallas import tpu as pltpu

F32 = jnp.float32
BF16 = jnp.bfloat16
I32 = jnp.int32
HIGHEST = lax.Precision.HIGHEST

D = 1024
DEPTH = 4
NORM_EPS = 1e-6
RW_H, RW_N = 16, 64
RW_GN_EPS = 64e-5
GD_H, GD_DK, GD_DV, GD_CONV = 8, 128, 128, 4
GD_C = 3 * GD_H * GD_DK
RT_H, RT_DK, RT_DV = 4, 256, 512
HG_H, HG_E, HG_DV = 8, 128, 128
N_EXPERTS, TOP_K, D_FF = 32, 4, 1024
SWIGLU_LIMIT, SWIGLU_ALPHA = 7.0, 1.702

LANES = 128
EXPERT_TILE = 512
ROUTER_TILE = 512
TOKEN_BLOCK = 256
VMEM_LIMIT = 56 * 1024 * 1024


def _cparams(sem, vmem=VMEM_LIMIT):
    return pltpu.CompilerParams(dimension_semantics=sem, vmem_limit_bytes=vmem)


def _sigmoid(x):
    return 1.0 / (1.0 + jnp.exp(-x))


def _silu(x):
    return x * _sigmoid(x)


def _softplus(x):
    return jnp.maximum(x, 0.0) + jnp.log(1.0 + jnp.exp(-jnp.abs(x)))


def _modulate(x, g, shift, scale):
    ms = jnp.mean(x * x, axis=-1, keepdims=True)
    return (x * lax.rsqrt(ms + NORM_EPS) * g) * (1.0 + scale) + shift


def _bdot(a, b):
    return jnp.dot(a.astype(BF16), b.astype(BF16), preferred_element_type=F32)


def _bdot_nt(a, b):
    return lax.dot_general(a.astype(BF16), b.astype(BF16), (((1,), (1,)), ((), ())),
                           preferred_element_type=F32)


def _bdot_tn(a, b):
    return lax.dot_general(a.astype(BF16), b.astype(BF16), (((0,), (0,)), ((), ())),
                           preferred_element_type=F32)


def _fdot(a, b):
    return jnp.dot(a, b, precision=HIGHEST, preferred_element_type=F32)


def _split_dot(a, b):
    a_hi, b_hi = a.astype(BF16), b.astype(BF16)
    a_lo = (a - a_hi.astype(F32)).astype(BF16)
    b_lo = (b - b_hi.astype(F32)).astype(BF16)
    dot = lambda x, y: jnp.dot(x, y, preferred_element_type=F32)
    return dot(a_hi, b_hi) + (dot(a_hi, b_lo) + dot(a_lo, b_hi))


class _Group:
    def __init__(self, b, t, tm):
        self.b, self.t, self.n = b, t, b * t
        self.tm = min(tm, self.n)
        assert self.n % self.tm == 0
        assert (self.t % self.tm == 0) or (self.tm % self.t == 0)
        self.per_batch = self.t % self.tm == 0
        self.tiles = self.n // self.tm

    def rows(self, width):
        return pl.BlockSpec((self.tm, width), lambda i: (i, 0))

    def rowmod(self, arr):
        w = arr.shape[-1]
        if self.per_batch:
            k = self.t // self.tm
            return arr.reshape(self.b, 1, w), pl.BlockSpec((1, 1, w), lambda i: (i // k, 0, 0))
        rep = jnp.repeat(arr, self.t, axis=0).reshape(self.tiles, self.tm, w)
        return rep, pl.BlockSpec((1, self.tm, w), lambda i: (i, 0, 0))

    def rowseq(self, arr):
        w = arr.shape[-1]
        if self.per_batch:
            assert arr.shape[1] == 1
            k = self.t // self.tm
            return arr, pl.BlockSpec((1, 1, w), lambda i: (i // k, 0, 0))
        return arr.reshape(self.tiles, self.tm, w), pl.BlockSpec((1, self.tm, w), lambda i: (i, 0, 0))

    def postab(self, tab):
        w = tab.shape[-1]
        if self.per_batch:
            k = self.t // self.tm
            return tab.reshape(k, self.tm, w), pl.BlockSpec((1, self.tm, w), lambda i: (i % k, 0, 0))
        rep = jnp.tile(tab, (self.tm // self.t, 1)).reshape(1, self.tm, w)
        return rep, pl.BlockSpec((1, self.tm, w), lambda i: (0, 0, 0))

    def prev8(self, width):
        k = self.tm // 8
        return pl.BlockSpec((8, width), lambda i: (jnp.maximum(i * k - 1, 0), 0))


def _full(shape):
    nd = len(shape)
    return pl.BlockSpec(shape, lambda *a: (0,) * nd)


def _tpos(tm, t):
    row = pl.program_id(0) * tm + lax.broadcasted_iota(I32, (tm, 1), 0)
    return row % t


def _stage_rows(scr, cur, prev8):
    scr[0:8, :] = prev8
    scr[8:, :] = cur


def _shifted_rows(scr, d):
    return scr[pl.ds(8 - d, scr.shape[0] - 8), :]


def _ada_kernel(c_ref, w_ref, b_ref, o_ref):
    o_ref[0] = _bdot(_silu(c_ref[...]), w_ref[0]) + b_ref[0]


def _ada_call(c_all, ada_w, ada_b):
    nb = c_all.shape[0]
    tn = 1536
    return pl.pallas_call(
        _ada_kernel,
        out_shape=jax.ShapeDtypeStruct((DEPTH, nb, 6 * D), F32),
        grid=(DEPTH, 6 * D // tn),
        in_specs=[pl.BlockSpec((nb, D), lambda l, j: (0, 0)),
                  pl.BlockSpec((1, D, tn), lambda l, j: (l, 0, j)),
                  pl.BlockSpec((1, 1, tn), lambda l, j: (l, 0, j))],
        out_specs=pl.BlockSpec((1, nb, tn), lambda l, j: (l, 0, j)),
        compiler_params=_cparams(("arbitrary", "arbitrary")),
        name="adaln",
    )(c_all, ada_w, ada_b.reshape(DEPTH, 1, 6 * D))


def _modrows_kernel(x_ref, g_ref, sh_ref, sc_ref, *rest):
    h = _modulate(x_ref[...], g_ref[...], sh_ref[...], sc_ref[...])
    if len(rest) == 2:
        w_ref, o_ref = rest
        o_ref[...] = _bdot(h, w_ref[...])
    else:
        rest[0][...] = h


def _modrows_call(x, g, shift, scale, w=None):
    n = x.shape[0]
    args = [x, g.reshape(1, D), shift, scale]
    specs = [_full((n, D)), _full((1, D)), _full((n, D)), _full((n, D))]
    width = D
    if w is not None:
        args.append(w)
        specs.append(_full(w.shape))
        width = w.shape[1]
    return pl.pallas_call(
        _modrows_kernel,
        out_shape=jax.ShapeDtypeStruct((n, width), F32),
        grid=(1,),
        in_specs=specs,
        out_specs=_full((n, width)),
        compiler_params=_cparams(("arbitrary",)),
        name="modrows",
    )(*args)


def _outproj_kernel(has_mul, x_ref, y_ref, *rest):
    if has_mul:
        m_ref, w_ref, gt_ref, o_ref = rest
        y = y_ref[...].astype(F32) * m_ref[...].astype(F32)
    else:
        w_ref, gt_ref, o_ref = rest
        y = y_ref[...]
    o_ref[...] = x_ref[...] + gt_ref[0] * _bdot(y, w_ref[...])


def _outproj_call(grp, x, y, w_o, gate, mul=None):
    dy = y.shape[1]
    gt, gt_spec = grp.rowmod(gate)
    args = [x, y]
    specs = [grp.rows(D), grp.rows(dy)]
    if mul is not None:
        args.append(mul)
        specs.append(grp.rows(dy))
    args += [w_o, gt]
    specs += [_full(w_o.shape), gt_spec]
    return pl.pallas_call(
        functools.partial(_outproj_kernel, mul is not None),
        out_shape=jax.ShapeDtypeStruct((grp.n, D), F32),
        grid=(grp.tiles,),
        in_specs=specs,
        out_specs=grp.rows(D),
        compiler_params=_cparams(("arbitrary",)),
        name="outproj",
    )(*args)


def _route_rows(x, cin_ref, g_ref, sh_ref, sc_ref, wr_ref, br_ref,
                h_ref, idx_ref, gate_ref, rank_ref, cnt_ref, carry):
    i = pl.program_id(0)

    @pl.when(i == 0)
    def _():
        carry[...] = cin_ref[...]

    tm = x.shape[0]
    h = _modulate(x, g_ref[...], sh_ref[0], sc_ref[0])
    h_ref[...] = h
    logits = _split_dot(h, wr_ref[...]) + br_ref[...]
    lane = lax.broadcasted_iota(I32, logits.shape, 1)
    work = logits
    sel = jnp.zeros(logits.shape, jnp.bool_)
    picks, vals = [], []
    for _ in range(TOP_K):
        m = jnp.max(work, axis=-1, keepdims=True)
        idx = jnp.min(jnp.where(work == m, lane, N_EXPERTS), axis=-1, keepdims=True)
        pick = lane == idx
        picks.append((idx, pick))
        vals.append(m)
        sel = jnp.logical_or(sel, pick)
        work = jnp.where(pick, -jnp.inf, work)
    es = [jnp.exp(v - vals[0]) for v in vals]
    denom = es[0] + es[1] + es[2] + es[3]
    self_f = sel.astype(F32)
    tri = (lax.broadcasted_iota(I32, (tm, tm), 0) > lax.broadcasted_iota(I32, (tm, tm), 1))
    local = jnp.dot(tri.astype(BF16), self_f.astype(BF16), preferred_element_type=F32)
    rank = local + carry[...]
    carry[...] = carry[...] + jnp.sum(self_f, axis=0, keepdims=True)
    cnt_ref[...] = carry[...]
    lane_o = lax.broadcasted_iota(I32, (tm, LANES), 1)
    idx_o = jnp.zeros((tm, LANES), I32)
    gate_o = jnp.zeros((tm, LANES), F32)
    rank_o = jnp.zeros((tm, LANES), I32)
    for k in range(TOP_K):
        idx, pick = picks[k]
        rk = jnp.sum(jnp.where(pick, rank, 0.0), axis=-1, keepdims=True)
        idx_o = jnp.where(lane_o == k, idx, idx_o)
        gate_o = jnp.where(lane_o == k, es[k] / denom, gate_o)
        rank_o = jnp.where(lane_o == k, rk.astype(I32), rank_o)
    idx_ref[...] = idx_o
    gate_ref[...] = gate_o
    rank_ref[...] = rank_o


def _router_kernel(cin_ref, x_ref, g_ref, sh_ref, sc_ref, wr_ref, br_ref,
                   h_ref, idx_ref, gate_ref, rank_ref, cnt_ref, carry):
    _route_rows(x_ref[...], cin_ref, g_ref, sh_ref, sc_ref, wr_ref, br_ref,
                h_ref, idx_ref, gate_ref, rank_ref, cnt_ref, carry)


def _router_call(grp, route, x):
    counts_in, g2, shift2, scale2, w_router, b_router = route
    sh, sh_spec = grp.rowmod(shift2)
    sc, sc_spec = grp.rowmod(scale2)
    pad = pl.BlockSpec((grp.tm, LANES), lambda i: (i, 0))
    outs = pl.pallas_call(
        _router_kernel,
        out_shape=(jax.ShapeDtypeStruct((grp.n, D), F32),
                   jax.ShapeDtypeStruct((grp.n, LANES), I32),
                   jax.ShapeDtypeStruct((grp.n, LANES), F32),
                   jax.ShapeDtypeStruct((grp.n, LANES), I32),
                   jax.ShapeDtypeStruct((1, N_EXPERTS), F32)),
        grid=(grp.tiles,),
        in_specs=[_full((1, N_EXPERTS)), grp.rows(D), _full((1, D)), sh_spec, sc_spec,
                  _full((D, N_EXPERTS)), _full((1, N_EXPERTS))],
        out_specs=(grp.rows(D), pad, pad, pad, _full((1, N_EXPERTS))),
        scratch_shapes=[pltpu.VMEM((1, N_EXPERTS), F32)],
        compiler_params=_cparams(("arbitrary",)),
        name="moe_router",
    )(counts_in, x, g2.reshape(1, D), sh, sc, w_router, b_router.reshape(1, N_EXPERTS))
    return (x,) + tuple(outs)


def _dispatch_kernel(cnt_ref, off_ref, nv_ref, pos_ref, h_ref, xs_ref, zbuf, sem, sem_z):
    i = pl.program_id(0)

    for t in range(TOKEN_BLOCK):
        for k in range(TOP_K):
            pltpu.make_async_copy(h_ref.at[pl.ds(t, 1)], xs_ref.at[pl.ds(pos_ref[t * TOP_K + k], 1)],
                                  sem).start(priority=k % 2)
    for k in range(TOP_K):
        pltpu.make_async_copy(h_ref, xs_ref.at[pl.ds(0, TOKEN_BLOCK)], sem).wait()

    @pl.when(i == pl.num_programs(0) - 1)
    def _():
        zbuf[...] = jnp.zeros(zbuf.shape, F32)
        bits = [1 << s for s in range(EXPERT_TILE.bit_length() - 2, 2, -1)]

        def pad_copies(e, wait):
            n = cnt_ref[e]
            start = off_ref[e] + n
            end = off_ref[e] + ((n + EXPERT_TILE - 1) // EXPERT_TILE) * EXPERT_TILE
            head = (-start) & 7

            def one(r, c):
                cp = pltpu.make_async_copy(zbuf.at[pl.ds(0, 1)], xs_ref.at[pl.ds(start + r, 1)], sem_z)
                if wait:
                    cp.wait()
                else:
                    cp.start()
                return c

            lax.fori_loop(0, head, one, 0)
            start8 = start + head
            rem = end - start8
            for bit in bits:
                @pl.when((rem & bit) != 0)
                def _():
                    s = pl.multiple_of(start8 + (rem & ~(2 * bit - 1)), 8)
                    cp = pltpu.make_async_copy(zbuf.at[pl.ds(0, bit)], xs_ref.at[pl.ds(s, bit)], sem_z)
                    if wait:
                        cp.wait()
                    else:
                        cp.start()

        def tail_copy(j):
            return pltpu.make_async_copy(zbuf, xs_ref.at[pl.ds(j * EXPERT_TILE, EXPERT_TILE)], sem_z)

        n_tiles = xs_ref.shape[0] // EXPERT_TILE
        for wait in (False, True):
            def per_expert(e, c):
                pad_copies(e, wait)
                return c

            def per_tail(j, c):
                if wait:
                    tail_copy(j).wait()
                else:
                    tail_copy(j).start()
                return c

            lax.fori_loop(0, N_EXPERTS, per_expert, 0)
            lax.fori_loop(nv_ref[0], n_tiles, per_tail, 0)


def _dispatch_call(counts, offsets, n_valid, pos_flat, h, n_rows):
    n = h.shape[0]
    return pl.pallas_call(
        _dispatch_kernel,
        out_shape=jax.ShapeDtypeStruct((n_rows, D), F32),
        grid_spec=pltpu.PrefetchScalarGridSpec(
            num_scalar_prefetch=3,
            grid=(n // TOKEN_BLOCK,),
            in_specs=[pl.BlockSpec((TOKEN_BLOCK * TOP_K,), lambda i, c, o, v: (i,), memory_space=pltpu.SMEM),
                      pl.BlockSpec((TOKEN_BLOCK, D), lambda i, c, o, v: (i, 0))],
            out_specs=pl.BlockSpec(memory_space=pl.ANY),
            scratch_shapes=[pltpu.VMEM((EXPERT_TILE, D), F32), pltpu.SemaphoreType.DMA(()),
                            pltpu.SemaphoreType.DMA(())]),
        compiler_params=_cparams(("arbitrary",)),
        name="moe_dispatch",
    )(counts, offsets, n_valid, pos_flat, h)


def _expert_kernel(te_ref, nv_ref, x_ref, wgu_ref, bgu_ref, wd_ref, bd_ref, o_ref, wgu_s, wd_s):
    j = pl.program_id(0)
    fresh = jnp.logical_or(j == 0, te_ref[j] != te_ref[jnp.maximum(j - 1, 0)])

    @pl.when(jnp.logical_and(j < nv_ref[0], fresh))
    def _():
        wgu_s[...] = wgu_ref[0, 0].astype(BF16)
        wd_s[...] = wd_ref[0, 0].astype(BF16)

    @pl.when(j < nv_ref[0])
    def _():
        x = x_ref[...].astype(BF16)
        gu = jnp.dot(x, wgu_s[...], preferred_element_type=F32) + bgu_ref[0, 0]
        gl = jnp.minimum(gu[:, :D_FF], SWIGLU_LIMIT)
        up = jnp.clip(gu[:, D_FF:], -SWIGLU_LIMIT, SWIGLU_LIMIT)
        act = (up + 1.0) * gl * _sigmoid(SWIGLU_ALPHA * gl)
        o_ref[...] = jnp.dot(act.astype(BF16), wd_s[...], preferred_element_type=F32) + bd_ref[0, 0]

    @pl.when(j >= nv_ref[0])
    def _():
        o_ref[...] = jnp.zeros(o_ref.shape, F32)


def _expert_call(layer, tile_expert, n_valid, xs, w_gu, b_gu, w_down, b_down):
    n_rows = xs.shape[0]
    g = n_rows // EXPERT_TILE
    return pl.pallas_call(
        _expert_kernel,
        out_shape=jax.ShapeDtypeStruct((n_rows, D), F32),
        grid_spec=pltpu.PrefetchScalarGridSpec(
            num_scalar_prefetch=2,
            grid=(g,),
            in_specs=[pl.BlockSpec((EXPERT_TILE, D), lambda j, te, nv: (jnp.minimum(j, nv[0] - 1), 0)),
                      pl.BlockSpec((1, 1, D, 2 * D_FF), lambda j, te, nv: (layer, te[j], 0, 0)),
                      pl.BlockSpec((1, 1, 1, 2 * D_FF), lambda j, te, nv: (layer, te[j], 0, 0)),
                      pl.BlockSpec((1, 1, D_FF, D), lambda j, te, nv: (layer, te[j], 0, 0)),
                      pl.BlockSpec((1, 1, 1, D), lambda j, te, nv: (layer, te[j], 0, 0))],
            out_specs=pl.BlockSpec((EXPERT_TILE, D), lambda j, te, nv: (j, 0)),
            scratch_shapes=[pltpu.VMEM((D, 2 * D_FF), BF16), pltpu.VMEM((D_FF, D), BF16)]),
        compiler_params=_cparams(("arbitrary",)),
        name="moe_experts",
    )(tile_expert, n_valid, xs, w_gu, b_gu.reshape(-1, N_EXPERTS, 1, 2 * D_FF), w_down,
      b_down.reshape(-1, N_EXPERTS, 1, D))


def _combine_kernel(with_norm, pos_ref, x_ref, gate_ref, gt_ref, ys_ref, *rest):
    if with_norm:
        ng_ref, o_ref, y_ref, buf, sems = rest
    else:
        o_ref, buf, sems = rest
    half = TOKEN_BLOCK // 2
    for hf in range(2):
        for t in range(hf * half, (hf + 1) * half):
            for k in range(TOP_K):
                pltpu.make_async_copy(ys_ref.at[pl.ds(pos_ref[t * TOP_K + k], 1)], buf.at[k, pl.ds(t, 1)],
                                      sems.at[hf]).start(priority=k % 2)
    for hf in range(2):
        rows = pl.ds(hf * half, half)
        for k in range(TOP_K):
            pltpu.make_async_copy(ys_ref.at[pl.ds(0, half)], buf.at[k, rows], sems.at[hf]).wait()
        gate = gate_ref[rows, :]
        f = gate[:, 0:1] * buf[0, rows]
        for k in range(1, TOP_K):
            f = f + gate[:, k:k + 1] * buf[k, rows]
        gt = gt_ref[0]
        x_new = x_ref[rows, :] + (gt if gt.shape[0] == 1 else gt[hf * half:(hf + 1) * half]) * f
        o_ref[rows, :] = x_new
        if with_norm:
            ms = jnp.mean(x_new * x_new, axis=-1, keepdims=True)
            y_ref[rows, :] = x_new * lax.rsqrt(ms + NORM_EPS) * ng_ref[...]


def _combine_call(grp, pos_flat, x, gate_pad, gate2, ys, final_g=None):
    gt, gt_spec = grp.rowmod(gate2)
    assert grp.tm == TOKEN_BLOCK
    with_norm = final_g is not None
    args = [pos_flat, x, gate_pad, gt, ys]
    specs = [pl.BlockSpec((TOKEN_BLOCK * TOP_K,), lambda i: (i,), memory_space=pltpu.SMEM),
             grp.rows(D), grp.rows(LANES), gt_spec, pl.BlockSpec(memory_space=pl.ANY)]
    shape = jax.ShapeDtypeStruct((grp.n, D), F32)
    if with_norm:
        args.append(final_g.reshape(1, D))
        specs.append(_full((1, D)))
    return pl.pallas_call(
        functools.partial(_combine_kernel, with_norm),
        out_shape=(shape, shape) if with_norm else shape,
        grid=(grp.tiles,),
        in_specs=specs,
        out_specs=(grp.rows(D), grp.rows(D)) if with_norm else grp.rows(D),
        scratch_shapes=[pltpu.VMEM((TOP_K, TOKEN_BLOCK, D), F32), pltpu.SemaphoreType.DMA((2,))],
        compiler_params=_cparams(("arbitrary",)),
        name="moe_combine",
    )(*args)


def _finish(grp, route, x, y, w_o, gate, mul=None):
    x_new = _outproj_call(grp, x, y, w_o, gate, mul)
    if route is None:
        return x_new
    return _router_call(_Group(grp.b, grp.t, ROUTER_TILE), route, x_new)


def _moe(layer, groups, routed, gates2, w_gu, b_gu, w_down, b_down, final_g=None):
    counts = routed[-1][5]
    xs_in = [r[0] for r in routed]
    hs = [r[1] for r in routed]
    idxs = [r[2][:, :TOP_K] for r in routed]
    gates = [r[3] for r in routed]
    ranks = [r[4][:, :TOP_K] for r in routed]
    mods = [(None, None, g2) for g2 in gates2]
    h_all = jnp.concatenate(hs, axis=0)
    idx_all = jnp.concatenate(idxs, axis=0)
    rank_all = jnp.concatenate(ranks, axis=0)
    n = h_all.shape[0]
    cnt = counts[0].astype(I32)
    padded = ((cnt + EXPERT_TILE - 1) // EXPERT_TILE) * EXPERT_TILE
    ends = jnp.cumsum(padded)
    offsets = ends - padded
    n_tiles = (n * TOP_K + N_EXPERTS * (EXPERT_TILE - 1)) // EXPERT_TILE
    n_rows = n_tiles * EXPERT_TILE
    pos = (jnp.take(offsets, idx_all) + rank_all).astype(I32)
    pos_flat = pos.reshape(n * TOP_K)
    n_valid = (ends[-1] // EXPERT_TILE).astype(I32)
    tile_start = jnp.arange(n_tiles, dtype=I32) * EXPERT_TILE
    tile_start = jnp.minimum(tile_start, ends[-1] - EXPERT_TILE)
    tile_expert = jnp.sum(tile_start[:, None] >= ends[None, :], axis=1).astype(I32)
    n_valid = n_valid.reshape(1)
    xs = _dispatch_call(cnt, offsets.astype(I32), n_valid, pos_flat, h_all, n_rows)
    ys = _expert_call(layer, tile_expert, n_valid, xs, w_gu, b_gu, w_down, b_down)
    outs = []
    start = 0
    for grp, x, gate, (_, _, gt2) in zip(groups, xs_in, gates, mods):
        cgrp = _Group(grp.b, grp.t, TOKEN_BLOCK)
        p = lax.dynamic_slice_in_dim(pos_flat, start * TOP_K, grp.n * TOP_K)
        outs.append(_combine_call(cgrp, p, x, gate, gt2, ys, final_g))
        start += grp.n
    return outs


def _rwkv_proj_kernel(t_len, x_ref, xp_ref, g_ref, sh_ref, sc_ref, s0_ref, mu_ref, wrkv_ref, w0_ref,
                      w1_ref, w2_ref, a0_ref, a1_ref, a2_ref, g1_ref, g2_ref,
                      r_ref, w_ref, k_ref, v_ref, a_ref, gg_ref, scr):
    tm = x_ref.shape[0]
    g, sh, sc = g_ref[...], sh_ref[0], sc_ref[0]
    h = _modulate(x_ref[...], g, sh, sc)
    hp = _modulate(xp_ref[...], g, sh[0:8] if sh.shape[0] > 1 else sh, sc[0:8] if sc.shape[0] > 1 else sc)
    _stage_rows(scr, h, hp)
    prev = jnp.where(_tpos(tm, t_len) == 0, s0_ref[0], _shifted_rows(scr, 1))
    dx = prev - h
    mu = mu_ref[...]
    xr, xw, xk, xv, xa, xg = [h + dx * mu[n:n + 1] for n in range(6)]
    r_ref[...] = _bdot(xr, wrkv_ref[0]).astype(BF16)
    k_ref[...] = _bdot(xk, wrkv_ref[1]).astype(BF16)
    v_ref[...] = _bdot(xv, wrkv_ref[2]).astype(BF16)
    w_log = -_softplus(-(w0_ref[...] + _bdot(jnp.tanh(_bdot(xw, w1_ref[...])), w2_ref[...]))) - 0.5
    w_ref[...] = jnp.exp(-jnp.exp(w_log))
    a_ref[...] = _sigmoid(a0_ref[...] + _bdot(_bdot(xa, a1_ref[...]), a2_ref[...])).astype(BF16)
    gg_ref[...] = _bdot(_sigmoid(_bdot(xg, g1_ref[...])), g2_ref[...]).astype(BF16)


def _rwkv_core_kernel(r_ref, w_ref, k_ref, v_ref, a_ref, kk_p, ka_p, rk_p, lnw_p, lnb_p, s0_ref,
                      y_ref, st_ref, state, kk_s, b_s, km_s, r_s):
    j = pl.program_id(1)
    tc = r_ref.shape[0]
    n = RW_N

    @pl.when(j == 0)
    def _():
        state[...] = s0_ref[...]

    def step(t, c):
        kt, at, vt, rt = [z[t].astype(F32) for z in (k_ref, a_ref, v_ref, r_ref)]
        r_s[...] = rt
        kk = kt * kk_p[...]
        kk = kk * lax.rsqrt(jnp.sum(kk * kk, axis=0, keepdims=True) + 1e-6)
        km = kt * (1.0 + (at - 1.0) * ka_p[...])
        kk_s[...] = kk
        b_s[...] = kk * at
        km_s[...] = km
        sa2 = [jnp.zeros((n, LANES), F32), jnp.zeros((n, LANES), F32)]
        for kx in range(n):
            sa2[kx % 2] = sa2[kx % 2] + state[kx] * kk_s[pl.ds(kx, 1), :]
        sa = sa2[0] + sa2[1]
        y2 = [jnp.zeros((n, LANES), F32), jnp.zeros((n, LANES), F32)]
        for kx in range(n):
            s_new = (state[kx] * w_ref[t, pl.ds(kx, 1), :] - sa * b_s[pl.ds(kx, 1), :]
                     + vt * km_s[pl.ds(kx, 1), :])
            state[kx] = s_new
            y2[kx % 2] = y2[kx % 2] + s_new * r_s[pl.ds(kx, 1), :]
        y = y2[0] + y2[1]
        mean = jnp.mean(y, axis=0, keepdims=True)
        yc = y - mean
        var = jnp.mean(yc * yc, axis=0, keepdims=True)
        bonus = jnp.sum(rt * km * rk_p[...], axis=0, keepdims=True) * vt
        y_ref[t] = (yc * lax.rsqrt(var + RW_GN_EPS) * lnw_p[...] + lnb_p[...] + bonus).astype(BF16)
        return c

    lax.fori_loop(0, tc, step, 0)

    @pl.when(j == pl.num_programs(1) - 1)
    def _():
        st_ref[...] = state[...]


def _rwkv_layer(grp, x, norm_g, mods, shift0, wkv0, mu, w_rkv, w0, w1, w2, a0, a1, a2, g1, g2,
                k_k, k_a, r_k, ln_w, ln_b, w_o, route=None):
    b, t = grp.b, grp.t
    shift, scale, gate = mods
    sh, sh_spec = grp.rowmod(shift)
    sc, sc_spec = grp.rowmod(scale)
    if shift0 is None:
        s0 = jnp.zeros((b, 1, D), F32)
    else:
        s0 = jnp.concatenate([shift0[:, None, :], jnp.zeros((b, t - 1, D), F32)], axis=1)
    s0, s0_spec = grp.rowseq(s0)
    bf = lambda z: z.astype(BF16)
    row = lambda z: z.reshape(1, -1)
    weights = [mu, bf(w_rkv), row(w0), bf(w1), bf(w2), row(a0), bf(a1), bf(a2), bf(g1), bf(g2)]
    outs = pl.pallas_call(
        functools.partial(_rwkv_proj_kernel, t),
        out_shape=tuple(jax.ShapeDtypeStruct((grp.n, D), dt) for dt in (BF16, F32, BF16, BF16, BF16, BF16)),
        grid=(grp.tiles,),
        in_specs=[grp.rows(D), grp.prev8(D), _full((1, D)), sh_spec, sc_spec, s0_spec]
        + [_full(z.shape) for z in weights],
        out_specs=tuple(grp.rows(D) for _ in range(6)),
        scratch_shapes=[pltpu.VMEM((grp.tm + 8, D), F32)],
        compiler_params=_cparams(("arbitrary",)),
        name="rwkv_proj",
    )(x, x, row(norm_g), sh, sc, s0, *weights)
    r, w, k, v, a, gg = outs
    bh = b * RW_H

    def to_core(z):
        return z.reshape(b, t, RW_H, RW_N).transpose(1, 3, 0, 2).reshape(t, RW_N, bh)

    def ptile(p):
        return jnp.tile(p.reshape(RW_H, RW_N).T, (1, b))

    if wkv0 is None:
        st0 = jnp.zeros((RW_N, RW_N, bh), F32)
    else:
        st0 = wkv0.transpose(3, 2, 0, 1).reshape(RW_N, RW_N, bh)
    tc = min(t, 32)
    seq = pl.BlockSpec((tc, RW_N, LANES), lambda q, j: (j, 0, q))
    par = pl.BlockSpec((RW_N, LANES), lambda q, j: (0, q))
    stt = pl.BlockSpec((RW_N, RW_N, LANES), lambda q, j: (0, 0, q))
    y, st = pl.pallas_call(
        _rwkv_core_kernel,
        out_shape=(jax.ShapeDtypeStruct((t, RW_N, bh), BF16),
                   jax.ShapeDtypeStruct((RW_N, RW_N, bh), F32)),
        grid=(bh // LANES, t // tc),
        in_specs=[seq] * 5 + [par] * 5 + [stt],
        out_specs=(seq, stt),
        scratch_shapes=[pltpu.VMEM((RW_N, RW_N, LANES), F32)] + [pltpu.VMEM((RW_N, LANES), F32)] * 4,
        compiler_params=_cparams(("arbitrary", "arbitrary")),
        name="rwkv_core",
    )(to_core(r), to_core(w), to_core(k), to_core(v), to_core(a),
      ptile(k_k), ptile(k_a), ptile(r_k.reshape(-1)), ptile(ln_w), ptile(ln_b), st0)
    y_rows = y.reshape(t, RW_N, b, RW_H).transpose(2, 0, 3, 1).reshape(grp.n, D)
    x_new = _finish(grp, route, x, y_rows, bf(w_o), gate, mul=gg)
    new_wkv = st.reshape(RW_N, RW_N, b, RW_H).transpose(2, 3, 1, 0)
    x_last = x.reshape(b, t, D)[:, -1]
    new_shift = _modrows_call(x_last, norm_g, shift, scale)
    return x_new, new_shift, new_wkv


def _pad_time(z, b, t, tp):
    if tp == t:
        return z
    w = z.shape[-1]
    return jnp.pad(z.reshape(b, t, w), ((0, 0), (0, tp - t), (0, 0))).reshape(b * tp, w)


def _unpad_time(z, b, t, tp):
    if tp == t:
        return z
    w = z.shape[-1]
    return z.reshape(b, tp, w)[:, :t].reshape(b * t, w)


def _gdn_proj_kernel(t_len, x_ref, xp_ref, g_ref, sh_ref, sc_ref, c1_ref, c2_ref, c3_ref, wqkv_ref,
                     wz_ref, wb_ref, wa_ref, cw_ref, alog_ref, dtb_ref,
                     qkv_ref, z_ref, beta_ref, gdec_ref, scr):
    tm = x_ref.shape[0]
    g, sh, sc = g_ref[...], sh_ref[0], sc_ref[0]
    h = _modulate(x_ref[...], g, sh, sc)
    hp = _modulate(xp_ref[...], g, sh[0:8] if sh.shape[0] > 1 else sh, sc[0:8] if sc.shape[0] > 1 else sc)
    hb = h.astype(BF16)
    pre = jnp.dot(hb, wqkv_ref[...], preferred_element_type=F32)
    pre8 = _bdot(hp, wqkv_ref[...])
    tpos = _tpos(tm, t_len)
    cw = cw_ref[...]
    conv = pre * cw[3:4]
    _stage_rows(scr, pre, pre8)
    for d, cref in ((1, c1_ref), (2, c2_ref), (3, c3_ref)):
        past = jnp.where(tpos >= d, _shifted_rows(scr, d), cref[0])
        conv = conv + past * cw[3 - d:4 - d]
    act = _silu(conv)
    nh = GD_H
    for hh in range(2 * nh):
        sl = slice(hh * GD_DK, (hh + 1) * GD_DK)
        seg = act[:, sl]
        seg = seg * lax.rsqrt(jnp.sum(seg * seg, axis=-1, keepdims=True) + 1e-6)
        if hh < nh:
            seg = seg * (GD_DK ** -0.5)
        qkv_ref[:, sl] = seg.astype(BF16)
    qkv_ref[:, 2 * nh * GD_DK:] = act[:, 2 * nh * GD_DK:].astype(BF16)
    z_ref[...] = jnp.dot(hb, wz_ref[...], preferred_element_type=F32).astype(BF16)
    beta_ref[...] = _sigmoid(jnp.dot(hb, wb_ref[...], preferred_element_type=F32))
    a_logit = jnp.dot(hb, wa_ref[...], preferred_element_type=F32)
    gdec_ref[...] = -jnp.exp(alog_ref[...]) * _softplus(a_logit + dtb_ref[...])


def _unit_lower_inverse(a, eye, masks):
    blk8, offs = masks
    n = range(len(a))
    a8 = [jnp.where(blk8, a[i], 0.0) for i in n]
    x = [eye - a8[i] for i in n]
    y = [_bdot(a8[i], a8[i]) for i in n]
    x = [x[i] + _bdot(x[i], y[i]) for i in n]
    y = [_bdot(y[i], y[i]) for i in n]
    x = [x[i] + _bdot(x[i], y[i]) for i in n]
    for off in offs:
        t = [_bdot(jnp.where(off, a[i], 0.0), x[i]) for i in n]
        x = [x[i] - _bdot(x[i], t[i]) for i in n]
    return x


def _inverse_masks(c):
    ri = lax.broadcasted_iota(I32, (c, c), 0)
    ci = lax.broadcasted_iota(I32, (c, c), 1)
    sr = lambda z, s: lax.shift_right_logical(z, jnp.full(z.shape, s, I32))
    blk8 = sr(ri, 3) == sr(ci, 3)
    offs = []
    m, lg = 8, 3
    while m < c:
        same = sr(ri, lg + 1) == sr(ci, lg + 1)
        lower = jnp.logical_and((sr(ri, lg) & 1) == 1, (sr(ci, lg) & 1) == 0)
        offs.append(jnp.logical_and(same, lower))
        m, lg = m * 2, lg + 1
    return ri, ci, (blk8, offs)


def _gdn_core_kernel(q_ref, k_ref, v_ref, z_ref, beta_ref, g_ref, s0_ref, nw_ref, y_ref, st_ref, state):
    cidx = pl.program_id(1)
    nb, c = q_ref.shape[0], q_ref.shape[1]

    @pl.when(cidx == 0)
    def _():
        state[...] = s0_ref[...]

    ri, ci, masks = _inverse_masks(c)
    incl = ri >= ci
    strict = ri > ci
    eye = (ri == ci).astype(F32)
    incl_f, incl_t = incl.astype(F32), (ci >= ri).astype(F32)
    nw = nw_ref[...]
    sls = [slice(h * GD_DK, (h + 1) * GD_DK) for h in range(GD_H)]
    pairs = [(s, h) for s in range(nb) for h in range(GD_H)]
    n = range(len(pairs))
    g = [g_ref[s] for s in range(nb)]
    cum = [_fdot(incl_f, g[s]) for s in range(nb)]
    cum_t = [lax.dot_general(g[s], incl_t, (((0,), (0,)), ((), ())), precision=HIGHEST,
                             preferred_element_type=F32) for s in range(nb)]
    beta = [beta_ref[s] for s in range(nb)]
    q = [q_ref[s, :, sls[h]] for s, h in pairs]
    k = [k_ref[s, :, sls[h]] for s, h in pairs]
    kf = [z.astype(F32) for z in k]
    v = [v_ref[s, :, sls[h]].astype(F32) for s, h in pairs]
    st = [state[s, h] for s, h in pairs]
    cum_c = [cum[s][:, h:h + 1] for s, h in pairs]
    dec = [jnp.where(incl, jnp.exp(jnp.where(incl, cum_c[i] - cum_t[s][h:h + 1, :], 0.0)), 0.0)
           for i, (s, h) in enumerate(pairs)]
    bcol = [beta[s][:, h:h + 1] for s, h in pairs]
    kb = [kf[i] * bcol[i] for i in n]
    a = [jnp.where(strict, _bdot_nt(kb[i], k[i]) * dec[i], 0.0) for i in n]
    attn = [_bdot_nt(q[i], k[i]) * dec[i] for i in n]
    x = _unit_lower_inverse(a, eye, masks)
    ecum = [jnp.exp(cum_c[i]) for i in n]
    sol = [_bdot(x[i], jnp.concatenate([v[i] * bcol[i], kb[i] * ecum[i]], axis=1)) for i in n]
    u = [sol[i][:, :GD_DV] - _bdot(sol[i][:, GD_DV:], st[i]) for i in n]
    o = [_bdot(q[i].astype(F32) * ecum[i], st[i]) + _bdot(attn[i], u[i]) for i in n]
    last = [cum[s][c - 1:c, h:h + 1] for s, h in pairs]
    s_new = [st[i] * jnp.exp(last[i]) + _bdot_tn(kf[i] * jnp.exp(last[i] - cum_c[i]), u[i]) for i in n]
    for i, (s, h) in enumerate(pairs):
        state[s, h] = s_new[i]
        on = o[i] * lax.rsqrt(jnp.mean(o[i] * o[i], axis=-1, keepdims=True) + NORM_EPS) * nw
        y_ref[s, :, sls[h]] = (on * _silu(z_ref[s, :, sls[h]].astype(F32))).astype(BF16)

    @pl.when(cidx == pl.num_programs(1) - 1)
    def _():
        st_ref[...] = state[...]


def _gdn_layer(grp, x, norm_g, mods, conv0, ssm0, w_in, conv_w, a_log, dt_bias, norm_w, w_o, chunk,
               seqs_per_step=1, route=None):
    b, t = grp.b, grp.t
    shift, scale, gate = mods
    sh, sh_spec = grp.rowmod(shift)
    sc, sc_spec = grp.rowmod(scale)
    kd = GD_H * GD_DK
    cstates, cspecs = [], []
    for d in (1, 2, 3):
        if conv0 is None:
            cs = jnp.zeros((b, 1, GD_C), F32)
        else:
            cs = jnp.concatenate([conv0[:, 3 - d:, :], jnp.zeros((b, t - d, GD_C), F32)], axis=1)
        cs, spec = grp.rowseq(cs)
        cstates.append(cs)
        cspecs.append(spec)
    bf = lambda z: z.astype(BF16)
    pad128 = lambda z: jnp.pad(z, ((0, 0), (0, LANES - z.shape[1])))
    w_qkv = bf(w_in[:, :GD_C])
    w_z = bf(w_in[:, GD_C:GD_C + kd])
    w_b = bf(pad128(w_in[:, GD_C + kd:GD_C + kd + GD_H]))
    w_a = bf(pad128(w_in[:, GD_C + kd + GD_H:]))
    weights = [w_qkv, w_z, w_b, w_a, conv_w, pad128(a_log.reshape(1, GD_H)), pad128(dt_bias.reshape(1, GD_H))]
    qkv, z, beta, gdec = pl.pallas_call(
        functools.partial(_gdn_proj_kernel, t),
        out_shape=(jax.ShapeDtypeStruct((grp.n, GD_C), BF16), jax.ShapeDtypeStruct((grp.n, kd), BF16),
                   jax.ShapeDtypeStruct((grp.n, LANES), F32), jax.ShapeDtypeStruct((grp.n, LANES), F32)),
        grid=(grp.tiles,),
        in_specs=[grp.rows(D), grp.prev8(D), _full((1, D)), sh_spec, sc_spec] + cspecs
        + [_full(z_.shape) for z_ in weights],
        out_specs=(grp.rows(GD_C), grp.rows(kd), grp.rows(LANES), grp.rows(LANES)),
        scratch_shapes=[pltpu.VMEM((grp.tm + 8, GD_C), F32)],
        compiler_params=_cparams(("arbitrary",)),
        name="gdn_proj",
    )(x, x, norm_g.reshape(1, D), sh, sc, *cstates, *weights)
    tp = ((t + chunk - 1) // chunk) * chunk
    nc = tp // chunk
    qkv_p, z_p = _pad_time(qkv, b, t, tp), _pad_time(z, b, t, tp)
    beta_p, g_p = _pad_time(beta, b, t, tp), _pad_time(gdec, b, t, tp)
    if ssm0 is None:
        ssm0 = jnp.zeros((b, GD_H, GD_DK, GD_DV), F32)
    nb = seqs_per_step
    assert b % nb == 0
    seq3 = lambda z_: z_.reshape(b, tp, z_.shape[-1])
    col = lambda j: pl.BlockSpec((nb, chunk, kd), lambda bi, c: (bi, c, j))
    lan = pl.BlockSpec((nb, chunk, LANES), lambda bi, c: (bi, c, 0))
    stt = pl.BlockSpec((nb, GD_H, GD_DK, GD_DV), lambda bi, c: (bi, 0, 0, 0))
    qkv3 = seq3(qkv_p)
    y, st = pl.pallas_call(
        _gdn_core_kernel,
        out_shape=(jax.ShapeDtypeStruct((b, tp, kd), BF16),
                   jax.ShapeDtypeStruct((b, GD_H, GD_DK, GD_DV), F32)),
        grid=(b // nb, nc),
        in_specs=[col(0), col(1), col(2), col(0), lan, lan, stt, _full((1, GD_DV))],
        out_specs=(col(0), stt),
        scratch_shapes=[pltpu.VMEM((nb, GD_H, GD_DK, GD_DV), F32)],
        compiler_params=_cparams(("arbitrary", "arbitrary")),
        name="gdn_core",
    )(qkv3, qkv3, qkv3, seq3(z_p), seq3(beta_p), seq3(g_p), ssm0, norm_w.reshape(1, GD_DV))
    y = y.reshape(b * tp, kd)
    x_new = _finish(grp, route, x, _unpad_time(y, b, t, tp), bf(w_o), gate)
    nl = min(t, GD_CONV - 1)
    x_last = x.reshape(b, t, D)[:, t - nl:].reshape(b * nl, D)
    rep = lambda m: jnp.repeat(m, nl, axis=0)
    pre_last = _modrows_call(x_last, norm_g, rep(shift), rep(scale), w_qkv).reshape(b, nl, GD_C)
    if nl < GD_CONV - 1:
        pre_last = jnp.concatenate([conv0[:, nl:], pre_last], axis=1)
    return x_new, pre_last, st


def _ret_proj_kernel(x_ref, g_ref, sh_ref, sc_ref, cos_ref, sin_ref, w_ref, q_ref, k_ref, v_ref, gate_ref):
    h = _modulate(x_ref[...], g_ref[...], sh_ref[0], sc_ref[0]).astype(BF16)
    kd = RT_H * RT_DK
    vd = RT_H * RT_DV
    cos, sin = cos_ref[0], sin_ref[0]
    even = (lax.broadcasted_iota(I32, (1, kd), 1) & 1) == 0

    def rotary(z):
        swapped = jnp.where(even, pltpu.roll(z, kd - 1, 1), pltpu.roll(z, 1, 1))
        return z * cos + swapped * sin

    q_ref[...] = rotary(jnp.dot(h, w_ref[:, 0:kd], preferred_element_type=F32)).astype(BF16)
    k = rotary(jnp.dot(h, w_ref[:, kd:2 * kd], preferred_element_type=F32))
    k_ref[...] = (k * (RT_DK ** -0.5)).astype(BF16)
    v_ref[...] = jnp.dot(h, w_ref[:, 2 * kd:2 * kd + vd], preferred_element_type=F32).astype(BF16)
    gate_ref[...] = jnp.dot(h, w_ref[:, 2 * kd + vd:], preferred_element_type=F32).astype(BF16)


def _ret_core_kernel(q_ref, k_ref, v_ref, gate_ref, dm_ref, qd_ref, kd_ref, cd_ref, s0_ref, nw_ref,
                     y_ref, st_ref, state):
    cidx = pl.program_id(1)
    nb = q_ref.shape[0]

    @pl.when(cidx == 0)
    def _():
        state[...] = s0_ref[...]

    kss = [slice(h * RT_DK, (h + 1) * RT_DK) for h in range(RT_H)]
    vss = [slice(h * RT_DV, (h + 1) * RT_DV) for h in range(RT_H)]
    pairs = [(s, h) for s in range(nb) for h in range(RT_H)]
    n = range(len(pairs))
    q = [q_ref[s, :, kss[h]] for s, h in pairs]
    k = [k_ref[s, :, kss[h]] for s, h in pairs]
    v = [v_ref[s, :, vss[h]] for s, h in pairs]
    st = [state[s, h] for s, h in pairs]
    inner = [_bdot_nt(q[i], k[i]) * dm_ref[pairs[i][1]] for i in n]
    cross = [_bdot(q[i], st[i]) * qd_ref[pairs[i][1]] for i in n]
    o = [_bdot(inner[i], v[i]) + cross[i] for i in n]
    s_new = [st[i] * cd_ref[pairs[i][1]] + _bdot_tn(k[i].astype(F32) * kd_ref[pairs[i][1]], v[i]) for i in n]
    for i, (s, h) in enumerate(pairs):
        state[s, h] = s_new[i]
        on = o[i] * lax.rsqrt(jnp.mean(o[i] * o[i], axis=-1, keepdims=True) + NORM_EPS) * nw_ref[:, vss[h]]
        y_ref[s, :, vss[h]] = (on * _silu(gate_ref[s, :, vss[h]].astype(F32))).astype(BF16)

    @pl.when(cidx == pl.num_programs(1) - 1)
    def _():
        st_ref[...] = state[...]


def _ret_layer(grp, x, norm_g, mods, s0, pos0, w_in, norm_w, w_o, chunk, seqs_per_step=1, route=None):
    b, t = grp.b, grp.t
    shift, scale, gate = mods
    sh, sh_spec = grp.rowmod(shift)
    sc, sc_spec = grp.rowmod(scale)
    kd, vd = RT_H * RT_DK, RT_H * RT_DV
    half = RT_DK // 2
    inv = 1.0 / (10000.0 ** jnp.linspace(0.0, 1.0, half, dtype=F32))
    pos = jnp.arange(t, dtype=F32) + float(pos0)
    ang = pos[:, None] * inv[None, :]
    cos = jnp.repeat(jnp.cos(ang), 2, axis=1)
    sin = jnp.stack([-jnp.sin(ang), jnp.sin(ang)], axis=-1).reshape(t, RT_DK)
    cos4, cos_spec = grp.postab(jnp.tile(cos, (1, RT_H)))
    sin4, sin_spec = grp.postab(jnp.tile(sin, (1, RT_H)))
    wb = w_in.astype(BF16)
    q, k, v, gt = pl.pallas_call(
        _ret_proj_kernel,
        out_shape=(jax.ShapeDtypeStruct((grp.n, kd), BF16), jax.ShapeDtypeStruct((grp.n, kd), BF16),
                   jax.ShapeDtypeStruct((grp.n, vd), BF16), jax.ShapeDtypeStruct((grp.n, vd), BF16)),
        grid=(grp.tiles,),
        in_specs=[grp.rows(D), _full((1, D)), sh_spec, sc_spec, cos_spec, sin_spec, _full(wb.shape)],
        out_specs=(grp.rows(kd), grp.rows(kd), grp.rows(vd), grp.rows(vd)),
        compiler_params=_cparams(("arbitrary",)),
        name="ret_proj",
    )(x, norm_g.reshape(1, D), sh, sc, cos4, sin4, wb)
    tp = ((t + chunk - 1) // chunk) * chunk
    nc = tp // chunk
    nv = min(t, chunk)
    assert tp == t or nc == 1
    log_gamma = jnp.log1p(-jnp.exp2(-5.0 - jnp.arange(RT_H, dtype=F32)))
    idx = jnp.arange(chunk, dtype=F32)
    diff = idx[:, None] - idx[None, :]
    dmask = jnp.where(diff >= 0, jnp.exp(log_gamma[:, None, None] * jnp.maximum(diff, 0.0)), 0.0)
    q_dec = jnp.exp(log_gamma[:, None] * (idx + 1.0))[:, :, None]
    k_dec = jnp.exp(log_gamma[:, None] * jnp.maximum(nv - 1.0 - idx, 0.0))[:, :, None]
    c_dec = jnp.exp(log_gamma * nv)[:, None, None]
    if s0 is None:
        s0 = jnp.zeros((b, RT_H, RT_DK, RT_DV), F32)
    nb = seqs_per_step
    assert b % nb == 0
    seq3 = lambda z_: _pad_time(z_, b, t, tp).reshape(b, tp, z_.shape[-1])
    rowk = pl.BlockSpec((nb, chunk, kd), lambda bi, c: (bi, c, 0))
    rowv = pl.BlockSpec((nb, chunk, vd), lambda bi, c: (bi, c, 0))
    stt = pl.BlockSpec((nb, RT_H, RT_DK, RT_DV), lambda bi, c: (bi, 0, 0, 0))
    y, st = pl.pallas_call(
        _ret_core_kernel,
        out_shape=(jax.ShapeDtypeStruct((b, tp, vd), BF16),
                   jax.ShapeDtypeStruct((b, RT_H, RT_DK, RT_DV), F32)),
        grid=(b // nb, nc),
        in_specs=[rowk, rowk, rowv, rowv, _full(dmask.shape), _full(q_dec.shape), _full(k_dec.shape),
                  _full(c_dec.shape), stt, _full((1, vd))],
        out_specs=(rowv, stt),
        scratch_shapes=[pltpu.VMEM((nb, RT_H, RT_DK, RT_DV), F32)],
        compiler_params=_cparams(("arbitrary", "arbitrary")),
        name="ret_core",
    )(seq3(q), seq3(k), seq3(v), seq3(gt), dmask, q_dec, k_dec, c_dec, s0, norm_w.reshape(1, vd))
    y = y.reshape(b * tp, vd)
    x_new = _finish(grp, route, x, _unpad_time(y, b, t, tp), w_o.astype(BF16), gate)
    return x_new, st


def _hgrn_proj_kernel(layer, x_ref, g_ref, sh_ref, sc_ref, lbl_ref, w_ref,
                      q_ref, k_ref, lf_ref, v_ref, gate_ref):
    h = _modulate(x_ref[...], g_ref[...], sh_ref[0], sc_ref[0]).astype(BF16)
    ed = HG_H * HG_E
    logits = lbl_ref[...]
    e = jnp.exp(logits - jnp.max(logits, axis=0, keepdims=True))
    lrow = lax.broadcasted_iota(I32, logits.shape, 0)
    part = jnp.where(jnp.logical_and(lrow >= 1, lrow <= layer), e, 0.0)
    lb = jnp.sum(part, axis=0, keepdims=True) / jnp.sum(e, axis=0, keepdims=True)
    q_ref[...] = jnp.dot(h, w_ref[:, 0:ed], preferred_element_type=F32)
    f = lb + (1.0 - lb) * _sigmoid(jnp.dot(h, w_ref[:, ed:2 * ed], preferred_element_type=F32))
    k_ref[...] = 1.0 - f
    lf_ref[...] = jnp.log(f)
    v_ref[...] = jnp.dot(h, w_ref[:, 2 * ed:3 * ed], preferred_element_type=F32).astype(BF16)
    gate_ref[...] = jnp.dot(h, w_ref[:, 3 * ed:], preferred_element_type=F32).astype(BF16)


def _hgrn_core_kernel(q_ref, k_ref, lf_ref, v_ref, gate_ref, s0_ref, nw_ref, y_ref, st_ref, state):
    cidx = pl.program_id(1)
    nb, c = q_ref.shape[0], q_ref.shape[1]

    @pl.when(cidx == 0)
    def _():
        state[...] = s0_ref[...]

    ri = lax.broadcasted_iota(I32, (c, c), 0)
    ci = lax.broadcasted_iota(I32, (c, c), 1)
    ltri = (ri >= ci).astype(F32)
    row8 = lax.broadcasted_iota(I32, (8, 1), 0)
    hsl = [slice(h * HG_E, (h + 1) * HG_E) for h in range(HG_H)]
    pairs = [(s, h) for s in range(nb) for h in range(HG_H)]
    heads = range(len(pairs))
    sls = [hsl[h] for _, h in pairs]
    q = [q_ref[s, :, hsl[h]] for s, h in pairs]
    k = [k_ref[s, :, hsl[h]] for s, h in pairs]
    v = [v_ref[s, :, hsl[h]].astype(F32) for s, h in pairs]
    st = [state[s, h] for s, h in pairs]
    cum = [_fdot(ltri, lf_ref[s, :, hsl[h]]) for s, h in pairs]
    inter = [_bdot_nt(q[h] * jnp.exp(cum[h]), st[h]) for h in heads]
    last = [cum[h][c - 1:c, :] for h in heads]
    s_new = [st[h] * jnp.exp(last[h]) + _bdot_tn(v[h], k[h] * jnp.exp(last[h] - cum[h])) for h in heads]
    for h in heads:
        seq = pairs[h][0]
        state[seq, pairs[h][1]] = s_new[h]
        parts = []
        for g0 in range(0, c, 8):
            qg, cg = q[h][g0:g0 + 8], cum[h][g0:g0 + 8]
            acc = inter[h][g0:g0 + 8]
            for j in range(g0 + 8):
                diff = cg - cum[h][j:j + 1, :]
                if j >= g0:
                    causal = row8 >= (j - g0)
                    diff = jnp.where(causal, diff, 0.0)
                col = jnp.sum(qg * k[h][j:j + 1, :] * jnp.exp(diff), axis=-1, keepdims=True)
                if j >= g0:
                    col = jnp.where(causal, col, 0.0)
                acc = acc + col * v[h][j:j + 1, :]
            parts.append(acc)
        o = parts[0] if len(parts) == 1 else jnp.concatenate(parts, axis=0)
        on = o * lax.rsqrt(jnp.mean(o * o, axis=-1, keepdims=True) + NORM_EPS) * nw_ref[:, sls[h]]
        y_ref[seq, :, sls[h]] = (on * _silu(gate_ref[seq, :, sls[h]].astype(F32))).astype(BF16)

    @pl.when(cidx == pl.num_programs(1) - 1)
    def _():
        st_ref[...] = state[...]


def _hgrn_layer(grp, x, norm_g, mods, s0, layer, lb_logits, w_in, norm_w, w_o, chunk, seqs_per_step=1,
                route=None):
    b, t = grp.b, grp.t
    shift, scale, gate = mods
    sh, sh_spec = grp.rowmod(shift)
    sc, sc_spec = grp.rowmod(scale)
    ed, vd = HG_H * HG_E, HG_H * HG_DV
    wb = w_in.astype(BF16)
    q, k, lf, v, gt = pl.pallas_call(
        functools.partial(_hgrn_proj_kernel, layer),
        out_shape=(jax.ShapeDtypeStruct((grp.n, ed), F32), jax.ShapeDtypeStruct((grp.n, ed), F32),
                   jax.ShapeDtypeStruct((grp.n, ed), F32), jax.ShapeDtypeStruct((grp.n, vd), BF16),
                   jax.ShapeDtypeStruct((grp.n, vd), BF16)),
        grid=(grp.tiles,),
        in_specs=[grp.rows(D), _full((1, D)), sh_spec, sc_spec, _full(lb_logits.shape), _full(wb.shape)],
        out_specs=(grp.rows(ed), grp.rows(ed), grp.rows(ed), grp.rows(vd), grp.rows(vd)),
        compiler_params=_cparams(("arbitrary",)),
        name="hgrn_proj",
    )(x, norm_g.reshape(1, D), sh, sc, lb_logits, wb)
    tp = ((t + chunk - 1) // chunk) * chunk
    nc = tp // chunk
    if s0 is None:
        s0 = jnp.zeros((b, HG_H, HG_E, HG_DV), F32)
    nb = seqs_per_step
    assert b % nb == 0
    seq3 = lambda z_: _pad_time(z_, b, t, tp).reshape(b, tp, z_.shape[-1])
    row = pl.BlockSpec((nb, chunk, ed), lambda bi, c: (bi, c, 0))
    stt = pl.BlockSpec((nb, HG_H, HG_E, HG_DV), lambda bi, c: (bi, 0, 0, 0))
    y, st = pl.pallas_call(
        _hgrn_core_kernel,
        out_shape=(jax.ShapeDtypeStruct((b, tp, vd), BF16),
                   jax.ShapeDtypeStruct((b, HG_H, HG_E, HG_DV), F32)),
        grid=(b // nb, nc),
        in_specs=[row, row, row, row, row, stt, _full((1, vd))],
        out_specs=(row, stt),
        scratch_shapes=[pltpu.VMEM((nb, HG_H, HG_E, HG_DV), F32)],
        compiler_params=_cparams(("arbitrary", "arbitrary")),
        name="hgrn_core",
    )(seq3(q), seq3(k), seq3(lf), seq3(v), seq3(gt), jnp.swapaxes(s0, 2, 3), norm_w.reshape(1, vd))
    y = y.reshape(b * tp, vd)
    x_new = _finish(grp, route, x, _unpad_time(y, b, t, tp), w_o.astype(BF16), gate)
    return x_new, jnp.swapaxes(st, 2, 3)


ROW_TILE = 256
GDN_CHUNK, RET_CHUNK, HGRN_CHUNK = 64, 128, 16
SAMPLE_CHUNK = 16
GDN_SEQS, RET_SEQS, HGRN_SEQS = (4, 8), (2, 2), (4, 4)
PAST_LEN = 16384


def kernel(x_prompt, x_sample, c_prompt, c_sample, state_rwkv_wkv, state_rwkv_shift, state_gdn_ssm, state_gdn_conv, state_ret, state_hgrn, ada_w, ada_b, norm_mix, norm_ffn, norm_final, rwkv_mu, rwkv_w_rkv, rwkv_w0, rwkv_w1, rwkv_w2, rwkv_a0, rwkv_a1, rwkv_a2, rwkv_g1, rwkv_g2, rwkv_k_k, rwkv_k_a, rwkv_r_k, rwkv_ln_w, rwkv_ln_b, rwkv_w_o, gdn_w_in, gdn_conv_w, gdn_a_log, gdn_dt_bias, gdn_norm_w, gdn_w_o, ret_w_in, ret_norm_w, ret_w_o, hgrn_w_in, hgrn_lb_logits, hgrn_norm_w, hgrn_w_o, moe_w_router, moe_b_router, moe_w_gu, moe_b_gu, moe_w_down, moe_b_down):
    bp, tp, _ = x_prompt.shape
    bs, ts, _ = x_sample.shape
    gp, gs = _Group(bp, tp, ROW_TILE), _Group(bs, ts, ROW_TILE)
    ada = _ada_call(jnp.concatenate([c_prompt, c_sample], axis=0), ada_w, ada_b)
    xp = x_prompt.reshape(bp * tp, D)
    xs = x_sample.reshape(bs * ts, D)
    outs_p = {k: [] for k in ("wkv", "shift", "ssm", "conv", "ret", "hgrn")}
    outs_s = {k: [] for k in ("wkv", "shift", "ssm", "conv", "ret", "hgrn")}
    for i in range(DEPTH):
        kind, j = i % 4, i // 4
        m = ada[i].reshape(bp + bs, 6, D)
        mod_p = [m[:bp, n] for n in range(6)]
        mod_s = [m[bp:, n] for n in range(6)]
        g = norm_mix[i]
        rt_p = (norm_ffn[i], mod_p[3], mod_p[4], moe_w_router[i], moe_b_router[i])
        rt_s = (norm_ffn[i], mod_s[3], mod_s[4], moe_w_router[i], moe_b_router[i])
        route_p = (jnp.zeros((1, N_EXPERTS), F32),) + rt_p
        if kind == 0:
            prm = (rwkv_mu[j], rwkv_w_rkv[j], rwkv_w0[j], rwkv_w1[j], rwkv_w2[j], rwkv_a0[j], rwkv_a1[j],
                   rwkv_a2[j], rwkv_g1[j], rwkv_g2[j], rwkv_k_k[j], rwkv_k_a[j], rwkv_r_k[j],
                   rwkv_ln_w[j], rwkv_ln_b[j], rwkv_w_o[j])
            xp, sh_p, wkv_p = _rwkv_layer(gp, xp, g, mod_p[:3], None, None, *prm, route=route_p)
            xs, sh_s, wkv_s = _rwkv_layer(gs, xs, g, mod_s[:3], state_rwkv_shift[j], state_rwkv_wkv[j], *prm,
                                          route=(xp[5],) + rt_s)
            outs_p["wkv"].append(wkv_p); outs_p["shift"].append(sh_p)
            outs_s["wkv"].append(wkv_s); outs_s["shift"].append(sh_s)
        elif kind == 1:
            prm = (gdn_w_in[j], gdn_conv_w[j], gdn_a_log[j], gdn_dt_bias[j], gdn_norm_w[j], gdn_w_o[j])
            xp, cv_p, ss_p = _gdn_layer(gp, xp, g, mod_p[:3], None, None, *prm, GDN_CHUNK, GDN_SEQS[0],
                                        route=route_p)
            xs, cv_s, ss_s = _gdn_layer(gs, xs, g, mod_s[:3], state_gdn_conv[j], state_gdn_ssm[j], *prm,
                                        SAMPLE_CHUNK, GDN_SEQS[1], route=(xp[5],) + rt_s)
            outs_p["ssm"].append(ss_p); outs_p["conv"].append(cv_p)
            outs_s["ssm"].append(ss_s); outs_s["conv"].append(cv_s)
        elif kind == 2:
            prm = (ret_w_in[j], ret_norm_w[j], ret_w_o[j])
            xp, r_p = _ret_layer(gp, xp, g, mod_p[:3], None, 0, *prm, RET_CHUNK, RET_SEQS[0], route=route_p)
            xs, r_s = _ret_layer(gs, xs, g, mod_s[:3], state_ret[j], PAST_LEN, *prm, SAMPLE_CHUNK,
                                  RET_SEQS[1], route=(xp[5],) + rt_s)
            outs_p["ret"].append(r_p); outs_s["ret"].append(r_s)
        else:
            prm = (i, hgrn_lb_logits, hgrn_w_in[j], hgrn_norm_w[j], hgrn_w_o[j])
            xp, h_p = _hgrn_layer(gp, xp, g, mod_p[:3], None, *prm, HGRN_CHUNK, HGRN_SEQS[0], route=route_p)
            xs, h_s = _hgrn_layer(gs, xs, g, mod_s[:3], state_hgrn[j], *prm, SAMPLE_CHUNK, HGRN_SEQS[1],
                                   route=(xp[5],) + rt_s)
            outs_p["hgrn"].append(h_p); outs_s["hgrn"].append(h_s)
        last = i == DEPTH - 1
        xp, xs = _moe(i, [gp, gs], [xp, xs], [mod_p[5], mod_s[5]], moe_w_gu, moe_b_gu, moe_w_down, moe_b_down,
                      norm_final if last else None)
    y_prompt = xp[1].reshape(bp, tp, D)
    y_sample = xs[1].reshape(bs, ts, D)
    order = ("wkv", "shift", "ssm", "conv", "ret", "hgrn")
    return ((y_prompt, y_sample) + tuple(jnp.stack(outs_p[k]) for k in order)
            + tuple(jnp.stack(outs_s[k]) for k in order))
```
